```python
import jax, jax.numpy as jnp
from jax import lax
import numpy as np

D_MODEL = 2048
BATCH = 8
SEQ = 4096
DEPTH = 4

N_MIXERS = 4
RMS_EPS = 1e-6
HGRN_EXPAND = 128
HGRN_HEADS = D_MODEL // HGRN_EXPAND
HGRN_DK = HGRN_EXPAND
HGRN_DV = D_MODEL // HGRN_HEADS
HGRN_WIDTH = HGRN_HEADS * HGRN_DK
HGRN_CHUNK = 64
SWA_HEAD_DIM = 64
SWA_Q_HEADS = D_MODEL // SWA_HEAD_DIM
SWA_KV_HEADS = SWA_Q_HEADS // 8
SWA_WINDOW = 128
SCONV_WIDTH = 3
FOX_HEAD_DIM = 64
FOX_HEADS = D_MODEL // FOX_HEAD_DIM
FOX_BLOCK = 128
ROPE_THETA = 500000.0
ROT_DIM = SWA_HEAD_DIM // 4
D_FF = 5632
FFN_CONV_WIDTH = 3

kernel_name = "hybrid_interleaved_hgrn2_swa_sconv_fox"


def rmsnorm(x, g):
    xf = x.astype(jnp.float32)
    y = xf * lax.rsqrt(jnp.mean(xf * xf, axis=-1, keepdims=True) + RMS_EPS)
    return (y * g.astype(jnp.float32)).astype(x.dtype)


def causal_dwconv(x, w):
    K, C = w.shape
    return lax.conv_general_dilated(
        x, w[:, None, :].astype(x.dtype), window_strides=(1,), padding=[(K - 1, 0)],
        dimension_numbers=("NWC", "WIO", "NWC"), feature_group_count=C)


def partial_rope(x, positions):
    half = ROT_DIM // 2
    inv_freq = ROPE_THETA ** (-jnp.arange(half, dtype=jnp.float32) / half)
    ang = positions.astype(jnp.float32)[:, None] * inv_freq[None, :]
    cos = jnp.cos(ang)[None, :, None, :]
    sin = jnp.sin(ang)[None, :, None, :]
    xf = x.astype(jnp.float32)
    x1, x2 = xf[..., :half], xf[..., half:ROT_DIM]
    out = jnp.concatenate([x1 * cos - x2 * sin, x2 * cos + x1 * sin, xf[..., ROT_DIM:]], axis=-1)
    return out.astype(x.dtype)


def hgrn2_mixer(h, w_in, w_out, norm_g, lb):
    Bsz, T, _ = h.shape
    C = HGRN_CHUNK
    nC = T // C
    q, f, i, g = jnp.split(h @ w_in, 4, axis=-1)
    q = jax.nn.silu(q.astype(jnp.float32))
    f = lb + (1.0 - lb) * jax.nn.sigmoid(f.astype(jnp.float32))
    log_f = jnp.log(f)
    k = 1.0 - f
    v = i.astype(jnp.float32)

    def to_chunks(a, d):
        return a.reshape(Bsz, nC, C, HGRN_HEADS, d).transpose(1, 0, 3, 2, 4)

    qc, kc, vc = to_chunks(q, HGRN_DK), to_chunks(k, HGRN_DK), to_chunks(v, HGRN_DV)
    bc = jnp.cumsum(to_chunks(log_f, HGRN_DK), axis=3)
    causal = jnp.tril(jnp.ones((C, C), dtype=bool))

    def chunk_step(S, inp):
        qb, kb, vb, bb = inp
        inter = jnp.einsum("bhtk,bhkv->bhtv", qb * jnp.exp(bb), S)
        rel = jnp.where(causal[:, :, None], bb[:, :, :, None, :] - bb[:, :, None, :, :], -jnp.inf)
        A = jnp.einsum("bhtk,bhsk,bhtsk->bhts", qb, kb, jnp.exp(rel))
        intra = jnp.einsum("bhts,bhsv->bhtv", A, vb)
        b_last = bb[:, :, -1:, :]
        S_new = jnp.exp(b_last[:, :, 0, :])[..., None] * S + jnp.einsum(
            "bhsk,bhsv->bhkv", kb * jnp.exp(b_last - bb), vb)
        return S_new, inter + intra

    S0 = jnp.zeros((Bsz, HGRN_HEADS, HGRN_DK, HGRN_DV), jnp.float32)
    _, o = lax.scan(chunk_step, S0, (qc, kc, vc, bc))
    o = o.transpose(1, 0, 3, 2, 4).reshape(Bsz, T, HGRN_HEADS, HGRN_DV)
    o = rmsnorm(o, norm_g).reshape(Bsz, T, HGRN_HEADS * HGRN_DV)
    o = o * jax.nn.silu(g.astype(jnp.float32))
    return o.astype(h.dtype) @ w_out


def swa_sink_mixer(h, positions, w_in, w_out, sinks):
    Bsz, T, _ = h.shape
    W, d = SWA_WINDOW, SWA_HEAD_DIM
    KV, G = SWA_KV_HEADS, SWA_Q_HEADS // SWA_KV_HEADS
    nblk = T // W
    q, k, v = jnp.split(h @ w_in, [SWA_Q_HEADS * d, SWA_Q_HEADS * d + KV * d], axis=-1)
    q = partial_rope(q.reshape(Bsz, T, SWA_Q_HEADS, d), positions)
    k = partial_rope(k.reshape(Bsz, T, KV, d), positions)
    v = v.reshape(Bsz, T, KV, d)
    qb = q.reshape(Bsz, nblk, W, KV, G, d)

    def band(a):
        cur = a.reshape(Bsz, nblk, W, KV, d)
        prev = jnp.concatenate([jnp.zeros_like(cur[:, :1]), cur[:, :-1]], axis=1)
        return jnp.concatenate([prev, cur], axis=2)

    kb, vb = band(k), band(v)
    s = jnp.einsum("bnqhgd,bnkhd->bnhgqk", qb, kb).astype(jnp.float32) * (d ** -0.5)
    qi = jnp.arange(W)[:, None]
    kj = jnp.arange(2 * W)[None, :]
    diff = qi + W - kj
    blk = jnp.arange(nblk)[:, None, None]
    allowed = (diff >= 0) & (diff < W) & (blk * W + kj - W >= 0)
    s = jnp.where(allowed[None, :, None, None], s, -jnp.inf)
    sink = sinks.astype(jnp.float32).reshape(KV, G)[None, None, :, :, None, None]
    m = jnp.maximum(jnp.max(s, axis=-1, keepdims=True), sink)
    e = jnp.exp(s - m)
    p = e / (jnp.sum(e, axis=-1, keepdims=True) + jnp.exp(sink - m))
    o = jnp.einsum("bnhgqk,bnkhd->bnqhgd", p.astype(h.dtype), vb)
    return o.reshape(Bsz, T, SWA_Q_HEADS * d) @ w_out


def short_conv_mixer(h, w_in, conv_w, w_out):
    b_gate, c_gate, xv = jnp.split(h @ w_in, 3, axis=-1)
    return (b_gate * causal_dwconv(c_gate * xv, conv_w)) @ w_out


def fox_mixer(h, w_in, b_f, w_out):
    Bsz, T, _ = h.shape
    H, d, W = FOX_HEADS, FOX_HEAD_DIM, FOX_BLOCK
    width = H * d
    nblk = T // W
    q, k, v, f_logit, g = jnp.split(h @ w_in, [width, 2 * width, 3 * width, 3 * width + H], axis=-1)
    q = q.reshape(Bsz, T, H, d)
    k = k.reshape(Bsz, T, H, d)
    v = v.reshape(Bsz, T, H, d)
    log_f = jax.nn.log_sigmoid(f_logit.astype(jnp.float32) + b_f.astype(jnp.float32))
    c = jnp.cumsum(log_f, axis=1).transpose(0, 2, 1)
    key_pos = jnp.arange(T)

    def q_block(n):
        start = n * W
        qs = lax.dynamic_slice_in_dim(q, start, W, axis=1)
        cq = lax.dynamic_slice_in_dim(c, start, W, axis=2)
        s = jnp.einsum("bqhd,bkhd->bhqk", qs, k).astype(jnp.float32) * (d ** -0.5)
        s = s + cq[..., None] - c[:, :, None, :]
        q_pos = start + jnp.arange(W)
        s = jnp.where((key_pos[None, :] <= q_pos[:, None])[None, None], s, -jnp.inf)
        p = jax.nn.softmax(s, axis=-1)
        return jnp.einsum("bhqk,bkhd->bqhd", p.astype(v.dtype), v)

    o = lax.map(q_block, jnp.arange(nblk))
    o = o.transpose(1, 0, 2, 3, 4).reshape(Bsz, T, width)
    o = o * jax.nn.sigmoid(g.astype(jnp.float32)).astype(o.dtype)
    return o @ w_out


def conv_glu_ffn(h, w_up, conv_w, conv_b, w_down):
    u = causal_dwconv(h @ w_up, conv_w) + conv_b.astype(h.dtype)
    gate, up = jnp.split(u, 2, axis=-1)
    return (jax.nn.silu(gate) * up) @ w_down


def _fwd_setup_inputs(seed: int = 0) -> dict:
    key = jax.random.key(seed)
    ks = iter(jax.random.split(key, 32))

    def nrm(shape, scale):
        return scale * jax.random.normal(next(ks), shape, jnp.float32)

    n_of = [len(range(m, DEPTH, N_MIXERS)) for m in range(N_MIXERS)]
    nA, nB, nC, nD = n_of
    D = D_MODEL
    sd = D ** -0.5
    return {
        "x": nrm((BATCH, SEQ, D), 1.0),
        "positions": jnp.arange(SEQ, dtype=jnp.int32),
        "mix_pre_g": 1.0 + nrm((DEPTH, D), 0.05),
        "mix_post_g": 1.0 + nrm((DEPTH, D), 0.05),
        "ffn_pre_g": 1.0 + nrm((DEPTH, D), 0.05),
        "ffn_post_g": 1.0 + nrm((DEPTH, D), 0.05),
        "hgrn_w_in": nrm((nA, D, 3 * HGRN_WIDTH + HGRN_HEADS * HGRN_DV), sd),
        "hgrn_w_out": nrm((nA, HGRN_HEADS * HGRN_DV, D), (HGRN_HEADS * HGRN_DV) ** -0.5),
        "hgrn_norm_g": 1.0 + nrm((nA, HGRN_DV), 0.05),
        "hgrn_lb_param": nrm((DEPTH + 1, HGRN_WIDTH), 0.5),
        "swa_w_in": nrm((nB, D, (SWA_Q_HEADS + 2 * SWA_KV_HEADS) * SWA_HEAD_DIM), sd),
        "swa_w_out": nrm((nB, SWA_Q_HEADS * SWA_HEAD_DIM, D), (SWA_Q_HEADS * SWA_HEAD_DIM) ** -0.5),
        "swa_sinks": nrm((nB, SWA_Q_HEADS), 0.5),
        "sc_w_in": nrm((nC, D, 3 * D), sd),
        "sc_conv_w": nrm((nC, SCONV_WIDTH, D), SCONV_WIDTH ** -0.5),
        "sc_w_out": nrm((nC, D, D), sd),
        "fox_w_in": nrm((nD, D, 4 * FOX_HEADS * FOX_HEAD_DIM + FOX_HEADS), sd),
        "fox_b_f": 3.0 + nrm((nD, FOX_HEADS), 0.5),
        "fox_w_out": nrm((nD, FOX_HEADS * FOX_HEAD_DIM, D), (FOX_HEADS * FOX_HEAD_DIM) ** -0.5),
        "ffn_w_up": nrm((DEPTH, D, 2 * D_FF), sd),
        "ffn_conv_w": nrm((DEPTH, FFN_CONV_WIDTH, 2 * D_FF), FFN_CONV_WIDTH ** -0.5),
        "ffn_conv_b": nrm((DEPTH, 2 * D_FF), 0.02),
        "ffn_w_down": nrm((DEPTH, D_FF, D), D_FF ** -0.5),
    }


def _fwd_reference(x, positions, mix_pre_g, mix_post_g, ffn_pre_g, ffn_post_g,
              hgrn_w_in, hgrn_w_out, hgrn_norm_g, hgrn_lb_param,
              swa_w_in, swa_w_out, swa_sinks,
              sc_w_in, sc_conv_w, sc_w_out,
              fox_w_in, fox_b_f, fox_w_out,
              ffn_w_up, ffn_conv_w, ffn_conv_b, ffn_w_down):
    lb_table = jnp.cumsum(jax.nn.softmax(hgrn_lb_param.astype(jnp.float32), axis=0), axis=0)
    for i in range(DEPTH):
        m, j = i % N_MIXERS, i // N_MIXERS
        hn = rmsnorm(x, mix_pre_g[i])
        if m == 0:
            y = hgrn2_mixer(hn, hgrn_w_in[j], hgrn_w_out[j], hgrn_norm_g[j], lb_table[i])
        elif m == 1:
            y = swa_sink_mixer(hn, positions, swa_w_in[j], swa_w_out[j], swa_sinks[j])
        elif m == 2:
            y = short_conv_mixer(hn, sc_w_in[j], sc_conv_w[j], sc_w_out[j])
        else:
            y = fox_mixer(hn, fox_w_in[j], fox_b_f[j], fox_w_out[j])
        x = x + rmsnorm(y.astype(x.dtype), mix_post_g[i])
        hn = rmsnorm(x, ffn_pre_g[i])
        y = conv_glu_ffn(hn, ffn_w_up[i], ffn_conv_w[i], ffn_conv_b[i], ffn_w_down[i])
        x = x + rmsnorm(y.astype(x.dtype), ffn_post_g[i])
    return x


import jax as _jax
import jax.numpy as _jnp

TWIN_FORMAT = 'train_step'
FWD_PARAMS = ['x', 'positions', 'mix_pre_g', 'mix_post_g', 'ffn_pre_g', 'ffn_post_g', 'hgrn_w_in', 'hgrn_w_out', 'hgrn_norm_g', 'hgrn_lb_param', 'swa_w_in', 'swa_w_out', 'swa_sinks', 'sc_w_in', 'sc_conv_w', 'sc_w_out', 'fox_w_in', 'fox_b_f', 'fox_w_out', 'ffn_w_up', 'ffn_conv_w', 'ffn_conv_b', 'ffn_w_down']
TWIN_WEIGHTS = ['mix_pre_g', 'mix_post_g', 'ffn_pre_g', 'ffn_post_g', 'hgrn_w_in', 'hgrn_w_out', 'hgrn_norm_g', 'hgrn_lb_param', 'swa_w_in', 'swa_w_out', 'swa_sinks', 'sc_w_in', 'sc_conv_w', 'sc_w_out', 'fox_w_in', 'fox_b_f', 'fox_w_out', 'ffn_w_up', 'ffn_conv_w', 'ffn_conv_b', 'ffn_w_down']
TWIN_DIFF_INPUT = 'x'
TWIN_INPUTS = ['x', 'positions', 'mix_pre_g', 'mix_post_g', 'ffn_pre_g', 'ffn_post_g', 'hgrn_w_in', 'hgrn_w_out', 'hgrn_norm_g', 'hgrn_lb_param', 'swa_w_in', 'swa_w_out', 'swa_sinks', 'sc_w_in', 'sc_conv_w', 'sc_w_out', 'fox_w_in', 'fox_b_f', 'fox_w_out', 'ffn_w_up', 'ffn_conv_w', 'ffn_conv_b', 'ffn_w_down', 'loss_target', 'm_mix_pre_g', 'm_mix_post_g', 'm_ffn_pre_g', 'm_ffn_post_g', 'm_hgrn_w_in', 'm_hgrn_w_out', 'm_hgrn_norm_g', 'm_hgrn_lb_param', 'm_swa_w_in', 'm_swa_w_out', 'm_swa_sinks', 'm_sc_w_in', 'm_sc_conv_w', 'm_sc_w_out', 'm_fox_w_in', 'm_fox_b_f', 'm_fox_w_out', 'm_ffn_w_up', 'm_ffn_conv_w', 'm_ffn_conv_b', 'm_ffn_w_down', 'v_mix_pre_g', 'v_mix_post_g', 'v_ffn_pre_g', 'v_ffn_post_g', 'v_hgrn_w_in', 'v_hgrn_w_out', 'v_hgrn_norm_g', 'v_hgrn_lb_param', 'v_swa_w_in', 'v_swa_w_out', 'v_swa_sinks', 'v_sc_w_in', 'v_sc_conv_w', 'v_sc_w_out', 'v_fox_w_in', 'v_fox_b_f', 'v_fox_w_out', 'v_ffn_w_up', 'v_ffn_conv_w', 'v_ffn_conv_b', 'v_ffn_w_down']
TWIN_OUTPUTS = ['loss', 'grad_x', 'grad_mix_pre_g', 'grad_mix_post_g', 'grad_ffn_pre_g', 'grad_ffn_post_g', 'grad_hgrn_w_in', 'grad_hgrn_w_out', 'grad_hgrn_norm_g', 'grad_hgrn_lb_param', 'grad_swa_w_in', 'grad_swa_w_out', 'grad_swa_sinks', 'grad_sc_w_in', 'grad_sc_conv_w', 'grad_sc_w_out', 'grad_fox_w_in', 'grad_fox_b_f', 'grad_fox_w_out', 'grad_ffn_w_up', 'grad_ffn_conv_w', 'grad_ffn_conv_b', 'grad_ffn_w_down', 'delta_mix_pre_g', 'delta_mix_post_g', 'delta_ffn_pre_g', 'delta_ffn_post_g', 'delta_hgrn_w_in', 'delta_hgrn_w_out', 'delta_hgrn_norm_g', 'delta_hgrn_lb_param', 'delta_swa_w_in', 'delta_swa_w_out', 'delta_swa_sinks', 'delta_sc_w_in', 'delta_sc_conv_w', 'delta_sc_w_out', 'delta_fox_w_in', 'delta_fox_b_f', 'delta_fox_w_out', 'delta_ffn_w_up', 'delta_ffn_conv_w', 'delta_ffn_conv_b', 'delta_ffn_w_down', 'new_m_mix_pre_g', 'new_m_mix_post_g', 'new_m_ffn_pre_g', 'new_m_ffn_post_g', 'new_m_hgrn_w_in', 'new_m_hgrn_w_out', 'new_m_hgrn_norm_g', 'new_m_hgrn_lb_param', 'new_m_swa_w_in', 'new_m_swa_w_out', 'new_m_swa_sinks', 'new_m_sc_w_in', 'new_m_sc_conv_w', 'new_m_sc_w_out', 'new_m_fox_w_in', 'new_m_fox_b_f', 'new_m_fox_w_out', 'new_m_ffn_w_up', 'new_m_ffn_conv_w', 'new_m_ffn_conv_b', 'new_m_ffn_w_down', 'new_v_mix_pre_g', 'new_v_mix_post_g', 'new_v_ffn_pre_g', 'new_v_ffn_post_g', 'new_v_hgrn_w_in', 'new_v_hgrn_w_out', 'new_v_hgrn_norm_g', 'new_v_hgrn_lb_param', 'new_v_swa_w_in', 'new_v_swa_w_out', 'new_v_swa_sinks', 'new_v_sc_w_in', 'new_v_sc_conv_w', 'new_v_sc_w_out', 'new_v_fox_w_in', 'new_v_fox_b_f', 'new_v_fox_w_out', 'new_v_ffn_w_up', 'new_v_ffn_conv_w', 'new_v_ffn_conv_b', 'new_v_ffn_w_down']
TWIN_LEAF_KINDS = {'loss': 'loss', 'grad_x': 'grad_x', 'grad_mix_pre_g': 'grad_w', 'grad_mix_post_g': 'grad_w', 'grad_ffn_pre_g': 'grad_w', 'grad_ffn_post_g': 'grad_w', 'grad_hgrn_w_in': 'grad_w', 'grad_hgrn_w_out': 'grad_w', 'grad_hgrn_norm_g': 'grad_w', 'grad_hgrn_lb_param': 'grad_w', 'grad_swa_w_in': 'grad_w', 'grad_swa_w_out': 'grad_w', 'grad_swa_sinks': 'grad_w', 'grad_sc_w_in': 'grad_w', 'grad_sc_conv_w': 'grad_w', 'grad_sc_w_out': 'grad_w', 'grad_fox_w_in': 'grad_w', 'grad_fox_b_f': 'grad_w', 'grad_fox_w_out': 'grad_w', 'grad_ffn_w_up': 'grad_w', 'grad_ffn_conv_w': 'grad_w', 'grad_ffn_conv_b': 'grad_w', 'grad_ffn_w_down': 'grad_w', 'delta_mix_pre_g': 'delta_w', 'delta_mix_post_g': 'delta_w', 'delta_ffn_pre_g': 'delta_w', 'delta_ffn_post_g': 'delta_w', 'delta_hgrn_w_in': 'delta_w', 'delta_hgrn_w_out': 'delta_w', 'delta_hgrn_norm_g': 'delta_w', 'delta_hgrn_lb_param': 'delta_w', 'delta_swa_w_in': 'delta_w', 'delta_swa_w_out': 'delta_w', 'delta_swa_sinks': 'delta_w', 'delta_sc_w_in': 'delta_w', 'delta_sc_conv_w': 'delta_w', 'delta_sc_w_out': 'delta_w', 'delta_fox_w_in': 'delta_w', 'delta_fox_b_f': 'delta_w', 'delta_fox_w_out': 'delta_w', 'delta_ffn_w_up': 'delta_w', 'delta_ffn_conv_w': 'delta_w', 'delta_ffn_conv_b': 'delta_w', 'delta_ffn_w_down': 'delta_w', 'new_m_mix_pre_g': 'new_m', 'new_m_mix_post_g': 'new_m', 'new_m_ffn_pre_g': 'new_m', 'new_m_ffn_post_g': 'new_m', 'new_m_hgrn_w_in': 'new_m', 'new_m_hgrn_w_out': 'new_m', 'new_m_hgrn_norm_g': 'new_m', 'new_m_hgrn_lb_param': 'new_m', 'new_m_swa_w_in': 'new_m', 'new_m_swa_w_out': 'new_m', 'new_m_swa_sinks': 'new_m', 'new_m_sc_w_in': 'new_m', 'new_m_sc_conv_w': 'new_m', 'new_m_sc_w_out': 'new_m', 'new_m_fox_w_in': 'new_m', 'new_m_fox_b_f': 'new_m', 'new_m_fox_w_out': 'new_m', 'new_m_ffn_w_up': 'new_m', 'new_m_ffn_conv_w': 'new_m', 'new_m_ffn_conv_b': 'new_m', 'new_m_ffn_w_down': 'new_m', 'new_v_mix_pre_g': 'new_v', 'new_v_mix_post_g': 'new_v', 'new_v_ffn_pre_g': 'new_v', 'new_v_ffn_post_g': 'new_v', 'new_v_hgrn_w_in': 'new_v', 'new_v_hgrn_w_out': 'new_v', 'new_v_hgrn_norm_g': 'new_v', 'new_v_hgrn_lb_param': 'new_v', 'new_v_swa_w_in': 'new_v', 'new_v_swa_w_out': 'new_v', 'new_v_swa_sinks': 'new_v', 'new_v_sc_w_in': 'new_v', 'new_v_sc_conv_w': 'new_v', 'new_v_sc_w_out': 'new_v', 'new_v_fox_w_in': 'new_v', 'new_v_fox_b_f': 'new_v', 'new_v_fox_w_out': 'new_v', 'new_v_ffn_w_up': 'new_v', 'new_v_ffn_conv_w': 'new_v', 'new_v_ffn_conv_b': 'new_v', 'new_v_ffn_w_down': 'new_v'}


def _forward(args):
    return _fwd_reference(*[args[k] for k in FWD_PARAMS])


def _output_shape():
    out = _jax.eval_shape(lambda: _forward(_fwd_setup_inputs(0)))
    return out.shape, out.dtype

N_MICROBATCH = 1
ADAM_LR = 0.001
ADAM_B1 = 0.9
ADAM_B2 = 0.999
ADAM_EPS = 1e-08
ADAM_WD = 0.01
ADAM_STEP = 10
PER_EXAMPLE_BATCH_AXIS = {'x': 0, 'loss_target': 0}
SHARED_INPUTS = ['positions']
_WEIGHT_DTYPES = {'mix_pre_g': _jnp.float32, 'mix_post_g': _jnp.float32, 'ffn_pre_g': _jnp.float32, 'ffn_post_g': _jnp.float32, 'hgrn_w_in': _jnp.float32, 'hgrn_w_out': _jnp.float32, 'hgrn_norm_g': _jnp.float32, 'hgrn_lb_param': _jnp.float32, 'swa_w_in': _jnp.float32, 'swa_w_out': _jnp.float32, 'swa_sinks': _jnp.float32, 'sc_w_in': _jnp.float32, 'sc_conv_w': _jnp.float32, 'sc_w_out': _jnp.float32, 'fox_w_in': _jnp.float32, 'fox_b_f': _jnp.float32, 'fox_w_out': _jnp.float32, 'ffn_w_up': _jnp.float32, 'ffn_conv_w': _jnp.float32, 'ffn_conv_b': _jnp.float32, 'ffn_w_down': _jnp.float32}
MOMENT_SCALE = {'mix_pre_g': 1.211237e+00, 'mix_post_g': 1.535289e+01, 'ffn_pre_g': 8.180239e-01, 'ffn_post_g': 1.578505e+01, 'hgrn_w_in': 9.573238e-01, 'hgrn_w_out': 1.335949e+00, 'hgrn_norm_g': 5.569883e+00, 'hgrn_lb_param': 2.795665e-02, 'swa_w_in': 1.092261e+00, 'swa_w_out': 9.117918e-01, 'swa_sinks': 2.634081e-01, 'sc_w_in': 3.914111e-01, 'sc_conv_w': 3.927315e-01, 'sc_w_out': 3.931336e-01, 'fox_w_in': 3.232883e-01, 'fox_b_f': 1.698268e+00, 'fox_w_out': 5.838278e-01, 'ffn_w_up': 3.548724e-01, 'ffn_conv_w': 3.568093e-01, 'ffn_conv_b': 8.364308e-01, 'ffn_w_down': 5.911217e-01}


def _to_microbatches(a, axis):
    t = _jnp.moveaxis(a, axis, 0)
    t = t.reshape((N_MICROBATCH, t.shape[0] // N_MICROBATCH) + t.shape[1:])
    return _jnp.moveaxis(t, 1, axis + 1)


def setup_inputs(seed: int = 0) -> dict:
    inp = _fwd_setup_inputs(seed)
    key = _jax.random.fold_in(_jax.random.key(seed), 7919)
    shape, _ = _output_shape()
    out = dict(inp)
    out["loss_target"] = _jax.random.normal(_jax.random.fold_in(key, 0), shape, _jnp.float32)
    for i, name in enumerate(TWIN_WEIGHTS):
        w = inp[name].astype(_jnp.float32)
        if MOMENT_SCALE is None:
            s = _jnp.sqrt(_jnp.mean(_jnp.square(w)) + 1e-30)
        else:
            s = MOMENT_SCALE[name]
        km, kv = _jax.random.split(_jax.random.fold_in(key, i + 1))
        out[name] = w
        out["m_" + name] = s * _jax.random.normal(km, w.shape, _jnp.float32)
        out["v_" + name] = (s * s) * _jax.random.uniform(kv, w.shape, _jnp.float32, 0.5, 1.5)
    if N_MICROBATCH > 1:
        for name, axis in PER_EXAMPLE_BATCH_AXIS.items():
            out[name] = _to_microbatches(out[name], axis)
    return {'x': out['x'], 'positions': out['positions'], 'mix_pre_g': out['mix_pre_g'], 'mix_post_g': out['mix_post_g'], 'ffn_pre_g': out['ffn_pre_g'], 'ffn_post_g': out['ffn_post_g'], 'hgrn_w_in': out['hgrn_w_in'], 'hgrn_w_out': out['hgrn_w_out'], 'hgrn_norm_g': out['hgrn_norm_g'], 'hgrn_lb_param': out['hgrn_lb_param'], 'swa_w_in': out['swa_w_in'], 'swa_w_out': out['swa_w_out'], 'swa_sinks': out['swa_sinks'], 'sc_w_in': out['sc_w_in'], 'sc_conv_w': out['sc_conv_w'], 'sc_w_out': out['sc_w_out'], 'fox_w_in': out['fox_w_in'], 'fox_b_f': out['fox_b_f'], 'fox_w_out': out['fox_w_out'], 'ffn_w_up': out['ffn_w_up'], 'ffn_conv_w': out['ffn_conv_w'], 'ffn_conv_b': out['ffn_conv_b'], 'ffn_w_down': out['ffn_w_down'], 'loss_target': out['loss_target'], 'm_mix_pre_g': out['m_mix_pre_g'], 'm_mix_post_g': out['m_mix_post_g'], 'm_ffn_pre_g': out['m_ffn_pre_g'], 'm_ffn_post_g': out['m_ffn_post_g'], 'm_hgrn_w_in': out['m_hgrn_w_in'], 'm_hgrn_w_out': out['m_hgrn_w_out'], 'm_hgrn_norm_g': out['m_hgrn_norm_g'], 'm_hgrn_lb_param': out['m_hgrn_lb_param'], 'm_swa_w_in': out['m_swa_w_in'], 'm_swa_w_out': out['m_swa_w_out'], 'm_swa_sinks': out['m_swa_sinks'], 'm_sc_w_in': out['m_sc_w_in'], 'm_sc_conv_w': out['m_sc_conv_w'], 'm_sc_w_out': out['m_sc_w_out'], 'm_fox_w_in': out['m_fox_w_in'], 'm_fox_b_f': out['m_fox_b_f'], 'm_fox_w_out': out['m_fox_w_out'], 'm_ffn_w_up': out['m_ffn_w_up'], 'm_ffn_conv_w': out['m_ffn_conv_w'], 'm_ffn_conv_b': out['m_ffn_conv_b'], 'm_ffn_w_down': out['m_ffn_w_down'], 'v_mix_pre_g': out['v_mix_pre_g'], 'v_mix_post_g': out['v_mix_post_g'], 'v_ffn_pre_g': out['v_ffn_pre_g'], 'v_ffn_post_g': out['v_ffn_post_g'], 'v_hgrn_w_in': out['v_hgrn_w_in'], 'v_hgrn_w_out': out['v_hgrn_w_out'], 'v_hgrn_norm_g': out['v_hgrn_norm_g'], 'v_hgrn_lb_param': out['v_hgrn_lb_param'], 'v_swa_w_in': out['v_swa_w_in'], 'v_swa_w_out': out['v_swa_w_out'], 'v_swa_sinks': out['v_swa_sinks'], 'v_sc_w_in': out['v_sc_w_in'], 'v_sc_conv_w': out['v_sc_conv_w'], 'v_sc_w_out': out['v_sc_w_out'], 'v_fox_w_in': out['v_fox_w_in'], 'v_fox_b_f': out['v_fox_b_f'], 'v_fox_w_out': out['v_fox_w_out'], 'v_ffn_w_up': out['v_ffn_w_up'], 'v_ffn_conv_w': out['v_ffn_conv_w'], 'v_ffn_conv_b': out['v_ffn_conv_b'], 'v_ffn_w_down': out['v_ffn_w_down']}


def _loss(weights, diff, rest, loss_target):
    with _jax.named_scope("forward"):
        args = {**rest, TWIN_DIFF_INPUT: diff, **{k: w.astype(_WEIGHT_DTYPES[k]) for k, w in weights.items()}}
        y = _forward(args)
    with _jax.named_scope("loss_head"):
        err = _jnp.square(y.astype(_jnp.float32) - loss_target)
        return 0.5 * _jnp.sum(_jnp.mean(err, axis=-1)) if err.ndim else 0.5 * err


def _adamw(w, g, m, v):
    m = ADAM_B1 * m + (1.0 - ADAM_B1) * g
    v = ADAM_B2 * v + (1.0 - ADAM_B2) * _jnp.square(g)
    m_hat = m / (1.0 - ADAM_B1 ** ADAM_STEP)
    v_hat = v / (1.0 - ADAM_B2 ** ADAM_STEP)
    delta = -ADAM_LR * (m_hat / (_jnp.sqrt(v_hat) + ADAM_EPS) + ADAM_WD * w)
    return delta, m, v


def reference(x, positions, mix_pre_g, mix_post_g, ffn_pre_g, ffn_post_g, hgrn_w_in, hgrn_w_out, hgrn_norm_g, hgrn_lb_param, swa_w_in, swa_w_out, swa_sinks, sc_w_in, sc_conv_w, sc_w_out, fox_w_in, fox_b_f, fox_w_out, ffn_w_up, ffn_conv_w, ffn_conv_b, ffn_w_down, loss_target, m_mix_pre_g, m_mix_post_g, m_ffn_pre_g, m_ffn_post_g, m_hgrn_w_in, m_hgrn_w_out, m_hgrn_norm_g, m_hgrn_lb_param, m_swa_w_in, m_swa_w_out, m_swa_sinks, m_sc_w_in, m_sc_conv_w, m_sc_w_out, m_fox_w_in, m_fox_b_f, m_fox_w_out, m_ffn_w_up, m_ffn_conv_w, m_ffn_conv_b, m_ffn_w_down, v_mix_pre_g, v_mix_post_g, v_ffn_pre_g, v_ffn_post_g, v_hgrn_w_in, v_hgrn_w_out, v_hgrn_norm_g, v_hgrn_lb_param, v_swa_w_in, v_swa_w_out, v_swa_sinks, v_sc_w_in, v_sc_conv_w, v_sc_w_out, v_fox_w_in, v_fox_b_f, v_fox_w_out, v_ffn_w_up, v_ffn_conv_w, v_ffn_conv_b, v_ffn_w_down):
    given = dict(x=x, positions=positions, mix_pre_g=mix_pre_g, mix_post_g=mix_post_g, ffn_pre_g=ffn_pre_g, ffn_post_g=ffn_post_g, hgrn_w_in=hgrn_w_in, hgrn_w_out=hgrn_w_out, hgrn_norm_g=hgrn_norm_g, hgrn_lb_param=hgrn_lb_param, swa_w_in=swa_w_in, swa_w_out=swa_w_out, swa_sinks=swa_sinks, sc_w_in=sc_w_in, sc_conv_w=sc_conv_w, sc_w_out=sc_w_out, fox_w_in=fox_w_in, fox_b_f=fox_b_f, fox_w_out=fox_w_out, ffn_w_up=ffn_w_up, ffn_conv_w=ffn_conv_w, ffn_conv_b=ffn_conv_b, ffn_w_down=ffn_w_down, loss_target=loss_target, m_mix_pre_g=m_mix_pre_g, m_mix_post_g=m_mix_post_g, m_ffn_pre_g=m_ffn_pre_g, m_ffn_post_g=m_ffn_post_g, m_hgrn_w_in=m_hgrn_w_in, m_hgrn_w_out=m_hgrn_w_out, m_hgrn_norm_g=m_hgrn_norm_g, m_hgrn_lb_param=m_hgrn_lb_param, m_swa_w_in=m_swa_w_in, m_swa_w_out=m_swa_w_out, m_swa_sinks=m_swa_sinks, m_sc_w_in=m_sc_w_in, m_sc_conv_w=m_sc_conv_w, m_sc_w_out=m_sc_w_out, m_fox_w_in=m_fox_w_in, m_fox_b_f=m_fox_b_f, m_fox_w_out=m_fox_w_out, m_ffn_w_up=m_ffn_w_up, m_ffn_conv_w=m_ffn_conv_w, m_ffn_conv_b=m_ffn_conv_b, m_ffn_w_down=m_ffn_w_down, v_mix_pre_g=v_mix_pre_g, v_mix_post_g=v_mix_post_g, v_ffn_pre_g=v_ffn_pre_g, v_ffn_post_g=v_ffn_post_g, v_hgrn_w_in=v_hgrn_w_in, v_hgrn_w_out=v_hgrn_w_out, v_hgrn_norm_g=v_hgrn_norm_g, v_hgrn_lb_param=v_hgrn_lb_param, v_swa_w_in=v_swa_w_in, v_swa_w_out=v_swa_w_out, v_swa_sinks=v_swa_sinks, v_sc_w_in=v_sc_w_in, v_sc_conv_w=v_sc_conv_w, v_sc_w_out=v_sc_w_out, v_fox_w_in=v_fox_w_in, v_fox_b_f=v_fox_b_f, v_fox_w_out=v_fox_w_out, v_ffn_w_up=v_ffn_w_up, v_ffn_conv_w=v_ffn_conv_w, v_ffn_conv_b=v_ffn_conv_b, v_ffn_w_down=v_ffn_w_down)
    weights = {n: given[n] for n in TWIN_WEIGHTS}
    shared = {n: given[n] for n in SHARED_INPUTS}
    per_example = {n: given[n] for n in ['x']}
    grad_fn = _jax.value_and_grad(_loss, argnums=(0, 1))

    def one_microbatch(ex, loss_target):
        ex = dict(ex)
        diff = ex.pop(TWIN_DIFF_INPUT)
        return grad_fn(weights, diff, {**shared, **ex}, loss_target)

    if N_MICROBATCH == 1:
        loss, (grad_w, grad_x) = one_microbatch(per_example, given["loss_target"])
    else:
        def body(carry, xs):
            loss_sum, grad_sum = carry
            l_k, (gw_k, gx_k) = one_microbatch(xs[0], xs[1])
            with _jax.named_scope("update"):
                return (loss_sum + l_k, _jax.tree.map(_jnp.add, grad_sum, gw_k)), gx_k

        init = (_jnp.zeros((), _jnp.float32), _jax.tree.map(_jnp.zeros_like, weights))
        (loss, grad_w), grad_x = _jax.lax.scan(body, init, (per_example, given["loss_target"]))
    with _jax.named_scope("update"):
        delta_w, new_m, new_v = {}, {}, {}
        for n in TWIN_WEIGHTS:
            delta_w[n], new_m[n], new_v[n] = _adamw(weights[n], grad_w[n], given["m_" + n], given["v_" + n])
    return (loss, grad_x, *[grad_w[n] for n in TWIN_WEIGHTS], *[delta_w[n] for n in TWIN_WEIGHTS],
            *[new_m[n] for n in TWIN_WEIGHTS], *[new_v[n] for n in TWIN_WEIGHTS])
```

```python
import functools
import math

import jax
import jax.numpy as jnp
from jax import lax
from jax.experimental import pallas as pl
from jax.experimental.pallas import tpu as pltpu

N_DEV = 8
MESH = pl.DeviceIdType.MESH
HBM_SPEC = pl.BlockSpec(memory_space=pltpu.HBM)
PACK_COLS = 512
PACK_ROW_BLOCK = 512

MM_TILE = 1024
MM_K_SINGLE = 2048
MM_K_TILE = 1536
MM_VMEM_BYTES = 48 * 1024 * 1024
ROW_BLOCK = 256
COL_BLOCK = 512
HALO = 8
ATT_BLOCK = 256
HGRN_SUB = 16
HGRN_ROWS = 256
NEG = -1e30

RMS_EPS = 1e-6
HGRN_EXPAND = 128
SWA_HEAD_DIM = 64
SWA_GROUP = 8
SWA_WINDOW = 128
FOX_HEAD_DIM = 64
ROPE_THETA = 500000.0
ROT_DIM = SWA_HEAD_DIM // 4

ADAM_LR = 0.001
ADAM_B1 = 0.9
ADAM_B2 = 0.999
ADAM_EPS = 1e-08
ADAM_WD = 0.01
ADAM_STEP = 10

_NT = (((1,), (1,)), ((), ()))
_NN = (((1,), (0,)), ((), ()))
_TN = (((0,), (0,)), ((), ()))


def _tile(dim, pref):
    if dim <= pref:
        return dim
    t = pref - pref % 128
    while t >= 128:
        if dim % t == 0:
            return t
        t -= 128
    return dim


def _dot(a, b, dims):
    return lax.dot_general(a.astype(jnp.bfloat16), b.astype(jnp.bfloat16), dims, preferred_element_type=jnp.float32)


def _sigmoid(x):
    return 1.0 / (1.0 + jnp.exp(-x))


def _col_to_row(c):
    n = c.shape[0]
    eye = lax.broadcasted_iota(jnp.int32, (n, n), 0) == lax.broadcasted_iota(jnp.int32, (n, n), 1)
    return jnp.sum(jnp.where(eye, c, 0.0), axis=0, keepdims=True)


def _row_to_col(r):
    n = r.shape[1]
    eye = lax.broadcasted_iota(jnp.int32, (n, n), 0) == lax.broadcasted_iota(jnp.int32, (n, n), 1)
    return jnp.sum(jnp.where(eye, r, 0.0), axis=1, keepdims=True)


def _all_gather(x, name):
    r, c = x.shape

    def body(x_ref, out_ref, send_sems, recv_sems, local_sem):
        mx, my, mc = lax.axis_index("x"), lax.axis_index("y"), lax.axis_index("c")
        me, sibling = (mx, my, mc), (mx, my, 1 - mc)
        chips = [(1 - mx, my), (mx, 1 - my), (1 - mx, 1 - my)]

        def rows(px, py, pc):
            return out_ref.at[4 * px + 2 * py + pc]

        def copy(k, block, to, src=None):
            return pltpu.make_async_remote_copy(
                src_ref=rows(*block) if src is None else src, dst_ref=rows(*block),
                send_sem=send_sems.at[k], recv_sem=recv_sems.at[k], device_id=to, device_id_type=MESH)

        mine = pltpu.make_async_copy(x_ref, rows(*me), local_sem)
        mine.start()
        first = [copy(0, me, sibling, src=x_ref)]
        first += [copy(1 + j, me, (*chip, mc), src=x_ref) for j, chip in enumerate(chips)]
        for cp in first:
            cp.start()
        passed = [copy(4 + j, (*chip, mc), sibling) for j, chip in enumerate(chips)]
        for j, chip in enumerate(chips):
            copy(1 + j, (*chip, mc), me).wait_recv()
            passed[j].start()
        copy(0, sibling, me).wait_recv()
        for j, chip in enumerate(chips):
            copy(4 + j, (*chip, 1 - mc), me).wait_recv()
        for cp in first + passed:
            cp.wait_send()
        mine.wait()

    return pl.pallas_call(
        body, name=name, out_shape=jax.ShapeDtypeStruct((N_DEV, r, c), x.dtype),
        in_specs=[HBM_SPEC], out_specs=HBM_SPEC,
        scratch_shapes=[pltpu.SemaphoreType.DMA((7,)), pltpu.SemaphoreType.DMA((7,)), pltpu.SemaphoreType.DMA],
    )(x)


def _exchange_blocks(x, name):
    _, r, c = x.shape
    flips = [(dx, dy, dc) for dx in (0, 1) for dy in (0, 1) for dc in (0, 1) if (dx, dy, dc) != (0, 0, 0)]

    def body(x_ref, out_ref, send_sems, recv_sems, local_sem):
        mx, my, mc = lax.axis_index("x"), lax.axis_index("y"), lax.axis_index("c")
        me = 4 * mx + 2 * my + mc

        def peer(flip):
            dx, dy, dc = flip
            return ((1 - mx) if dx else mx, (1 - my) if dy else my, (1 - mc) if dc else mc)

        def copy(k, dst_block):
            p = peer(flips[k])
            return pltpu.make_async_remote_copy(
                src_ref=x_ref.at[4 * p[0] + 2 * p[1] + p[2]], dst_ref=out_ref.at[dst_block],
                send_sem=send_sems.at[k], recv_sem=recv_sems.at[k], device_id=p, device_id_type=MESH)

        mine = pltpu.make_async_copy(x_ref.at[me], out_ref.at[me], local_sem)
        mine.start()
        sends = [copy(k, me) for k in range(len(flips))]
        for cp in sends:
            cp.start()
        for k in range(len(flips)):
            p = peer(flips[k])
            copy(k, 4 * p[0] + 2 * p[1] + p[2]).wait_recv()
        for cp in sends:
            cp.wait_send()
        mine.wait()

    return pl.pallas_call(
        body, name=name, out_shape=jax.ShapeDtypeStruct(x.shape, x.dtype),
        in_specs=[HBM_SPEC], out_specs=HBM_SPEC,
        scratch_shapes=[pltpu.SemaphoreType.DMA((7,)), pltpu.SemaphoreType.DMA((7,)), pltpu.SemaphoreType.DMA],
    )(x)


def _sum_blocks(x, name):
    _, r, c = x.shape
    tr = PACK_ROW_BLOCK if r % PACK_ROW_BLOCK == 0 else r

    def body(x_ref, o_ref):
        acc = x_ref[0].astype(jnp.float32)
        for j in range(1, N_DEV):
            acc = acc + x_ref[j].astype(jnp.float32)
        o_ref[...] = acc

    return pl.pallas_call(
        body, name=name, grid=(r // tr,), out_shape=jax.ShapeDtypeStruct((r, c), jnp.float32),
        in_specs=[pl.BlockSpec((N_DEV, tr, c), lambda i: (0, i, 0))],
        out_specs=pl.BlockSpec((tr, c), lambda i: (i, 0)),
    )(x)


def _matmul(a, b, mode, name, out_dtype=jnp.float32):
    if mode == "nn":
        (m, k), (_, n) = a.shape, b.shape
    elif mode == "nt":
        (m, k), (n, _) = a.shape, b.shape
    else:
        (k, m), (_, n) = a.shape, b.shape
    tm, tn = _tile(m, MM_TILE), _tile(n, MM_TILE)
    tk = k if k <= MM_K_SINGLE else _tile(k, MM_K_TILE)
    nk = k // tk
    dims = {"nn": _NN, "nt": _NT, "tn": _TN}[mode]
    rows_inner = nk == 1 and (n // tn) * a.size < (m // tm) * b.size

    def body(a_ref, b_ref, o_ref, *scratch):
        part = lax.dot_general(a_ref[...], b_ref[...], dims, preferred_element_type=jnp.float32)
        if nk == 1:
            o_ref[...] = part.astype(o_ref.dtype)
            return
        acc_ref, = scratch
        kk = pl.program_id(2)

        @pl.when(kk == 0)
        def _():
            acc_ref[...] = part

        @pl.when(kk > 0)
        def _():
            acc_ref[...] += part

        @pl.when(kk == nk - 1)
        def _():
            o_ref[...] = acc_ref[...].astype(o_ref.dtype)

    def ij(g0, g1):
        return (g1, g0) if rows_inner else (g0, g1)

    def a_map(g0, g1, kk):
        i, _ = ij(g0, g1)
        return (kk, i) if mode == "tn" else (i, kk)

    def b_map(g0, g1, kk):
        _, j = ij(g0, g1)
        return (j, kk) if mode == "nt" else (kk, j)

    def o_map(g0, g1, kk):
        return ij(g0, g1)

    grid = (n // tn, m // tm, nk) if rows_inner else (m // tm, n // tn, nk)
    return pl.pallas_call(
        body, name=name, grid=grid, out_shape=jax.ShapeDtypeStruct((m, n), out_dtype),
        in_specs=[pl.BlockSpec((tk, tm) if mode == "tn" else (tm, tk), a_map),
                  pl.BlockSpec((tn, tk) if mode == "nt" else (tk, tn), b_map)],
        out_specs=pl.BlockSpec((tm, tn), o_map),
        scratch_shapes=[pltpu.VMEM((tm, tn), jnp.float32)] if nk > 1 else [],
        compiler_params=pltpu.CompilerParams(dimension_semantics=("parallel", "parallel", "arbitrary"),
                                             vmem_limit_bytes=MM_VMEM_BYTES),
    )(a, b)


def _pad_rows(flat, axis):
    n = flat.shape[axis]
    unit = PACK_COLS * PACK_ROW_BLOCK
    pad = (-n) % unit
    if pad:
        widths = [(0, 0)] * flat.ndim
        widths[axis] = (0, pad)
        flat = jnp.pad(flat, widths)
    return flat


def _gather_pack(shards, kinds):
    parts = []
    for s, kind in zip(shards, kinds):
        if kind == "exact":
            parts.append(lax.bitcast_convert_type(s, jnp.bfloat16).reshape(-1))
        else:
            parts.append(s.astype(jnp.bfloat16).reshape(-1))
    return _pad_rows(jnp.concatenate(parts), 0).reshape(-1, PACK_COLS)


def _gather_unpack(gathered, shapes, kinds):
    flat = gathered.reshape(N_DEV, -1)
    out, off = [], 0
    for shape, kind in zip(shapes, kinds):
        size = math.prod(shape)
        if kind == "exact":
            piece = flat[:, off:off + 2 * size].reshape((N_DEV,) + tuple(shape) + (2,))
            piece = lax.bitcast_convert_type(piece, jnp.float32)
            off += 2 * size
            kind = "col"
        else:
            piece = flat[:, off:off + size].reshape((N_DEV,) + tuple(shape))
            off += size
        if kind == "col":
            piece = jnp.moveaxis(piece, 0, -2)
            piece = piece.reshape(piece.shape[:-2] + (N_DEV * shape[-1],))
        else:
            piece = piece.reshape((N_DEV * shape[0],) + tuple(shape[1:]))
        out.append(piece)
    return tuple(out)


def _scatter_pack(cts, shapes, kinds):
    parts = []
    for ct, shape, kind in zip(cts, shapes, kinds):
        ct = ct.astype(jnp.bfloat16)
        if kind == "row":
            piece = ct.reshape((N_DEV,) + tuple(shape))
        else:
            piece = ct.reshape(tuple(shape[:-1]) + (N_DEV, shape[-1]))
            piece = jnp.moveaxis(piece, -2, 0)
        parts.append(piece.reshape(N_DEV, -1))
    return _pad_rows(jnp.concatenate(parts, axis=1), 1).reshape(N_DEV, -1, PACK_COLS)


def _scatter_unpack(summed, shapes):
    flat = summed.reshape(-1)
    out, off = [], 0
    for shape in shapes:
        size = math.prod(shape)
        out.append(flat[off:off + size].reshape(shape))
        off += size
    return tuple(out)


def _make_layer_gather(shapes, kinds, tag):
    shapes = tuple(tuple(s) for s in shapes)

    def impl(shards):
        return _gather_unpack(_all_gather(_gather_pack(shards, kinds), tag + "_gather"), shapes, kinds)

    gather = jax.custom_vjp(impl)

    def fwd(shards):
        return impl(shards), None

    def bwd(_, cts):
        packed = _scatter_pack(cts, shapes, kinds)
        summed = _sum_blocks(_exchange_blocks(packed, tag + "_scatter"), tag + "_scatter_sum")
        return (_scatter_unpack(summed, shapes),)

    gather.defvjp(fwd, bwd)
    return gather


def _adamw(w, g, m, v, name):
    shape = w.shape
    cols = shape[-1]
    rows = math.prod(shape[:-1])
    tr = rows
    if rows > 512:
        for cand in (256, 128, 64, 32, 16, 8):
            if rows % cand == 0:
                tr = cand
                break
    c1 = 1.0 - ADAM_B1 ** ADAM_STEP
    c2 = 1.0 - ADAM_B2 ** ADAM_STEP

    def body(w_ref, g_ref, m_ref, v_ref, d_ref, nm_ref, nv_ref):
        gg = g_ref[...]
        nm = ADAM_B1 * m_ref[...] + (1.0 - ADAM_B1) * gg
        nv = ADAM_B2 * v_ref[...] + (1.0 - ADAM_B2) * (gg * gg)
        m_hat = nm / c1
        v_hat = nv / c2
        d_ref[...] = -ADAM_LR * (m_hat / (jnp.sqrt(v_hat) + ADAM_EPS) + ADAM_WD * w_ref[...])
        nm_ref[...] = nm
        nv_ref[...] = nv

    spec = pl.BlockSpec((tr, cols), lambda i: (i, 0))
    outs = pl.pallas_call(
        body, name=name, grid=(rows // tr,), out_shape=[jax.ShapeDtypeStruct((rows, cols), jnp.float32)] * 3,
        in_specs=[spec] * 4, out_specs=[spec] * 3,
    )(*[t.reshape(rows, cols) for t in (w, g, m, v)])
    return tuple(o.reshape(shape) for o in outs)


def _rms_fwd_call(x, g, res, name):
    t, d = x.shape
    tr = _tile(t, ROW_BLOCK)
    with_res = res is not None

    def body(*refs):
        if with_res:
            x_ref, g_ref, r_ref, y_ref = refs
        else:
            x_ref, g_ref, y_ref = refs
        xv = x_ref[...]
        rstd = lax.rsqrt(jnp.mean(xv * xv, axis=-1, keepdims=True) + RMS_EPS)
        y = xv * rstd * g_ref[...]
        if with_res:
            y = y + r_ref[...]
        y_ref[...] = y

    row = pl.BlockSpec((tr, d), lambda i: (i, 0))
    gspec = pl.BlockSpec((1, d), lambda i: (0, 0))
    ins = [x, g] + ([res] if with_res else [])
    return pl.pallas_call(
        body, name=name, grid=(t // tr,), out_shape=jax.ShapeDtypeStruct((t, d), jnp.float32),
        in_specs=[row, gspec] + ([row] if with_res else []), out_specs=row,
    )(*ins)


def _rms_bwd_call(x, g, dy, name):
    t, d = x.shape
    tr = _tile(t, ROW_BLOCK)

    def body(x_ref, g_ref, dy_ref, dx_ref, dg_ref):
        i = pl.program_id(0)
        xv = x_ref[...]
        rstd = lax.rsqrt(jnp.mean(xv * xv, axis=-1, keepdims=True) + RMS_EPS)
        xh = xv * rstd
        dyv = dy_ref[...]
        dyg = dyv * g_ref[...]
        dx_ref[...] = rstd * (dyg - xh * jnp.mean(dyg * xh, axis=-1, keepdims=True))

        @pl.when(i == 0)
        def _():
            dg_ref[...] = jnp.zeros_like(dg_ref)

        dg_ref[...] += jnp.sum(dyv * xh, axis=0, keepdims=True)

    row = pl.BlockSpec((tr, d), lambda i: (i, 0))
    gspec = pl.BlockSpec((1, d), lambda i: (0, 0))
    return pl.pallas_call(
        body, name=name, grid=(t // tr,),
        out_shape=[jax.ShapeDtypeStruct((t, d), jnp.float32), jax.ShapeDtypeStruct((1, d), jnp.float32)],
        in_specs=[row, gspec, row], out_specs=[row, gspec],
        compiler_params=pltpu.CompilerParams(dimension_semantics=("arbitrary",)),
    )(x, g, dy)


@functools.partial(jax.custom_vjp, nondiff_argnums=(3,))
def _norm_add(y, g, res, name):
    return _rms_fwd_call(y, g.reshape(1, -1), res, name + "_fwd")


def _norm_add_fwd(y, g, res, name):
    return _rms_fwd_call(y, g.reshape(1, -1), res, name + "_fwd"), (y, g)


def _norm_add_bwd(name, saved, dout):
    y, g = saved
    dy, dg = _rms_bwd_call(y, g.reshape(1, -1), dout, name + "_bwd")
    return dy, dg.reshape(g.shape), dout


_norm_add.defvjp(_norm_add_fwd, _norm_add_bwd)


@functools.partial(jax.custom_vjp, nondiff_argnums=(2,))
def _norm(x, g, name):
    return _rms_fwd_call(x, g.reshape(1, -1), None, name + "_fwd")


def _norm_fwd(x, g, name):
    return _rms_fwd_call(x, g.reshape(1, -1), None, name + "_fwd"), (x, g)


def _norm_bwd(name, saved, dout):
    x, g = saved
    dx, dg = _rms_bwd_call(x, g.reshape(1, -1), dout, name + "_bwd")
    return dx, dg.reshape(g.shape)


_norm.defvjp(_norm_fwd, _norm_bwd)


def _shift_down(x, before1, before2, row):
    xm1 = jnp.where(row == 0, before1, pltpu.roll(x, 1, 0))
    xm2 = jnp.where(row == 0, before2, jnp.where(row == 1, before1, pltpu.roll(x, 2, 0)))
    return xm1, xm2


def _shift_up(x, after1, after2, row):
    n = x.shape[0]
    xp1 = jnp.where(row == n - 1, after1, pltpu.roll(x, n - 1, 0))
    xp2 = jnp.where(row == n - 1, after2, jnp.where(row == n - 2, after1, pltpu.roll(x, n - 2, 0)))
    return xp1, xp2


def _ffn_act_fwd_call(u, conv_w, conv_b, name):
    t, f2 = u.shape
    f = f2 // 2
    tr, tc = _tile(t, ROW_BLOCK), _tile(f, COL_BLOCK)
    nj, hb = f // tc, tr // HALO

    def body(ug_ref, uu_ref, hg_ref, hu_ref, wg_ref, wu_ref, bg_ref, bu_ref, cg_ref, cu_ref, a_ref):
        i = pl.program_id(0)
        row = lax.broadcasted_iota(jnp.int32, (tr, tc), 0)

        def conv(x_ref, h_ref, w_ref, b_ref):
            xv = x_ref[...]
            h1 = jnp.where(i > 0, h_ref[7:8, :], 0.0)
            h2 = jnp.where(i > 0, h_ref[6:7, :], 0.0)
            xm1, xm2 = _shift_down(xv, h1, h2, row)
            return w_ref[0:1, :] * xm2 + w_ref[1:2, :] * xm1 + w_ref[2:3, :] * xv + b_ref[...]

        cg = conv(ug_ref, hg_ref, wg_ref, bg_ref)
        cu = conv(uu_ref, hu_ref, wu_ref, bu_ref)
        cg_ref[...] = cg
        cu_ref[...] = cu
        a_ref[...] = (cg * _sigmoid(cg) * cu).astype(jnp.bfloat16)

    def blk(off):
        return pl.BlockSpec((tr, tc), lambda i, j: (i, j + off))

    def halo(off):
        return pl.BlockSpec((HALO, tc), lambda i, j: (jnp.maximum(i * hb - 1, 0), j + off))

    def wspec(rows, off):
        return pl.BlockSpec((rows, tc), lambda i, j: (0, j + off))

    out_blk = pl.BlockSpec((tr, tc), lambda i, j: (i, j))
    return pl.pallas_call(
        body, name=name, grid=(t // tr, nj),
        out_shape=[jax.ShapeDtypeStruct((t, f), jnp.float32), jax.ShapeDtypeStruct((t, f), jnp.float32),
                   jax.ShapeDtypeStruct((t, f), jnp.bfloat16)],
        in_specs=[blk(0), blk(nj), halo(0), halo(nj), wspec(3, 0), wspec(3, nj), wspec(1, 0), wspec(1, nj)],
        out_specs=[out_blk, out_blk, out_blk],
    )(u, u, u, u, conv_w, conv_w, conv_b, conv_b)


def _ffn_act_bwd_call(da, cg, cu, u, conv_w, name):
    t, f = da.shape
    tr, tc = _tile(t, ROW_BLOCK), _tile(f, COL_BLOCK)
    nj, hb, ni = f // tc, tr // HALO, t // tr

    def body(da_ref, cg_ref, cu_ref, nda_ref, ncg_ref, ncu_ref, u_ref, hu_ref, w_ref, du_ref, dw_ref, db_ref):
        j, i = pl.program_id(0), pl.program_id(1)
        is_gate = j < nj
        row = lax.broadcasted_iota(jnp.int32, (tr, tc), 0)

        def duc_of(dav, cgv, cuv):
            sg = _sigmoid(cgv)
            d_gate = dav * cuv * (sg * (1.0 + cgv * (1.0 - sg)))
            d_up = dav * (cgv * sg)
            return jnp.where(is_gate, d_gate, d_up)

        duc = duc_of(da_ref[...], cg_ref[...], cu_ref[...])
        n1 = jnp.where(i < ni - 1, duc_of(nda_ref[0:1, :], ncg_ref[0:1, :], ncu_ref[0:1, :]), 0.0)
        n2 = jnp.where(i < ni - 1, duc_of(nda_ref[1:2, :], ncg_ref[1:2, :], ncu_ref[1:2, :]), 0.0)
        dp1, dp2 = _shift_up(duc, n1, n2, row)
        du_ref[...] = (w_ref[2:3, :] * duc + w_ref[1:2, :] * dp1 + w_ref[0:1, :] * dp2).astype(jnp.bfloat16)

        uv = u_ref[...]
        h1 = jnp.where(i > 0, hu_ref[7:8, :], 0.0)
        h2 = jnp.where(i > 0, hu_ref[6:7, :], 0.0)
        um1, um2 = _shift_down(uv, h1, h2, row)

        @pl.when(i == 0)
        def _():
            dw_ref[...] = jnp.zeros_like(dw_ref)
            db_ref[...] = jnp.zeros_like(db_ref)

        rows8 = lax.broadcasted_iota(jnp.int32, (HALO, tc), 0)
        s0 = jnp.sum(duc * um2, axis=0, keepdims=True)
        s1 = jnp.sum(duc * um1, axis=0, keepdims=True)
        s2 = jnp.sum(duc * uv, axis=0, keepdims=True)
        dw_ref[...] += jnp.where(rows8 == 0, s0, jnp.where(rows8 == 1, s1, jnp.where(rows8 == 2, s2, 0.0)))
        db_ref[...] += jnp.sum(duc, axis=0, keepdims=True)

    half = pl.BlockSpec((tr, tc), lambda j, i: (i, j % nj))
    nhalf = pl.BlockSpec((HALO, tc), lambda j, i: (jnp.minimum((i + 1) * hb, t // HALO - 1), j % nj))
    full = pl.BlockSpec((tr, tc), lambda j, i: (i, j))
    hfull = pl.BlockSpec((HALO, tc), lambda j, i: (jnp.maximum(i * hb - 1, 0), j))
    return pl.pallas_call(
        body, name=name, grid=(2 * nj, ni),
        out_shape=[jax.ShapeDtypeStruct((t, 2 * f), jnp.bfloat16), jax.ShapeDtypeStruct((HALO, 2 * f), jnp.float32),
                   jax.ShapeDtypeStruct((1, 2 * f), jnp.float32)],
        in_specs=[half, half, half, nhalf, nhalf, nhalf, full, hfull, pl.BlockSpec((3, tc), lambda j, i: (0, j))],
        out_specs=[full, pl.BlockSpec((HALO, tc), lambda j, i: (0, j)), pl.BlockSpec((1, tc), lambda j, i: (0, j))],
        compiler_params=pltpu.CompilerParams(dimension_semantics=("parallel", "arbitrary")),
    )(da, cg, cu, da, cg, cu, u, u, conv_w)


@functools.partial(jax.custom_vjp, nondiff_argnums=(5,))
def _ffn_block(h, w_up, conv_w, conv_b, w_down, name):
    return _ffn_block_fwd(h, w_up, conv_w, conv_b, w_down, name)[0]


def _ffn_block_fwd(h, w_up, conv_w, conv_b, w_down, name):
    hb = h.astype(jnp.bfloat16)
    u = _matmul(hb, w_up, "nn", name + "_up")
    cg, cu, a = _ffn_act_fwd_call(u, conv_w, conv_b.reshape(1, -1), name + "_act")
    return _matmul(a, w_down, "nn", name + "_down"), (hb, u, cg, cu, a, w_up, conv_w, conv_b, w_down)


def _ffn_block_bwd(name, saved, dy):
    hb, u, cg, cu, a, w_up, conv_w, conv_b, w_down = saved
    dyb = dy.astype(jnp.bfloat16)
    da = _matmul(dyb, w_down, "nt", name + "_down_dx")
    dw_down = _matmul(a, dyb, "tn", name + "_down_dw", jnp.bfloat16)
    du, dcw, dcb = _ffn_act_bwd_call(da, cg, cu, u, conv_w, name + "_act_bwd")
    dh = _matmul(du, w_up, "nt", name + "_up_dx")
    dw_up = _matmul(hb, du, "tn", name + "_up_dw", jnp.bfloat16)
    return dh, dw_up, dcw[:3], dcb.reshape(conv_b.shape), dw_down


_ffn_block.defvjp(_ffn_block_fwd, _ffn_block_bwd)


def _sconv_fwd_call(z, conv_w, name):
    t, d3 = z.shape
    d = d3 // 3
    tr, tc = _tile(t, ROW_BLOCK), _tile(d, COL_BLOCK)
    nj, hb = d // tc, tr // HALO

    def body(b_ref, c_ref, x_ref, hc_ref, hx_ref, w_ref, a_ref):
        i = pl.program_id(0)
        row = lax.broadcasted_iota(jnp.int32, (tr, tc), 0)
        p = c_ref[...] * x_ref[...]
        h1 = jnp.where(i > 0, hc_ref[7:8, :] * hx_ref[7:8, :], 0.0)
        h2 = jnp.where(i > 0, hc_ref[6:7, :] * hx_ref[6:7, :], 0.0)
        pm1, pm2 = _shift_down(p, h1, h2, row)
        pc = w_ref[0:1, :] * pm2 + w_ref[1:2, :] * pm1 + w_ref[2:3, :] * p
        a_ref[...] = (b_ref[...] * pc).astype(jnp.bfloat16)

    def blk(off):
        return pl.BlockSpec((tr, tc), lambda i, j: (i, j + off))

    def halo(off):
        return pl.BlockSpec((HALO, tc), lambda i, j: (jnp.maximum(i * hb - 1, 0), j + off))

    return pl.pallas_call(
        body, name=name, grid=(t // tr, nj), out_shape=jax.ShapeDtypeStruct((t, d), jnp.bfloat16),
        in_specs=[blk(0), blk(nj), blk(2 * nj), halo(nj), halo(2 * nj), pl.BlockSpec((3, tc), lambda i, j: (0, j))],
        out_specs=pl.BlockSpec((tr, tc), lambda i, j: (i, j)),
    )(z, z, z, z, z, conv_w)


def _sconv_bwd_call(da, z, conv_w, name):
    t, d = da.shape
    tr, tc = _tile(t, ROW_BLOCK), _tile(d, COL_BLOCK)
    nj, hb, ni = d // tc, tr // HALO, t // tr

    def body(da_ref, b_ref, c_ref, x_ref, hc_ref, hx_ref, nda_ref, nb_ref, w_ref, db_ref, dc_ref, dx_ref, dw_ref):
        i = pl.program_id(1)
        row = lax.broadcasted_iota(jnp.int32, (tr, tc), 0)
        cv, xv, dav = c_ref[...], x_ref[...], da_ref[...]
        p = cv * xv
        h1 = jnp.where(i > 0, hc_ref[7:8, :] * hx_ref[7:8, :], 0.0)
        h2 = jnp.where(i > 0, hc_ref[6:7, :] * hx_ref[6:7, :], 0.0)
        pm1, pm2 = _shift_down(p, h1, h2, row)
        pc = w_ref[0:1, :] * pm2 + w_ref[1:2, :] * pm1 + w_ref[2:3, :] * p
        db_ref[...] = (dav * pc).astype(jnp.bfloat16)
        dpc = dav * b_ref[...]
        n1 = jnp.where(i < ni - 1, nda_ref[0:1, :] * nb_ref[0:1, :], 0.0)
        n2 = jnp.where(i < ni - 1, nda_ref[1:2, :] * nb_ref[1:2, :], 0.0)
        dp1, dp2 = _shift_up(dpc, n1, n2, row)
        dp = w_ref[2:3, :] * dpc + w_ref[1:2, :] * dp1 + w_ref[0:1, :] * dp2
        dc_ref[...] = (dp * xv).astype(jnp.bfloat16)
        dx_ref[...] = (dp * cv).astype(jnp.bfloat16)

        @pl.when(i == 0)
        def _():
            dw_ref[...] = jnp.zeros_like(dw_ref)

        rows8 = lax.broadcasted_iota(jnp.int32, (HALO, tc), 0)
        s0 = jnp.sum(dpc * pm2, axis=0, keepdims=True)
        s1 = jnp.sum(dpc * pm1, axis=0, keepdims=True)
        s2 = jnp.sum(dpc * p, axis=0, keepdims=True)
        dw_ref[...] += jnp.where(rows8 == 0, s0, jnp.where(rows8 == 1, s1, jnp.where(rows8 == 2, s2, 0.0)))

    def blk(off):
        return pl.BlockSpec((tr, tc), lambda j, i: (i, j + off))

    def halo(off):
        return pl.BlockSpec((HALO, tc), lambda j, i: (jnp.maximum(i * hb - 1, 0), j + off))

    def nhalo(off):
        return pl.BlockSpec((HALO, tc), lambda j, i: (jnp.minimum((i + 1) * hb, t // HALO - 1), j + off))

    out = pl.BlockSpec((tr, tc), lambda j, i: (i, j))
    return pl.pallas_call(
        body, name=name, grid=(nj, ni),
        out_shape=[jax.ShapeDtypeStruct((t, d), jnp.bfloat16)] * 3 + [jax.ShapeDtypeStruct((HALO, d), jnp.float32)],
        in_specs=[blk(0), blk(0), blk(nj), blk(2 * nj), halo(nj), halo(2 * nj), nhalo(0), nhalo(0),
                  pl.BlockSpec((3, tc), lambda j, i: (0, j))],
        out_specs=[out, out, out, pl.BlockSpec((HALO, tc), lambda j, i: (0, j))],
        compiler_params=pltpu.CompilerParams(dimension_semantics=("parallel", "arbitrary")),
    )(da, z, z, z, z, z, da, z, conv_w)


@functools.partial(jax.custom_vjp, nondiff_argnums=(4,))
def _sconv_block(h, w_in, conv_w, w_out, name):
    return _sconv_block_fwd(h, w_in, conv_w, w_out, name)[0]


def _sconv_block_fwd(h, w_in, conv_w, w_out, name):
    hb = h.astype(jnp.bfloat16)
    z = _matmul(hb, w_in, "nn", name + "_in")
    a = _sconv_fwd_call(z, conv_w, name + "_gate")
    return _matmul(a, w_out, "nn", name + "_out"), (hb, z, a, w_in, conv_w, w_out)


def _sconv_block_bwd(name, saved, dy):
    hb, z, a, w_in, conv_w, w_out = saved
    dyb = dy.astype(jnp.bfloat16)
    da = _matmul(dyb, w_out, "nt", name + "_out_dx")
    dw_out = _matmul(a, dyb, "tn", name + "_out_dw", jnp.bfloat16)
    db, dc, dx, dcw = _sconv_bwd_call(da, z, conv_w, name + "_gate_bwd")
    dz = jnp.concatenate([db, dc, dx], axis=1)
    dh = _matmul(dz, w_in, "nt", name + "_in_dx")
    dw_in = _matmul(hb, dz, "tn", name + "_in_dw", jnp.bfloat16)
    return dh, dw_in, dcw[:3], dw_out


_sconv_block.defvjp(_sconv_block_fwd, _sconv_block_bwd)


def _to_heads(x, heads):
    t = x.shape[0]
    return x.reshape(t, heads, -1).transpose(1, 0, 2)


def _from_heads(x):
    h, t, d = x.shape
    return x.transpose(1, 0, 2).reshape(t, h * d)


def _partial_rope(x, positions):
    half = ROT_DIM // 2
    inv_freq = ROPE_THETA ** (-jnp.arange(half, dtype=jnp.float32) / half)
    ang = positions.astype(jnp.float32)[:, None] * inv_freq[None, :]
    cos = jnp.cos(ang)[:, None, :]
    sin = jnp.sin(ang)[:, None, :]
    x1, x2 = x[..., :half], x[..., half:ROT_DIM]
    return jnp.concatenate([x1 * cos - x2 * sin, x2 * cos + x1 * sin, x[..., ROT_DIM:]], axis=-1)


def _swa_probs(q_ref, kp_ref, kc_ref, sink_ref, n, w, scale):
    grp, d = q_ref.shape[0], q_ref.shape[2]
    q2 = q_ref[...].reshape(grp * w, d)
    kcat = jnp.concatenate([kp_ref[0], kc_ref[0]], axis=0)
    s3 = (_dot(q2, kcat, _NT) * scale).reshape(grp, w, 2 * w)
    qi = lax.broadcasted_iota(jnp.int32, s3.shape, 1)
    kj = lax.broadcasted_iota(jnp.int32, s3.shape, 2)
    diff = qi + w - kj
    allowed = (diff >= 0) & (diff < w) & ((kj >= w) | (n > 0))
    s3 = jnp.where(allowed, s3, NEG)
    sink = sink_ref[...]
    m = jnp.maximum(jnp.max(s3, axis=2, keepdims=True), sink)
    e = jnp.exp(s3 - m)
    es = jnp.exp(sink - m)
    z = jnp.sum(e, axis=2, keepdims=True) + es
    return q2, kcat, e / z, es / z


def _swa_fwd_call(q, k, v, sinks, name):
    qh, t, d = q.shape
    kv = k.shape[0]
    grp, w = qh // kv, SWA_WINDOW
    nblk = t // w
    scale = d ** -0.5

    def body(q_ref, kp_ref, kc_ref, vp_ref, vc_ref, sink_ref, o_ref):
        n = pl.program_id(1)
        _, _, p3, _ = _swa_probs(q_ref, kp_ref, kc_ref, sink_ref, n, w, scale)
        vcat = jnp.concatenate([vp_ref[0], vc_ref[0]], axis=0)
        o = _dot(p3.reshape(grp * w, 2 * w), vcat, _NN)
        o_ref[...] = o.reshape(grp, w, d).astype(o_ref.dtype)

    qspec = pl.BlockSpec((grp, w, d), lambda h, n: (h, n, 0))
    prev = pl.BlockSpec((1, w, d), lambda h, n: (h, jnp.maximum(n - 1, 0), 0))
    cur = pl.BlockSpec((1, w, d), lambda h, n: (h, n, 0))
    return pl.pallas_call(
        body, name=name, grid=(kv, nblk), out_shape=jax.ShapeDtypeStruct((qh, t, d), jnp.bfloat16),
        in_specs=[qspec, prev, cur, prev, cur, pl.BlockSpec((grp, 1, 1), lambda h, n: (h, 0, 0))],
        out_specs=qspec,
    )(q, k, k, v, v, sinks)


def _swa_bwd_call(q, k, v, sinks, do, name):
    qh, t, d = q.shape
    kv = k.shape[0]
    grp, w = qh // kv, SWA_WINDOW
    nblk = t // w
    scale = d ** -0.5

    def body(q_ref, kp_ref, kc_ref, vp_ref, vc_ref, sink_ref, do_ref, dq_ref, dk_ref, dv_ref, ds_ref,
             carry_k, carry_v, part_k, part_v):
        n = pl.program_id(1)

        @pl.when(n == 0)
        def _():
            carry_k[...] = jnp.zeros_like(carry_k)
            carry_v[...] = jnp.zeros_like(carry_v)
            ds_ref[...] = jnp.zeros_like(ds_ref)

        @pl.when(n < nblk)
        def _():
            q2, kcat, p3, ps = _swa_probs(q_ref, kp_ref, kc_ref, sink_ref, n, w, scale)
            vcat = jnp.concatenate([vp_ref[0], vc_ref[0]], axis=0)
            do2 = do_ref[...].reshape(grp * w, d)
            dp3 = _dot(do2, vcat, _NT).reshape(grp, w, 2 * w)
            rs = jnp.sum(p3 * dp3, axis=2, keepdims=True)
            ds2 = (p3 * (dp3 - rs)).reshape(grp * w, 2 * w)
            dsink = -jnp.sum(ps * rs, axis=1, keepdims=True)
            ds_ref[...] += jnp.broadcast_to(dsink, ds_ref.shape)
            dq_ref[...] = (_dot(ds2, kcat, _NN) * scale).reshape(grp, w, d)
            part_k[...] = _dot(ds2, q2, _TN) * scale
            part_v[...] = _dot(p3.reshape(grp * w, 2 * w), do2, _TN)

        @pl.when(n == nblk)
        def _():
            part_k[...] = jnp.zeros_like(part_k)
            part_v[...] = jnp.zeros_like(part_v)

        dk_ref[0] = carry_k[...] + part_k[0:w, :]
        dv_ref[0] = carry_v[...] + part_v[0:w, :]
        carry_k[...] = part_k[w:2 * w, :]
        carry_v[...] = part_v[w:2 * w, :]

    last = nblk - 1
    qspec = pl.BlockSpec((grp, w, d), lambda h, n: (h, jnp.minimum(n, last), 0))
    prev = pl.BlockSpec((1, w, d), lambda h, n: (h, jnp.maximum(jnp.minimum(n, last) - 1, 0), 0))
    cur = pl.BlockSpec((1, w, d), lambda h, n: (h, jnp.minimum(n, last), 0))
    kout = pl.BlockSpec((1, w, d), lambda h, n: (h, jnp.maximum(n - 1, 0), 0))
    return pl.pallas_call(
        body, name=name, grid=(kv, nblk + 1),
        out_shape=[jax.ShapeDtypeStruct((qh, t, d), jnp.float32), jax.ShapeDtypeStruct((kv, t, d), jnp.float32),
                   jax.ShapeDtypeStruct((kv, t, d), jnp.float32), jax.ShapeDtypeStruct((qh, 1, 128), jnp.float32)],
        in_specs=[qspec, prev, cur, prev, cur, pl.BlockSpec((grp, 1, 1), lambda h, n: (h, 0, 0)), qspec],
        out_specs=[qspec, kout, kout, pl.BlockSpec((grp, 1, 128), lambda h, n: (h, 0, 0))],
        scratch_shapes=[pltpu.VMEM((w, d), jnp.float32), pltpu.VMEM((w, d), jnp.float32),
                        pltpu.VMEM((2 * w, d), jnp.float32), pltpu.VMEM((2 * w, d), jnp.float32)],
        compiler_params=pltpu.CompilerParams(dimension_semantics=("parallel", "arbitrary")),
    )(q, k, k, v, v, sinks, do)


def _swa_split(z, positions, qh, kv, d):
    q, k, v = jnp.split(z, [qh * d, qh * d + kv * d], axis=-1)
    t = z.shape[0]
    q = _partial_rope(q.reshape(t, qh, d), positions).transpose(1, 0, 2)
    k = _partial_rope(k.reshape(t, kv, d), positions).transpose(1, 0, 2)
    v = v.reshape(t, kv, d).transpose(1, 0, 2)
    return q, k, v


@functools.partial(jax.custom_vjp, nondiff_argnums=(5,))
def _swa_block(h, positions, w_in, w_out, sinks, name):
    return _swa_block_fwd(h, positions, w_in, w_out, sinks, name)[0]


def _swa_block_fwd(h, positions, w_in, w_out, sinks, name):
    d = SWA_HEAD_DIM
    qh = h.shape[1] // d
    kv = qh // SWA_GROUP
    hb = h.astype(jnp.bfloat16)
    z = _matmul(hb, w_in, "nn", name + "_in")
    q, k, v = [a.astype(jnp.bfloat16) for a in _swa_split(z, positions, qh, kv, d)]
    o = _swa_fwd_call(q, k, v, sinks.reshape(qh, 1, 1), name + "_attn")
    a = _from_heads(o)
    return _matmul(a, w_out, "nn", name + "_out"), (hb, positions, z, q, k, v, a, w_in, w_out, sinks)


def _swa_block_bwd(name, saved, dy):
    hb, positions, z, q, k, v, a, w_in, w_out, sinks = saved
    d = SWA_HEAD_DIM
    qh = q.shape[0]
    kv = k.shape[0]
    dyb = dy.astype(jnp.bfloat16)
    da = _matmul(dyb, w_out, "nt", name + "_out_dx")
    dw_out = _matmul(a, dyb, "tn", name + "_out_dw", jnp.bfloat16)
    do = _to_heads(da, qh)
    dq, dk, dv, dsinks = _swa_bwd_call(q, k, v, sinks.reshape(qh, 1, 1), do, name + "_attn_bwd")
    _, split_vjp = jax.vjp(lambda zz: _swa_split(zz, positions, qh, kv, d), z)
    dz = split_vjp((dq, dk, dv))[0].astype(jnp.bfloat16)
    dh = _matmul(dz, w_in, "nt", name + "_in_dx")
    dw_in = _matmul(hb, dz, "tn", name + "_in_dw", jnp.bfloat16)
    return dh, None, dw_in, dw_out, dsinks[:, 0, 0].reshape(sinks.shape)


_swa_block.defvjp(_swa_block_fwd, _swa_block_bwd)


def _fox_fwd_call(q, k, v, g, bias, name):
    hh, nb, tb, d = q.shape
    scale = d ** -0.5

    def body(q_ref, k_ref, v_ref, g_ref, b_ref, og_ref, o_ref, lse_ref):
        i = pl.program_id(1)
        qv = q_ref[0, 0]
        rows = lax.broadcasted_iota(jnp.int32, (tb, tb), 0)
        cols = lax.broadcasted_iota(jnp.int32, (tb, tb), 1)

        def step(kb, carry, masked):
            m, l, acc = carry
            s = _dot(qv, k_ref[0, kb], _NT) * scale + b_ref[0, kb]
            if masked:
                s = jnp.where(cols <= rows, s, NEG)
            m_new = jnp.maximum(m, jnp.max(s, axis=1, keepdims=True))
            alpha = jnp.exp(m - m_new)
            p = jnp.exp(s - m_new)
            return m_new, alpha * l + jnp.sum(p, axis=1, keepdims=True), alpha * acc + _dot(p, v_ref[0, kb], _NN)

        init = (jnp.full((tb, 1), NEG, jnp.float32), jnp.zeros((tb, 1), jnp.float32), jnp.zeros((tb, d), jnp.float32))
        carry = lax.fori_loop(0, i, lambda kb, c: step(kb, c, False), init)
        m, l, acc = step(i, carry, True)
        o = acc / l
        o_ref[0, 0] = o
        og_ref[0, 0] = (o * _sigmoid(g_ref[0, 0])).astype(og_ref.dtype)
        lse_ref[0, 0] = _col_to_row(m + jnp.log(l))

    blk = pl.BlockSpec((1, 1, tb, d), lambda h, i: (h, i, 0, 0))
    whole = pl.BlockSpec((1, nb, tb, d), lambda h, i: (h, 0, 0, 0))
    rowblk = pl.BlockSpec((1, 1, 1, tb), lambda h, i: (h, i, 0, 0))
    return pl.pallas_call(
        body, name=name, grid=(hh, nb),
        out_shape=[jax.ShapeDtypeStruct(q.shape, jnp.bfloat16), jax.ShapeDtypeStruct(q.shape, jnp.float32),
                   jax.ShapeDtypeStruct((hh, nb, 1, tb), jnp.float32)],
        in_specs=[blk, whole, whole, blk, pl.BlockSpec((1, nb, 1, tb), lambda h, i: (h, 0, 0, 0))],
        out_specs=[blk, blk, rowblk],
    )(q, k, v, g, bias)


def _fox_dq_call(q, k, v, g, o, dog, bias, lse, name):
    hh, nb, tb, d = q.shape
    scale = d ** -0.5

    def body(q_ref, k_ref, v_ref, g_ref, o_ref, dog_ref, b_ref, lse_ref, dq_ref, dg_ref, do_ref, delta_ref):
        i = pl.program_id(1)
        qv = q_ref[0, 0]
        sig = _sigmoid(g_ref[0, 0])
        ov, dogv = o_ref[0, 0], dog_ref[0, 0]
        do = dogv * sig
        dg_ref[0, 0] = dogv * ov * sig * (1.0 - sig)
        do_ref[0, 0] = do.astype(do_ref.dtype)
        delta = jnp.sum(do * ov, axis=1, keepdims=True)
        lse = _row_to_col(lse_ref[0, 0])
        rows = lax.broadcasted_iota(jnp.int32, (tb, tb), 0)
        cols = lax.broadcasted_iota(jnp.int32, (tb, tb), 1)

        def step(kb, carry, masked):
            dq, spdp, sp = carry
            s = _dot(qv, k_ref[0, kb], _NT) * scale + b_ref[0, kb]
            p = jnp.exp(s - lse)
            if masked:
                p = jnp.where(cols <= rows, p, 0.0)
            dp = _dot(do, v_ref[0, kb], _NT)
            spdp = spdp + jnp.sum(p * dp, axis=1, keepdims=True)
            sp = sp + jnp.sum(p, axis=1, keepdims=True)
            return dq + _dot(p * (dp - delta), k_ref[0, kb], _NN), spdp, sp

        zcol = jnp.zeros((tb, 1), jnp.float32)
        carry = lax.fori_loop(0, i, lambda kb, c: step(kb, c, False), (jnp.zeros((tb, d), jnp.float32), zcol, zcol))
        dq, spdp, sp = step(i, carry, True)
        dq_ref[0, 0] = dq * scale
        delta_ref[0, 0] = _col_to_row(spdp / sp)

    blk = pl.BlockSpec((1, 1, tb, d), lambda h, i: (h, i, 0, 0))
    whole = pl.BlockSpec((1, nb, tb, d), lambda h, i: (h, 0, 0, 0))
    rowblk = pl.BlockSpec((1, 1, 1, tb), lambda h, i: (h, i, 0, 0))
    return pl.pallas_call(
        body, name=name, grid=(hh, nb),
        out_shape=[jax.ShapeDtypeStruct(q.shape, jnp.float32), jax.ShapeDtypeStruct(q.shape, jnp.float32),
                   jax.ShapeDtypeStruct(q.shape, jnp.bfloat16), jax.ShapeDtypeStruct((hh, nb, 1, tb), jnp.float32)],
        in_specs=[blk, whole, whole, blk, blk, blk, pl.BlockSpec((1, nb, 1, tb), lambda h, i: (h, 0, 0, 0)), rowblk],
        out_specs=[blk, blk, blk, rowblk],
    )(q, k, v, g, o, dog, bias, lse)


def _fox_dkv_call(q, k, v, do, bias, lse, delta, name):
    hh, nb, tb, d = q.shape
    scale = d ** -0.5

    def body(q_ref, k_ref, v_ref, do_ref, b_ref, lse_ref, delta_ref, dk_ref, dv_ref, db_ref):
        j = pl.program_id(1)
        kv_, vv = k_ref[0, 0], v_ref[0, 0]
        bcol = _row_to_col(b_ref[0, 0])
        rows = lax.broadcasted_iota(jnp.int32, (tb, tb), 0)
        cols = lax.broadcasted_iota(jnp.int32, (tb, tb), 1)

        def step(qb, carry, masked):
            dk, dv, db = carry
            st = _dot(kv_, q_ref[0, qb], _NT) * scale + bcol - lse_ref[0, qb]
            pt = jnp.exp(st)
            if masked:
                pt = jnp.where(rows <= cols, pt, 0.0)
            dv = dv + _dot(pt, do_ref[0, qb], _NN)
            dpt = _dot(vv, do_ref[0, qb], _NT)
            dst = pt * (dpt - delta_ref[0, qb])
            dk = dk + _dot(dst, q_ref[0, qb], _NN)
            return dk, dv, db + jnp.sum(dst, axis=1, keepdims=True)

        init = (jnp.zeros((tb, d), jnp.float32), jnp.zeros((tb, d), jnp.float32), jnp.zeros((tb, 1), jnp.float32))
        carry = step(j, init, True)
        dk, dv, db = lax.fori_loop(j + 1, nb, lambda qb, c: step(qb, c, False), carry)
        dk_ref[0, 0] = dk * scale
        dv_ref[0, 0] = dv
        db_ref[0, 0] = _col_to_row(db)

    blk = pl.BlockSpec((1, 1, tb, d), lambda h, j: (h, j, 0, 0))
    whole = pl.BlockSpec((1, nb, tb, d), lambda h, j: (h, 0, 0, 0))
    rowblk = pl.BlockSpec((1, 1, 1, tb), lambda h, j: (h, j, 0, 0))
    rowwhole = pl.BlockSpec((1, nb, 1, tb), lambda h, j: (h, 0, 0, 0))
    return pl.pallas_call(
        body, name=name, grid=(hh, nb),
        out_shape=[jax.ShapeDtypeStruct(q.shape, jnp.float32), jax.ShapeDtypeStruct(q.shape, jnp.float32),
                   jax.ShapeDtypeStruct((hh, nb, 1, tb), jnp.float32)],
        in_specs=[whole, blk, blk, whole, rowblk, rowwhole, rowwhole],
        out_specs=[blk, blk, rowblk],
    )(q, k, v, do, bias, lse, delta)


def _fox_bias(f_logit, b_f):
    return -jnp.cumsum(jax.nn.log_sigmoid(f_logit + b_f), axis=0).T


@functools.partial(jax.custom_vjp, nondiff_argnums=(5,))
def _fox_block(h, w_qkvg, w_f, b_f, w_out, name):
    return _fox_block_fwd(h, w_qkvg, w_f, b_f, w_out, name)[0]


def _fox_block_fwd(h, w_qkvg, w_f, b_f, w_out, name):
    t, dm = h.shape
    d = FOX_HEAD_DIM
    heads = dm // d
    tb = _tile(t, ATT_BLOCK)
    nb = t // tb
    hb = h.astype(jnp.bfloat16)
    z = _matmul(hb, w_qkvg, "nn", name + "_in")
    f_logit = _matmul(hb, w_f, "nn", name + "_inf")[:, :heads]
    z4 = z.reshape(t, 4, heads, d).transpose(1, 2, 0, 3).reshape(4, heads, nb, tb, d)
    q, k, v = [z4[i].astype(jnp.bfloat16) for i in range(3)]
    g = z4[3]
    bias = _fox_bias(f_logit, b_f).reshape(heads, nb, 1, tb)
    og, o, lse = _fox_fwd_call(q, k, v, g, bias, name + "_attn")
    a = _from_heads(og.reshape(heads, t, d))
    return _matmul(a, w_out, "nn", name + "_out"), (hb, q, k, v, g, bias, o, lse, a, f_logit, b_f, w_qkvg, w_f, w_out)


def _fox_block_bwd(name, saved, dy):
    hb, q, k, v, g, bias, o, lse, a, f_logit, b_f, w_qkvg, w_f, w_out = saved
    heads, nb, tb, d = q.shape
    t = nb * tb
    dyb = dy.astype(jnp.bfloat16)
    da = _matmul(dyb, w_out, "nt", name + "_out_dx")
    dw_out = _matmul(a, dyb, "tn", name + "_out_dw", jnp.bfloat16)
    dog = _to_heads(da, heads).reshape(heads, nb, tb, d)
    dq, dg, do, delta = _fox_dq_call(q, k, v, g, o, dog, bias, lse, name + "_attn_dq")
    dk, dv, dbias = _fox_dkv_call(q, k, v, do, bias, lse, delta, name + "_attn_dkv")
    dz = jnp.stack([dq, dk, dv, dg]).reshape(4, heads, t, d).transpose(2, 0, 1, 3).reshape(t, 4 * heads * d)
    dz = dz.astype(jnp.bfloat16)
    _, bias_vjp = jax.vjp(_fox_bias, f_logit, b_f)
    dfl, db_f = bias_vjp(dbias.reshape(heads, t))
    dflp = jnp.pad(dfl, ((0, 0), (0, w_f.shape[1] - heads))).astype(jnp.bfloat16)
    dh = _matmul(dz, w_qkvg, "nt", name + "_in_dx") + _matmul(dflp, w_f, "nt", name + "_inf_dx")
    dw_qkvg = _matmul(hb, dz, "tn", name + "_in_dw", jnp.bfloat16)
    dw_f = _matmul(hb, dflp, "tn", name + "_inf_dw", jnp.bfloat16)
    return dh, dw_qkvg, dw_f, db_f, dw_out


_fox_block.defvjp(_fox_block_fwd, _fox_block_bwd)


def _group_cumsum(x, c, pos):
    sh = 1
    while sh < c:
        x = x + jnp.where(pos >= sh, pltpu.roll(x, sh, 0), 0.0)
        sh *= 2
    return x


def _group_rcumsum(x, c, pos):
    n = x.shape[0]
    sh = 1
    while sh < c:
        x = x + jnp.where(pos < c - sh, pltpu.roll(x, n - sh, 0), 0.0)
        sh *= 2
    return x


def _hgrn_prep(zq_ref, zf_ref, zi_ref, lb_ref, b_scr, rows, c):
    nb = rows // c
    zq, zf = zq_ref[...], zf_ref[...]
    lbv = lb_ref[...]
    sq = _sigmoid(zq)
    sf = _sigmoid(zf)
    f = lbv + (1.0 - lbv) * sf
    pos = lax.broadcasted_iota(jnp.int32, (rows, 128), 0) % c
    b = _group_cumsum(jnp.log(f), c, pos)
    sh = (nb, c, 128)
    q3, k3, v3, b3 = (zq * sq).reshape(sh), (1.0 - f).reshape(sh), zi_ref[...].reshape(sh), b.reshape(sh)
    b_scr[...] = b3
    glast = b_scr[:, c - 1:c, :]
    eb = jnp.exp(b3)
    ek = jnp.exp(glast - b3)
    return dict(zq=zq, sq=sq, sf=sf, f=f, pos=pos, q3=q3, k3=k3, v3=v3, b3=b3, eb=eb, ek=ek,
                qt=q3 * eb, kh=k3 * ek, dec=jnp.exp(glast))


def _hgrn_fwd_call(z, lb, norm_g, name):
    t, w4 = z.shape
    w = w4 // 4
    heads = w // HGRN_EXPAND
    rows, c = _tile(t, HGRN_ROWS), HGRN_SUB
    nb, ntb = rows // c, t // rows

    def body(zq_ref, zf_ref, zi_ref, zg_ref, lb_ref, ng_ref, og_ref, o_ref, ck_ref, st_ref, b_scr, k_scr, v_scr):
        tb = pl.program_id(1)

        @pl.when(tb == 0)
        def _():
            st_ref[...] = jnp.zeros_like(st_ref)

        ck_ref[0, 0] = st_ref[...]
        p = _hgrn_prep(zq_ref, zf_ref, zi_ref, lb_ref, b_scr, rows, c)
        k_scr[...] = p["k3"]
        v_scr[...] = p["v3"]
        tpos = lax.broadcasted_iota(jnp.int32, (nb, c, 128), 1)
        od = jnp.zeros((nb, c, 128), jnp.float32)
        for s in range(c):
            bs, ks, vs = b_scr[:, s:s + 1, :], k_scr[:, s:s + 1, :], v_scr[:, s:s + 1, :]
            e = jnp.exp(jnp.minimum(p["b3"] - bs, 0.0))
            a = jnp.sum(jnp.where(tpos >= s, p["q3"] * ks * e, 0.0), axis=2, keepdims=True)
            od = od + a * vs
        st = st_ref[...]
        for i in range(nb):
            o_ref[pl.ds(i * c, c), :] = _dot(p["qt"][i], st, _NT) + od[i]
            st = st * p["dec"][i] + _dot(p["v3"][i], p["kh"][i], _TN)
        st_ref[...] = st
        o = o_ref[...]
        rstd = lax.rsqrt(jnp.mean(o * o, axis=1, keepdims=True) + RMS_EPS)
        zg = zg_ref[...]
        og_ref[...] = (o * rstd * ng_ref[...] * (zg * _sigmoid(zg))).astype(og_ref.dtype)

    def col(off):
        return pl.BlockSpec((rows, 128), lambda h, tb: (tb, h + off))

    out = pl.BlockSpec((rows, 128), lambda h, tb: (tb, h))
    return pl.pallas_call(
        body, name=name, grid=(heads, ntb),
        out_shape=[jax.ShapeDtypeStruct((t, w), jnp.bfloat16), jax.ShapeDtypeStruct((t, w), jnp.float32),
                   jax.ShapeDtypeStruct((heads, ntb, 128, 128), jnp.float32)],
        in_specs=[col(0), col(heads), col(2 * heads), col(3 * heads), pl.BlockSpec((1, 128), lambda h, tb: (0, h)),
                  pl.BlockSpec((1, 128), lambda h, tb: (0, 0))],
        out_specs=[out, out, pl.BlockSpec((1, 1, 128, 128), lambda h, tb: (h, tb, 0, 0))],
        scratch_shapes=[pltpu.VMEM((128, 128), jnp.float32)] + [pltpu.VMEM((nb, c, 128), jnp.float32)] * 3,
        compiler_params=pltpu.CompilerParams(dimension_semantics=("parallel", "arbitrary")),
    )(z, z, z, z, lb, norm_g)


def _hgrn_bwd_call(z, lb, norm_g, o_raw, dog, ck, name):
    t, w4 = z.shape
    w = w4 // 4
    heads = w // HGRN_EXPAND
    rows, c = _tile(t, HGRN_ROWS), HGRN_SUB
    nb, ntb = rows // c, t // rows

    def body(zq_ref, zf_ref, zi_ref, zg_ref, lb_ref, ng_ref, o_ref, dog_ref, ck_ref,
             dzq_ref, dzf_ref, dzi_ref, dzg_ref, dlb_ref, dng_ref,
             dst_ref, s_store, b_scr, k_scr, v_scr, dqt_scr, dkh_scr, dv_scr, ddec_scr):
        tb = pl.program_id(1)

        @pl.when(tb == 0)
        def _():
            dst_ref[...] = jnp.zeros_like(dst_ref)
            dlb_ref[...] = jnp.zeros_like(dlb_ref)
            dng_ref[...] = jnp.zeros_like(dng_ref)

        p = _hgrn_prep(zq_ref, zf_ref, zi_ref, lb_ref, b_scr, rows, c)
        q3, k3, v3, b3 = p["q3"], p["k3"], p["v3"], p["b3"]

        o = o_ref[...]
        rstd = lax.rsqrt(jnp.mean(o * o, axis=1, keepdims=True) + RMS_EPS)
        xh = o * rstd
        zg = zg_ref[...]
        sg = _sigmoid(zg)
        ng = ng_ref[...]
        dogv = dog_ref[...]
        don = dogv * (zg * sg)
        dzg_ref[...] = (dogv * (xh * ng) * (sg * (1.0 + zg * (1.0 - sg)))).astype(dzg_ref.dtype)
        dng_ref[0] += jnp.sum(don * xh, axis=0, keepdims=True)
        dyg = don * ng
        do = rstd * (dyg - xh * jnp.mean(dyg * xh, axis=1, keepdims=True))
        do3 = do.reshape(nb, c, 128)

        st = ck_ref[0, 0]
        for i in range(nb):
            s_store[i] = st
            st = st * p["dec"][i] + _dot(v3[i], p["kh"][i], _TN)

        dst = dst_ref[...]
        for i in reversed(range(nb)):
            sl = pl.ds(i * c, c)
            st_i = s_store[i]
            dqt_scr[sl, :] = _dot(do3[i], st_i, _NN)
            dv_scr[sl, :] = _dot(p["kh"][i], dst, _NT)
            dkh_scr[sl, :] = _dot(v3[i], dst, _NN)
            ddec_scr[i] = jnp.broadcast_to(jnp.sum(st_i * dst, axis=0, keepdims=True), (8, 128))
            dst = dst * p["dec"][i] + _dot(do3[i], p["qt"][i], _TN)
        dst_ref[...] = dst

        k_scr[...] = k3
        v_scr[...] = v3
        tpos = lax.broadcasted_iota(jnp.int32, (nb, c, 128), 1)
        zero = jnp.zeros((nb, c, 128), jnp.float32)
        dqd, dkd, dvd, dbd = zero, zero, zero, zero
        for s in range(c):
            bs, ks, vs = b_scr[:, s:s + 1, :], k_scr[:, s:s + 1, :], v_scr[:, s:s + 1, :]
            e = jnp.where(tpos >= s, jnp.exp(jnp.minimum(b3 - bs, 0.0)), 0.0)
            qe = q3 * e
            a = jnp.sum(qe * ks, axis=2, keepdims=True)
            da = jnp.sum(do3 * vs, axis=2, keepdims=True)
            y = da * (ks * e)
            dqd = dqd + y
            dbd = dbd + y * q3
            dks = jnp.sum(da * qe, axis=1, keepdims=True)
            dvs = jnp.sum(a * do3, axis=1, keepdims=True)
            at_s = tpos == s
            dkd = dkd + jnp.where(at_s, dks, 0.0)
            dvd = dvd + jnp.where(at_s, dvs, 0.0)
            dbd = dbd - jnp.where(at_s, dks * ks, 0.0)

        dqt3 = dqt_scr[...].reshape(nb, c, 128)
        dkh3 = dkh_scr[...].reshape(nb, c, 128)
        ddec = ddec_scr[:, 0:1, :]
        dq_act = dqt3 * p["eb"] + dqd
        dk = dkh3 * p["ek"] + dkd
        khd = p["kh"] * dkh3
        db = p["qt"] * dqt3 - khd + dbd
        dglast = jnp.sum(khd, axis=1, keepdims=True) + ddec * p["dec"]
        db = db + jnp.where(tpos == c - 1, dglast, 0.0)
        dlogf = _group_rcumsum(db.reshape(rows, 128), c, p["pos"])

        zq, sq, sf, f = p["zq"], p["sq"], p["sf"], p["f"]
        lbv = lb_ref[...]
        dzq_ref[...] = (dq_act.reshape(rows, 128) * (sq * (1.0 + zq * (1.0 - sq)))).astype(dzq_ref.dtype)
        df = dlogf / f - dk.reshape(rows, 128)
        dzf_ref[...] = (df * (1.0 - lbv) * (sf * (1.0 - sf))).astype(dzf_ref.dtype)
        dlb_ref[...] += jnp.sum(df * (1.0 - sf), axis=0, keepdims=True)
        dzi_ref[...] = (dv_scr[...] + dvd.reshape(rows, 128)).astype(dzi_ref.dtype)

    last = ntb - 1

    def col(off):
        return pl.BlockSpec((rows, 128), lambda h, tb: (last - tb, h + off))

    out = pl.BlockSpec((rows, 128), lambda h, tb: (last - tb, h))
    lbs = pl.BlockSpec((1, 128), lambda h, tb: (0, h))
    big = pltpu.VMEM((rows, 128), jnp.float32)
    return pl.pallas_call(
        body, name=name, grid=(heads, ntb),
        out_shape=[jax.ShapeDtypeStruct((t, w), jnp.bfloat16)] * 4
        + [jax.ShapeDtypeStruct((1, w), jnp.float32), jax.ShapeDtypeStruct((heads, 1, 128), jnp.float32)],
        in_specs=[col(0), col(heads), col(2 * heads), col(3 * heads), lbs, pl.BlockSpec((1, 128), lambda h, tb: (0, 0)),
                  out, out, pl.BlockSpec((1, 1, 128, 128), lambda h, tb: (h, last - tb, 0, 0))],
        out_specs=[out, out, out, out, lbs, pl.BlockSpec((1, 1, 128), lambda h, tb: (h, 0, 0))],
        scratch_shapes=[pltpu.VMEM((128, 128), jnp.float32), pltpu.VMEM((nb, 128, 128), jnp.float32)]
        + [pltpu.VMEM((nb, c, 128), jnp.float32)] * 3 + [big] * 3 + [pltpu.VMEM((nb, 8, 128), jnp.float32)],
        compiler_params=pltpu.CompilerParams(dimension_semantics=("parallel", "arbitrary")),
    )(z, z, z, z, lb, norm_g, o_raw, dog, ck)


@functools.partial(jax.custom_vjp, nondiff_argnums=(5,))
def _hgrn_block(h, w_in, w_out, norm_g, lb, name):
    return _hgrn_block_fwd(h, w_in, w_out, norm_g, lb, name)[0]


def _hgrn_block_fwd(h, w_in, w_out, norm_g, lb, name):
    hb = h.astype(jnp.bfloat16)
    z = _matmul(hb, w_in, "nn", name + "_in")
    og, o_raw, ck = _hgrn_fwd_call(z, lb.reshape(1, -1), norm_g.reshape(1, -1), name + "_scan")
    return _matmul(og, w_out, "nn", name + "_out"), (hb, z, og, o_raw, ck, w_in, w_out, norm_g, lb)


def _hgrn_block_bwd(name, saved, dy):
    hb, z, og, o_raw, ck, w_in, w_out, norm_g, lb = saved
    dyb = dy.astype(jnp.bfloat16)
    dog = _matmul(dyb, w_out, "nt", name + "_out_dx")
    dw_out = _matmul(og, dyb, "tn", name + "_out_dw", jnp.bfloat16)
    dzq, dzf, dzi, dzg, dlb, dng = _hgrn_bwd_call(z, lb.reshape(1, -1), norm_g.reshape(1, -1), o_raw, dog, ck,
                                                  name + "_scan_bwd")
    dz = jnp.concatenate([dzq, dzf, dzi, dzg], axis=1)
    dh = _matmul(dz, w_in, "nt", name + "_in_dx")
    dw_in = _matmul(hb, dz, "tn", name + "_in_dw", jnp.bfloat16)
    return dh, dw_in, dw_out, jnp.sum(dng, axis=0).reshape(norm_g.shape), dlb.reshape(lb.shape)


_hgrn_block.defvjp(_hgrn_block_fwd, _hgrn_block_bwd)


def _layer_plan(sharded):
    mixers = [
        [("hgrn_w_in", "col"), ("hgrn_w_out", "row")],
        [("swa_w_in", "col"), ("swa_w_out", "row")],
        [("sc_w_in", "col"), ("sc_conv_w", "exact"), ("sc_w_out", "row")],
        [("fox_w_in", "col"), ("fox_w_out", "row")],
    ]
    depth = sharded["ffn_w_up"].shape[0]
    plan = []
    for i in range(depth):
        m, j = i % len(mixers), i // len(mixers)
        entries = [(n, j, kind) for n, kind in mixers[m]]
        entries += [("ffn_w_up", i, "col"), ("ffn_conv_w", i, "exact"), ("ffn_w_down", i, "row")]
        plan.append(entries)
    return plan


def _local_loss(sharded, repl, x, positions, loss_target):
    x = x[0]
    loss_target = loss_target[0]
    d = x.shape[-1]
    plan = _layer_plan(sharded)
    lb_table = jnp.cumsum(jax.nn.softmax(repl["hgrn_lb_param"], axis=0), axis=0)
    for i, entries in enumerate(plan):
        shards = tuple(sharded[n][j] for n, j, _ in entries)
        kinds = tuple(kind for _, _, kind in entries)
        full = _make_layer_gather([s.shape for s in shards], kinds, "layer%d" % i)(shards)
        wts = {n: f for (n, _, _), f in zip(entries, full)}
        m, j = i % 4, i // 4
        tag = "l%d" % i
        hn = _norm(x, repl["mix_pre_g"][i], tag + "_mix_pre")
        if m == 0:
            y = _hgrn_block(hn, wts["hgrn_w_in"], wts["hgrn_w_out"], repl["hgrn_norm_g"][j], lb_table[i], tag + "_hgrn")
        elif m == 1:
            y = _swa_block(hn, positions, wts["swa_w_in"], wts["swa_w_out"], repl["swa_sinks"][j], tag + "_swa")
        elif m == 2:
            y = _sconv_block(hn, wts["sc_w_in"], wts["sc_conv_w"], wts["sc_w_out"], tag + "_sc")
        else:
            w_in = wts["fox_w_in"]
            heads = d // FOX_HEAD_DIM
            w_qkvg = jnp.concatenate([w_in[:, :3 * d], w_in[:, 3 * d + heads:]], axis=1)
            w_f = jnp.pad(w_in[:, 3 * d:3 * d + heads], ((0, 0), (0, 128 - heads)))
            y = _fox_block(hn, w_qkvg, w_f, repl["fox_b_f"][j], wts["fox_w_out"], tag + "_fox")
        x = _norm_add(y, repl["mix_post_g"][i], x, tag + "_mix_post")
        hn = _norm(x, repl["ffn_pre_g"][i], tag + "_ffn_pre")
        y = _ffn_block(hn, wts["ffn_w_up"], wts["ffn_conv_w"], repl["ffn_conv_b"][i], wts["ffn_w_down"], tag + "_ffn")
        x = _norm_add(y, repl["ffn_post_g"][i], x, tag + "_ffn_post")
    err = jnp.square(x - loss_target)
    return 0.5 * jnp.sum(jnp.mean(err, axis=-1))


SHARDED = ["hgrn_w_in", "hgrn_w_out", "swa_w_in", "swa_w_out", "sc_w_in", "sc_conv_w", "sc_w_out",
           "fox_w_in", "fox_w_out", "ffn_w_up", "ffn_conv_w", "ffn_w_down"]
REPLICATED = ["mix_pre_g", "mix_post_g", "ffn_pre_g", "ffn_post_g", "hgrn_norm_g", "hgrn_lb_param",
              "swa_sinks", "fox_b_f", "ffn_conv_b"]
WEIGHTS = ["mix_pre_g", "mix_post_g", "ffn_pre_g", "ffn_post_g", "hgrn_w_in", "hgrn_w_out", "hgrn_norm_g",
           "hgrn_lb_param", "swa_w_in", "swa_w_out", "swa_sinks", "sc_w_in", "sc_conv_w", "sc_w_out",
           "fox_w_in", "fox_b_f", "fox_w_out", "ffn_w_up", "ffn_conv_w", "ffn_conv_b", "ffn_w_down"]


def _sum_replicated(loss, grads):
    parts = [loss.reshape(1)] + [grads[n].reshape(-1) for n in REPLICATED]
    flat = jnp.concatenate(parts)
    n = flat.shape[0]
    cols = 1024
    rows = -(-n // cols)
    rows += (-rows) % 8
    flat = jnp.pad(flat, (0, rows * cols - n)).reshape(rows, cols)
    total = _sum_blocks(_all_gather(flat, "small_gather"), "small_sum").reshape(-1)
    out, off = {}, 1
    for name in REPLICATED:
        size = grads[name].size
        out[name] = total[off:off + size].reshape(grads[name].shape)
        off += size
    return total[0], out


def kernel(x, positions, mix_pre_g, mix_post_g, ffn_pre_g, ffn_post_g, hgrn_w_in, hgrn_w_out, hgrn_norm_g, hgrn_lb_param, swa_w_in, swa_w_out, swa_sinks, sc_w_in, sc_conv_w, sc_w_out, fox_w_in, fox_b_f, fox_w_out, ffn_w_up, ffn_conv_w, ffn_conv_b, ffn_w_down, loss_target, m_mix_pre_g, m_mix_post_g, m_ffn_pre_g, m_ffn_post_g, m_hgrn_w_in, m_hgrn_w_out, m_hgrn_norm_g, m_hgrn_lb_param, m_swa_w_in, m_swa_w_out, m_swa_sinks, m_sc_w_in, m_sc_conv_w, m_sc_w_out, m_fox_w_in, m_fox_b_f, m_fox_w_out, m_ffn_w_up, m_ffn_conv_w, m_ffn_conv_b, m_ffn_w_down, v_mix_pre_g, v_mix_post_g, v_ffn_pre_g, v_ffn_post_g, v_hgrn_w_in, v_hgrn_w_out, v_hgrn_norm_g, v_hgrn_lb_param, v_swa_w_in, v_swa_w_out, v_swa_sinks, v_sc_w_in, v_sc_conv_w, v_sc_w_out, v_fox_w_in, v_fox_b_f, v_fox_w_out, v_ffn_w_up, v_ffn_conv_w, v_ffn_conv_b, v_ffn_w_down):
    given = dict(locals())
    sharded = {n: given[n] for n in SHARDED}
    repl = {n: given[n] for n in REPLICATED}
    loss, (g_sharded, g_repl, grad_x) = jax.value_and_grad(_local_loss, argnums=(0, 1, 2))(
        sharded, repl, x, positions, loss_target)
    loss, g_repl = _sum_replicated(loss, g_repl)
    grads = {**g_sharded, **g_repl}
    delta, new_m, new_v = {}, {}, {}
    for n in WEIGHTS:
        delta[n], new_m[n], new_v[n] = _adamw(given[n], grads[n], given["m_" + n], given["v_" + n], "adamw_" + n)
    return (loss, grad_x, *[grads[n] for n in WEIGHTS], *[delta[n] for n in WEIGHTS],
            *[new_m[n] for n in WEIGHTS], *[new_v[n] for n in WEIGHTS])
```

```python
import functools
import math

import jax
import jax.numpy as jnp
from jax import lax
from jax.experimental import pallas as pl
from jax.experimental.pallas import tpu as pltpu

N_DEV = 8
MESH = pl.DeviceIdType.MESH
HBM_SPEC = pl.BlockSpec(memory_space=pltpu.HBM)
PACK_COLS = 512
PACK_ROW_BLOCK = 512

MM_TILE = 1024
MM_K_SINGLE = 2048
MM_K_TILE = 1536
MM_VMEM_BYTES = 48 * 1024 * 1024
ROW_BLOCK = 256
COL_BLOCK = 512
HALO = 8
ATT_BLOCK = 256
FOX_HEADS_PER_STEP = 2
HGRN_SUB = 16
HGRN_ROWS = 256
NEG = -1e30

RMS_EPS = 1e-6
HGRN_EXPAND = 128
SWA_HEAD_DIM = 64
SWA_GROUP = 8
SWA_WINDOW = 128
FOX_HEAD_DIM = 64
ROPE_THETA = 500000.0
ROT_DIM = SWA_HEAD_DIM // 4

ADAM_LR = 0.001
ADAM_B1 = 0.9
ADAM_B2 = 0.999
ADAM_EPS = 1e-08
ADAM_WD = 0.01
ADAM_STEP = 10

_NT = (((1,), (1,)), ((), ()))
_NN = (((1,), (0,)), ((), ()))
_TN = (((0,), (0,)), ((), ()))


def _tile(dim, pref):
    if dim <= pref:
        return dim
    t = pref - pref % 128
    while t >= 128:
        if dim % t == 0:
            return t
        t -= 128
    return dim


def _dot(a, b, dims):
    return lax.dot_general(a.astype(jnp.bfloat16), b.astype(jnp.bfloat16), dims, preferred_element_type=jnp.float32)


def _sigmoid(x):
    return 1.0 / (1.0 + jnp.exp(-x))


def _col_to_row(c):
    n = c.shape[0]
    eye = lax.broadcasted_iota(jnp.int32, (n, n), 0) == lax.broadcasted_iota(jnp.int32, (n, n), 1)
    return jnp.sum(jnp.where(eye, c, 0.0), axis=0, keepdims=True)


def _row_to_col(r):
    n = r.shape[1]
    eye = lax.broadcasted_iota(jnp.int32, (n, n), 0) == lax.broadcasted_iota(jnp.int32, (n, n), 1)
    return jnp.sum(jnp.where(eye, r, 0.0), axis=1, keepdims=True)


def _all_gather(x, name):
    r, c = x.shape

    def body(x_ref, out_ref, send_sems, recv_sems, local_sem):
        mx, my, mc = lax.axis_index("x"), lax.axis_index("y"), lax.axis_index("c")
        me, sibling = (mx, my, mc), (mx, my, 1 - mc)
        chips = [(1 - mx, my), (mx, 1 - my), (1 - mx, 1 - my)]

        def rows(px, py, pc):
            return out_ref.at[4 * px + 2 * py + pc]

        def copy(k, block, to, src=None):
            return pltpu.make_async_remote_copy(
                src_ref=rows(*block) if src is None else src, dst_ref=rows(*block),
                send_sem=send_sems.at[k], recv_sem=recv_sems.at[k], device_id=to, device_id_type=MESH)

        mine = pltpu.make_async_copy(x_ref, rows(*me), local_sem)
        mine.start()
        first = [copy(0, me, sibling, src=x_ref)]
        first += [copy(1 + j, me, (*chip, mc), src=x_ref) for j, chip in enumerate(chips)]
        for cp in first:
            cp.start()
        passed = [copy(4 + j, (*chip, mc), sibling) for j, chip in enumerate(chips)]
        for j, chip in enumerate(chips):
            copy(1 + j, (*chip, mc), me).wait_recv()
            passed[j].start()
        copy(0, sibling, me).wait_recv()
        for j, chip in enumerate(chips):
            copy(4 + j, (*chip, 1 - mc), me).wait_recv()
        for cp in first + passed:
            cp.wait_send()
        mine.wait()

    return pl.pallas_call(
        body, name=name, out_shape=jax.ShapeDtypeStruct((N_DEV, r, c), x.dtype),
        in_specs=[HBM_SPEC], out_specs=HBM_SPEC,
        scratch_shapes=[pltpu.SemaphoreType.DMA((7,)), pltpu.SemaphoreType.DMA((7,)), pltpu.SemaphoreType.DMA],
    )(x)


def _pair_exchange(x, name):
    n = x.shape[0]

    def body(x_ref, out_ref, send_sems, recv_sems):
        sibling = (lax.axis_index("x"), lax.axis_index("y"), 1 - lax.axis_index("c"))
        copies = [pltpu.make_async_remote_copy(
            src_ref=x_ref.at[j], dst_ref=out_ref.at[j], send_sem=send_sems.at[j], recv_sem=recv_sems.at[j],
            device_id=sibling, device_id_type=MESH) for j in range(n)]
        for cp in copies:
            cp.start()
        for cp in copies:
            cp.wait()

    return pl.pallas_call(
        body, name=name, out_shape=jax.ShapeDtypeStruct(x.shape, x.dtype),
        in_specs=[HBM_SPEC], out_specs=HBM_SPEC,
        scratch_shapes=[pltpu.SemaphoreType.DMA((n,)), pltpu.SemaphoreType.DMA((n,))],
    )(x)


def _chip_exchange(x, name):
    flips = [(1, 0), (0, 1), (1, 1)]

    def body(x_ref, out_ref, send_sems, recv_sems, local_sem):
        mx, my, mc = lax.axis_index("x"), lax.axis_index("y"), lax.axis_index("c")
        me = 2 * mx + my

        def chip(flip):
            return ((1 - mx) if flip[0] else mx, (1 - my) if flip[1] else my)

        def copy(k, dst_block):
            px, py = chip(flips[k])
            return pltpu.make_async_remote_copy(
                src_ref=x_ref.at[2 * px + py], dst_ref=out_ref.at[dst_block],
                send_sem=send_sems.at[k], recv_sem=recv_sems.at[k], device_id=(px, py, mc), device_id_type=MESH)

        mine = pltpu.make_async_copy(x_ref.at[me], out_ref.at[me], local_sem)
        mine.start()
        sends = [copy(k, me) for k in range(len(flips))]
        for cp in sends:
            cp.start()
        for k in range(len(flips)):
            px, py = chip(flips[k])
            copy(k, 2 * px + py).wait_recv()
        for cp in sends:
            cp.wait_send()
        mine.wait()

    return pl.pallas_call(
        body, name=name, out_shape=jax.ShapeDtypeStruct(x.shape, x.dtype),
        in_specs=[HBM_SPEC], out_specs=HBM_SPEC,
        scratch_shapes=[pltpu.SemaphoreType.DMA((3,)), pltpu.SemaphoreType.DMA((3,)), pltpu.SemaphoreType.DMA],
    )(x)


def _sum_blocks(x, name, out_dtype=jnp.float32):
    n, r, c = x.shape
    tr = PACK_ROW_BLOCK if r % PACK_ROW_BLOCK == 0 else r

    def body(x_ref, o_ref):
        acc = x_ref[0].astype(jnp.float32)
        for j in range(1, n):
            acc = acc + x_ref[j].astype(jnp.float32)
        o_ref[...] = acc.astype(o_ref.dtype)

    return pl.pallas_call(
        body, name=name, grid=(r // tr,), out_shape=jax.ShapeDtypeStruct((r, c), out_dtype),
        in_specs=[pl.BlockSpec((n, tr, c), lambda i: (0, i, 0))],
        out_specs=pl.BlockSpec((tr, c), lambda i: (i, 0)),
    )(x)


def _pair_sum(a, b, name):
    n, r, c = a.shape
    tr = PACK_ROW_BLOCK if r % PACK_ROW_BLOCK == 0 else r

    def body(a_ref, b_ref, o_ref):
        o_ref[...] = (a_ref[...].astype(jnp.float32) + b_ref[...].astype(jnp.float32)).astype(o_ref.dtype)

    spec = pl.BlockSpec((1, tr, c), lambda j, i: (j, i, 0))
    return pl.pallas_call(
        body, name=name, grid=(n, r // tr), out_shape=jax.ShapeDtypeStruct(a.shape, a.dtype),
        in_specs=[spec, spec], out_specs=spec,
    )(a, b)


def _reduce_scatter(x, name):
    _, r, c = x.shape
    mc = lax.axis_index("c")
    by_chip = x.reshape(4, 2, r, c)
    keep = lax.dynamic_index_in_dim(by_chip, mc, axis=1, keepdims=False)
    give = lax.dynamic_index_in_dim(by_chip, 1 - mc, axis=1, keepdims=False)
    paired = _pair_sum(keep, _pair_exchange(give, name + "_pair"), name + "_pair_sum")
    return _sum_blocks(_chip_exchange(paired, name + "_chips"), name + "_sum")


def _matmul(a, b, mode, name, out_dtype=jnp.float32):
    if mode == "nn":
        (m, k), (_, n) = a.shape, b.shape
    elif mode == "nt":
        (m, k), (n, _) = a.shape, b.shape
    else:
        (k, m), (_, n) = a.shape, b.shape
    tm, tn = _tile(m, MM_TILE), _tile(n, MM_TILE)
    tk = k if k <= MM_K_SINGLE else _tile(k, MM_K_TILE)
    nk = k // tk
    dims = {"nn": _NN, "nt": _NT, "tn": _TN}[mode]
    rows_inner = nk == 1 and (n // tn) * a.size < (m // tm) * b.size

    def body(a_ref, b_ref, o_ref, *scratch):
        part = lax.dot_general(a_ref[...], b_ref[...], dims, preferred_element_type=jnp.float32)
        if nk == 1:
            o_ref[...] = part.astype(o_ref.dtype)
            return
        acc_ref, = scratch
        kk = pl.program_id(2)

        @pl.when(kk == 0)
        def _():
            acc_ref[...] = part

        @pl.when(kk > 0)
        def _():
            acc_ref[...] += part

        @pl.when(kk == nk - 1)
        def _():
            o_ref[...] = acc_ref[...].astype(o_ref.dtype)

    def ij(g0, g1):
        return (g1, g0) if rows_inner else (g0, g1)

    def a_map(g0, g1, kk):
        i, _ = ij(g0, g1)
        return (kk, i) if mode == "tn" else (i, kk)

    def b_map(g0, g1, kk):
        _, j = ij(g0, g1)
        return (j, kk) if mode == "nt" else (kk, j)

    def o_map(g0, g1, kk):
        return ij(g0, g1)

    grid = (n // tn, m // tm, nk) if rows_inner else (m // tm, n // tn, nk)
    return pl.pallas_call(
        body, name=name, grid=grid, out_shape=jax.ShapeDtypeStruct((m, n), out_dtype),
        in_specs=[pl.BlockSpec((tk, tm) if mode == "tn" else (tm, tk), a_map),
                  pl.BlockSpec((tn, tk) if mode == "nt" else (tk, tn), b_map)],
        out_specs=pl.BlockSpec((tm, tn), o_map),
        scratch_shapes=[pltpu.VMEM((tm, tn), jnp.float32)] if nk > 1 else [],
        compiler_params=pltpu.CompilerParams(dimension_semantics=("parallel", "parallel", "arbitrary"),
                                             vmem_limit_bytes=MM_VMEM_BYTES),
    )(a, b)


def _pad_rows(flat, axis):
    n = flat.shape[axis]
    unit = PACK_COLS * PACK_ROW_BLOCK
    pad = (-n) % unit
    if pad:
        widths = [(0, 0)] * flat.ndim
        widths[axis] = (0, pad)
        flat = jnp.pad(flat, widths)
    return flat


def _gather_pack(shards, kinds):
    parts = []
    for s, kind in zip(shards, kinds):
        if kind == "exact":
            parts.append(lax.bitcast_convert_type(s, jnp.bfloat16).reshape(-1))
        elif kind == "colT":
            parts.append(s.T.astype(jnp.bfloat16).reshape(-1))
        else:
            parts.append(s.astype(jnp.bfloat16).reshape(-1))
    return _pad_rows(jnp.concatenate(parts), 0).reshape(-1, PACK_COLS)


def _gather_unpack(gathered, shapes, kinds):
    flat = gathered.reshape(N_DEV, -1)
    out, off = [], 0
    for shape, kind in zip(shapes, kinds):
        size = math.prod(shape)
        if kind == "colT":
            out.append(flat[:, off:off + size].reshape(N_DEV * shape[1], shape[0]))
            off += size
            continue
        if kind == "exact":
            piece = flat[:, off:off + 2 * size].reshape((N_DEV,) + tuple(shape) + (2,))
            piece = lax.bitcast_convert_type(piece, jnp.float32)
            off += 2 * size
            kind = "col"
        else:
            piece = flat[:, off:off + size].reshape((N_DEV,) + tuple(shape))
            off += size
        if kind == "col":
            piece = jnp.moveaxis(piece, 0, -2)
            piece = piece.reshape(piece.shape[:-2] + (N_DEV * shape[-1],))
        else:
            piece = piece.reshape((N_DEV * shape[0],) + tuple(shape[1:]))
        out.append(piece)
    return tuple(out)


def _scatter_pack(cts, shapes, kinds):
    parts = []
    for ct, shape, kind in zip(cts, shapes, kinds):
        ct = ct.astype(jnp.bfloat16)
        if kind in ("row", "colT"):
            piece = ct.reshape(N_DEV, -1)
        else:
            piece = ct.reshape(tuple(shape[:-1]) + (N_DEV, shape[-1]))
            piece = jnp.moveaxis(piece, -2, 0)
        parts.append(piece.reshape(N_DEV, -1))
    return _pad_rows(jnp.concatenate(parts, axis=1), 1).reshape(N_DEV, -1, PACK_COLS)


def _scatter_unpack(summed, shapes, kinds):
    flat = summed.reshape(-1)
    out, off = [], 0
    for shape, kind in zip(shapes, kinds):
        size = math.prod(shape)
        piece = flat[off:off + size]
        out.append(piece.reshape(shape[1], shape[0]).T if kind == "colT" else piece.reshape(shape))
        off += size
    return tuple(out)


def _make_layer_gather(shapes, kinds, tag):
    shapes = tuple(tuple(s) for s in shapes)

    def impl(shards):
        return _gather_unpack(_all_gather(_gather_pack(shards, kinds), tag + "_gather"), shapes, kinds)

    gather = jax.custom_vjp(impl)

    def fwd(shards):
        return impl(shards), None

    def bwd(_, cts):
        packed = _scatter_pack(cts, shapes, kinds)
        return (_scatter_unpack(_reduce_scatter(packed, tag + "_scatter"), shapes, kinds),)

    gather.defvjp(fwd, bwd)
    return gather


def _adamw(w, g, m, v, name):
    shape = w.shape
    cols = shape[-1]
    rows = math.prod(shape[:-1])
    tr = rows
    if rows > 512:
        for cand in (256, 128, 64, 32, 16, 8):
            if rows % cand == 0:
                tr = cand
                break
    c1 = 1.0 - ADAM_B1 ** ADAM_STEP
    c2 = 1.0 - ADAM_B2 ** ADAM_STEP

    def body(w_ref, g_ref, m_ref, v_ref, d_ref, nm_ref, nv_ref):
        gg = g_ref[...]
        nm = ADAM_B1 * m_ref[...] + (1.0 - ADAM_B1) * gg
        nv = ADAM_B2 * v_ref[...] + (1.0 - ADAM_B2) * (gg * gg)
        m_hat = nm / c1
        v_hat = nv / c2
        d_ref[...] = -ADAM_LR * (m_hat / (jnp.sqrt(v_hat) + ADAM_EPS) + ADAM_WD * w_ref[...])
        nm_ref[...] = nm
        nv_ref[...] = nv

    spec = pl.BlockSpec((tr, cols), lambda i: (i, 0))
    outs = pl.pallas_call(
        body, name=name, grid=(rows // tr,), out_shape=[jax.ShapeDtypeStruct((rows, cols), jnp.float32)] * 3,
        in_specs=[spec] * 4, out_specs=[spec] * 3,
    )(*[t.reshape(rows, cols) for t in (w, g, m, v)])
    return tuple(o.reshape(shape) for o in outs)


def _rms_fwd_call(x, g, res, name):
    t, d = x.shape
    tr = _tile(t, ROW_BLOCK)
    with_res = res is not None

    def body(*refs):
        if with_res:
            x_ref, g_ref, r_ref, y_ref = refs
        else:
            x_ref, g_ref, y_ref = refs
        xv = x_ref[...]
        rstd = lax.rsqrt(jnp.mean(xv * xv, axis=-1, keepdims=True) + RMS_EPS)
        y = xv * rstd * g_ref[...]
        if with_res:
            y = y + r_ref[...]
        y_ref[...] = y

    row = pl.BlockSpec((tr, d), lambda i: (i, 0))
    gspec = pl.BlockSpec((1, d), lambda i: (0, 0))
    ins = [x, g] + ([res] if with_res else [])
    return pl.pallas_call(
        body, name=name, grid=(t // tr,), out_shape=jax.ShapeDtypeStruct((t, d), jnp.float32),
        in_specs=[row, gspec] + ([row] if with_res else []), out_specs=row,
    )(*ins)


def _rms_bwd_call(x, g, dy, name):
    t, d = x.shape
    tr = _tile(t, ROW_BLOCK)

    def body(x_ref, g_ref, dy_ref, dx_ref, dg_ref):
        i = pl.program_id(0)
        xv = x_ref[...]
        rstd = lax.rsqrt(jnp.mean(xv * xv, axis=-1, keepdims=True) + RMS_EPS)
        xh = xv * rstd
        dyv = dy_ref[...]
        dyg = dyv * g_ref[...]
        dx_ref[...] = rstd * (dyg - xh * jnp.mean(dyg * xh, axis=-1, keepdims=True))

        @pl.when(i == 0)
        def _():
            dg_ref[...] = jnp.zeros_like(dg_ref)

        dg_ref[...] += jnp.sum(dyv * xh, axis=0, keepdims=True)

    row = pl.BlockSpec((tr, d), lambda i: (i, 0))
    gspec = pl.BlockSpec((1, d), lambda i: (0, 0))
    return pl.pallas_call(
        body, name=name, grid=(t // tr,),
        out_shape=[jax.ShapeDtypeStruct((t, d), jnp.float32), jax.ShapeDtypeStruct((1, d), jnp.float32)],
        in_specs=[row, gspec, row], out_specs=[row, gspec],
        compiler_params=pltpu.CompilerParams(dimension_semantics=("arbitrary",)),
    )(x, g, dy)


@functools.partial(jax.custom_vjp, nondiff_argnums=(3,))
def _norm_add(y, g, res, name):
    return _rms_fwd_call(y, g.reshape(1, -1), res, name + "_fwd")


def _norm_add_fwd(y, g, res, name):
    return _rms_fwd_call(y, g.reshape(1, -1), res, name + "_fwd"), (y, g)


def _norm_add_bwd(name, saved, dout):
    y, g = saved
    dy, dg = _rms_bwd_call(y, g.reshape(1, -1), dout, name + "_bwd")
    return dy, dg.reshape(g.shape), dout


_norm_add.defvjp(_norm_add_fwd, _norm_add_bwd)


@functools.partial(jax.custom_vjp, nondiff_argnums=(2,))
def _norm(x, g, name):
    return _rms_fwd_call(x, g.reshape(1, -1), None, name + "_fwd")


def _norm_fwd(x, g, name):
    return _rms_fwd_call(x, g.reshape(1, -1), None, name + "_fwd"), (x, g)


def _norm_bwd(name, saved, dout):
    x, g = saved
    dx, dg = _rms_bwd_call(x, g.reshape(1, -1), dout, name + "_bwd")
    return dx, dg.reshape(g.shape)


_norm.defvjp(_norm_fwd, _norm_bwd)


def _shift_down(x, before1, before2, row):
    xm1 = jnp.where(row == 0, before1, pltpu.roll(x, 1, 0))
    xm2 = jnp.where(row == 0, before2, jnp.where(row == 1, before1, pltpu.roll(x, 2, 0)))
    return xm1, xm2


def _shift_up(x, after1, after2, row):
    n = x.shape[0]
    xp1 = jnp.where(row == n - 1, after1, pltpu.roll(x, n - 1, 0))
    xp2 = jnp.where(row == n - 1, after2, jnp.where(row == n - 2, after1, pltpu.roll(x, n - 2, 0)))
    return xp1, xp2


def _ffn_act_fwd_call(u, conv_w, conv_b, name):
    t, f2 = u.shape
    f = f2 // 2
    tr, tc = _tile(t, ROW_BLOCK), _tile(f, COL_BLOCK)
    nj, hb = f // tc, tr // HALO

    def body(ug_ref, uu_ref, hg_ref, hu_ref, wg_ref, wu_ref, bg_ref, bu_ref, cg_ref, cu_ref, a_ref):
        i = pl.program_id(0)
        row = lax.broadcasted_iota(jnp.int32, (tr, tc), 0)

        def conv(x_ref, h_ref, w_ref, b_ref):
            xv = x_ref[...]
            h1 = jnp.where(i > 0, h_ref[7:8, :], 0.0)
            h2 = jnp.where(i > 0, h_ref[6:7, :], 0.0)
            xm1, xm2 = _shift_down(xv, h1, h2, row)
            return w_ref[0:1, :] * xm2 + w_ref[1:2, :] * xm1 + w_ref[2:3, :] * xv + b_ref[...]

        cg = conv(ug_ref, hg_ref, wg_ref, bg_ref)
        cu = conv(uu_ref, hu_ref, wu_ref, bu_ref)
        cg_ref[...] = cg
        cu_ref[...] = cu
        a_ref[...] = (cg * _sigmoid(cg) * cu).astype(jnp.bfloat16)

    def blk(off):
        return pl.BlockSpec((tr, tc), lambda i, j: (i, j + off))

    def halo(off):
        return pl.BlockSpec((HALO, tc), lambda i, j: (jnp.maximum(i * hb - 1, 0), j + off))

    def wspec(rows, off):
        return pl.BlockSpec((rows, tc), lambda i, j: (0, j + off))

    out_blk = pl.BlockSpec((tr, tc), lambda i, j: (i, j))
    return pl.pallas_call(
        body, name=name, grid=(t // tr, nj),
        out_shape=[jax.ShapeDtypeStruct((t, f), jnp.float32), jax.ShapeDtypeStruct((t, f), jnp.float32),
                   jax.ShapeDtypeStruct((t, f), jnp.bfloat16)],
        in_specs=[blk(0), blk(nj), halo(0), halo(nj), wspec(3, 0), wspec(3, nj), wspec(1, 0), wspec(1, nj)],
        out_specs=[out_blk, out_blk, out_blk],
    )(u, u, u, u, conv_w, conv_w, conv_b, conv_b)


def _ffn_act_bwd_call(da, cg, cu, u, conv_w, name):
    t, f = da.shape
    tr, tc = _tile(t, ROW_BLOCK), _tile(f, COL_BLOCK)
    nj, hb, ni = f // tc, tr // HALO, t // tr

    def body(da_ref, cg_ref, cu_ref, nda_ref, ncg_ref, ncu_ref, u_ref, hu_ref, w_ref, du_ref, dw_ref, db_ref):
        j, i = pl.program_id(0), pl.program_id(1)
        is_gate = j < nj
        row = lax.broadcasted_iota(jnp.int32, (tr, tc), 0)

        def duc_of(dav, cgv, cuv):
            sg = _sigmoid(cgv)
            d_gate = dav * cuv * (sg * (1.0 + cgv * (1.0 - sg)))
            d_up = dav * (cgv * sg)
            return jnp.where(is_gate, d_gate, d_up)

        duc = duc_of(da_ref[...], cg_ref[...], cu_ref[...])
        n1 = jnp.where(i < ni - 1, duc_of(nda_ref[0:1, :], ncg_ref[0:1, :], ncu_ref[0:1, :]), 0.0)
        n2 = jnp.where(i < ni - 1, duc_of(nda_ref[1:2, :], ncg_ref[1:2, :], ncu_ref[1:2, :]), 0.0)
        dp1, dp2 = _shift_up(duc, n1, n2, row)
        du_ref[...] = (w_ref[2:3, :] * duc + w_ref[1:2, :] * dp1 + w_ref[0:1, :] * dp2).astype(jnp.bfloat16)

        uv = u_ref[...]
        h1 = jnp.where(i > 0, hu_ref[7:8, :], 0.0)
        h2 = jnp.where(i > 0, hu_ref[6:7, :], 0.0)
        um1, um2 = _shift_down(uv, h1, h2, row)

        @pl.when(i == 0)
        def _():
            dw_ref[...] = jnp.zeros_like(dw_ref)
            db_ref[...] = jnp.zeros_like(db_ref)

        rows8 = lax.broadcasted_iota(jnp.int32, (HALO, tc), 0)
        s0 = jnp.sum(duc * um2, axis=0, keepdims=True)
        s1 = jnp.sum(duc * um1, axis=0, keepdims=True)
        s2 = jnp.sum(duc * uv, axis=0, keepdims=True)
        dw_ref[...] += jnp.where(rows8 == 0, s0, jnp.where(rows8 == 1, s1, jnp.where(rows8 == 2, s2, 0.0)))
        db_ref[...] += jnp.sum(duc, axis=0, keepdims=True)

    half = pl.BlockSpec((tr, tc), lambda j, i: (i, j % nj))
    nhalf = pl.BlockSpec((HALO, tc), lambda j, i: (jnp.minimum((i + 1) * hb, t // HALO - 1), j % nj))
    full = pl.BlockSpec((tr, tc), lambda j, i: (i, j))
    hfull = pl.BlockSpec((HALO, tc), lambda j, i: (jnp.maximum(i * hb - 1, 0), j))
    return pl.pallas_call(
        body, name=name, grid=(2 * nj, ni),
        out_shape=[jax.ShapeDtypeStruct((t, 2 * f), jnp.bfloat16), jax.ShapeDtypeStruct((HALO, 2 * f), jnp.float32),
                   jax.ShapeDtypeStruct((1, 2 * f), jnp.float32)],
        in_specs=[half, half, half, nhalf, nhalf, nhalf, full, hfull, pl.BlockSpec((3, tc), lambda j, i: (0, j))],
        out_specs=[full, pl.BlockSpec((HALO, tc), lambda j, i: (0, j)), pl.BlockSpec((1, tc), lambda j, i: (0, j))],
        compiler_params=pltpu.CompilerParams(dimension_semantics=("parallel", "arbitrary")),
    )(da, cg, cu, da, cg, cu, u, u, conv_w)


@functools.partial(jax.custom_vjp, nondiff_argnums=(5,))
def _ffn_block(h, w_up, conv_w, conv_b, w_down, name):
    return _ffn_block_fwd(h, w_up, conv_w, conv_b, w_down, name)[0]


def _ffn_block_fwd(h, w_up, conv_w, conv_b, w_down, name):
    hb = h.astype(jnp.bfloat16)
    u = _matmul(hb, w_up, "nn", name + "_up")
    cg, cu, a = _ffn_act_fwd_call(u, conv_w, conv_b.reshape(1, -1), name + "_act")
    return _matmul(a, w_down, "nn", name + "_down"), (hb, u, cg, cu, a, w_up, conv_w, conv_b, w_down)


def _ffn_block_bwd(name, saved, dy):
    hb, u, cg, cu, a, w_up, conv_w, conv_b, w_down = saved
    dyb = dy.astype(jnp.bfloat16)
    da = _matmul(dyb, w_down, "nt", name + "_down_dx")
    dw_down = _matmul(a, dyb, "tn", name + "_down_dw", jnp.bfloat16)
    du, dcw, dcb = _ffn_act_bwd_call(da, cg, cu, u, conv_w, name + "_act_bwd")
    dh = _matmul(du, w_up, "nt", name + "_up_dx")
    dw_up = _matmul(hb, du, "tn", name + "_up_dw", jnp.bfloat16)
    return dh, dw_up, dcw[:3], dcb.reshape(conv_b.shape), dw_down


_ffn_block.defvjp(_ffn_block_fwd, _ffn_block_bwd)


def _sconv_fwd_call(z, conv_w, name):
    t, d3 = z.shape
    d = d3 // 3
    tr, tc = _tile(t, ROW_BLOCK), _tile(d, COL_BLOCK)
    nj, hb = d // tc, tr // HALO

    def body(b_ref, c_ref, x_ref, hc_ref, hx_ref, w_ref, a_ref):
        i = pl.program_id(0)
        row = lax.broadcasted_iota(jnp.int32, (tr, tc), 0)
        p = c_ref[...] * x_ref[...]
        h1 = jnp.where(i > 0, hc_ref[7:8, :] * hx_ref[7:8, :], 0.0)
        h2 = jnp.where(i > 0, hc_ref[6:7, :] * hx_ref[6:7, :], 0.0)
        pm1, pm2 = _shift_down(p, h1, h2, row)
        pc = w_ref[0:1, :] * pm2 + w_ref[1:2, :] * pm1 + w_ref[2:3, :] * p
        a_ref[...] = (b_ref[...] * pc).astype(jnp.bfloat16)

    def blk(off):
        return pl.BlockSpec((tr, tc), lambda i, j: (i, j + off))

    def halo(off):
        return pl.BlockSpec((HALO, tc), lambda i, j: (jnp.maximum(i * hb - 1, 0), j + off))

    return pl.pallas_call(
        body, name=name, grid=(t // tr, nj), out_shape=jax.ShapeDtypeStruct((t, d), jnp.bfloat16),
        in_specs=[blk(0), blk(nj), blk(2 * nj), halo(nj), halo(2 * nj), pl.BlockSpec((3, tc), lambda i, j: (0, j))],
        out_specs=pl.BlockSpec((tr, tc), lambda i, j: (i, j)),
    )(z, z, z, z, z, conv_w)


def _sconv_bwd_call(da, z, conv_w, name):
    t, d = da.shape
    tr, tc = _tile(t, ROW_BLOCK), _tile(d, COL_BLOCK)
    nj, hb, ni = d // tc, tr // HALO, t // tr

    def body(da_ref, b_ref, c_ref, x_ref, hc_ref, hx_ref, nda_ref, nb_ref, w_ref, db_ref, dc_ref, dx_ref, dw_ref):
        i = pl.program_id(1)
        row = lax.broadcasted_iota(jnp.int32, (tr, tc), 0)
        cv, xv, dav = c_ref[...], x_ref[...], da_ref[...]
        p = cv * xv
        h1 = jnp.where(i > 0, hc_ref[7:8, :] * hx_ref[7:8, :], 0.0)
        h2 = jnp.where(i > 0, hc_ref[6:7, :] * hx_ref[6:7, :], 0.0)
        pm1, pm2 = _shift_down(p, h1, h2, row)
        pc = w_ref[0:1, :] * pm2 + w_ref[1:2, :] * pm1 + w_ref[2:3, :] * p
        db_ref[...] = (dav * pc).astype(jnp.bfloat16)
        dpc = dav * b_ref[...]
        n1 = jnp.where(i < ni - 1, nda_ref[0:1, :] * nb_ref[0:1, :], 0.0)
        n2 = jnp.where(i < ni - 1, nda_ref[1:2, :] * nb_ref[1:2, :], 0.0)
        dp1, dp2 = _shift_up(dpc, n1, n2, row)
        dp = w_ref[2:3, :] * dpc + w_ref[1:2, :] * dp1 + w_ref[0:1, :] * dp2
        dc_ref[...] = (dp * xv).astype(jnp.bfloat16)
        dx_ref[...] = (dp * cv).astype(jnp.bfloat16)

        @pl.when(i == 0)
        def _():
            dw_ref[...] = jnp.zeros_like(dw_ref)

        rows8 = lax.broadcasted_iota(jnp.int32, (HALO, tc), 0)
        s0 = jnp.sum(dpc * pm2, axis=0, keepdims=True)
        s1 = jnp.sum(dpc * pm1, axis=0, keepdims=True)
        s2 = jnp.sum(dpc * p, axis=0, keepdims=True)
        dw_ref[...] += jnp.where(rows8 == 0, s0, jnp.where(rows8 == 1, s1, jnp.where(rows8 == 2, s2, 0.0)))

    def blk(off):
        return pl.BlockSpec((tr, tc), lambda j, i: (i, j + off))

    def halo(off):
        return pl.BlockSpec((HALO, tc), lambda j, i: (jnp.maximum(i * hb - 1, 0), j + off))

    def nhalo(off):
        return pl.BlockSpec((HALO, tc), lambda j, i: (jnp.minimum((i + 1) * hb, t // HALO - 1), j + off))

    out = pl.BlockSpec((tr, tc), lambda j, i: (i, j))
    return pl.pallas_call(
        body, name=name, grid=(nj, ni),
        out_shape=[jax.ShapeDtypeStruct((t, d), jnp.bfloat16)] * 3 + [jax.ShapeDtypeStruct((HALO, d), jnp.float32)],
        in_specs=[blk(0), blk(0), blk(nj), blk(2 * nj), halo(nj), halo(2 * nj), nhalo(0), nhalo(0),
                  pl.BlockSpec((3, tc), lambda j, i: (0, j))],
        out_specs=[out, out, out, pl.BlockSpec((HALO, tc), lambda j, i: (0, j))],
        compiler_params=pltpu.CompilerParams(dimension_semantics=("parallel", "arbitrary")),
    )(da, z, z, z, z, z, da, z, conv_w)


@functools.partial(jax.custom_vjp, nondiff_argnums=(4,))
def _sconv_block(h, w_in, conv_w, w_out, name):
    return _sconv_block_fwd(h, w_in, conv_w, w_out, name)[0]


def _sconv_block_fwd(h, w_in, conv_w, w_out, name):
    hb = h.astype(jnp.bfloat16)
    z = _matmul(hb, w_in, "nn", name + "_in")
    a = _sconv_fwd_call(z, conv_w, name + "_gate")
    return _matmul(a, w_out, "nn", name + "_out"), (hb, z, a, w_in, conv_w, w_out)


def _sconv_block_bwd(name, saved, dy):
    hb, z, a, w_in, conv_w, w_out = saved
    dyb = dy.astype(jnp.bfloat16)
    da = _matmul(dyb, w_out, "nt", name + "_out_dx")
    dw_out = _matmul(a, dyb, "tn", name + "_out_dw", jnp.bfloat16)
    db, dc, dx, dcw = _sconv_bwd_call(da, z, conv_w, name + "_gate_bwd")
    dz = jnp.concatenate([db, dc, dx], axis=1)
    dh = _matmul(dz, w_in, "nt", name + "_in_dx")
    dw_in = _matmul(hb, dz, "tn", name + "_in_dw", jnp.bfloat16)
    return dh, dw_in, dcw[:3], dw_out


_sconv_block.defvjp(_sconv_block_fwd, _sconv_block_bwd)


def _to_heads(x, heads):
    t = x.shape[0]
    return x.reshape(t, heads, -1).transpose(1, 0, 2)


def _from_heads(x):
    h, t, d = x.shape
    return x.transpose(1, 0, 2).reshape(t, h * d)


def _partial_rope(x, positions):
    half = ROT_DIM // 2
    inv_freq = ROPE_THETA ** (-jnp.arange(half, dtype=jnp.float32) / half)
    ang = positions.astype(jnp.float32)[:, None] * inv_freq[None, :]
    cos = jnp.cos(ang)[:, None, :]
    sin = jnp.sin(ang)[:, None, :]
    x1, x2 = x[..., :half], x[..., half:ROT_DIM]
    return jnp.concatenate([x1 * cos - x2 * sin, x2 * cos + x1 * sin, x[..., ROT_DIM:]], axis=-1)


def _swa_probs(q_ref, kp_ref, kc_ref, sink_ref, n, w, scale):
    grp, d = q_ref.shape[0], q_ref.shape[2]
    q2 = q_ref[...].reshape(grp * w, d)
    kcat = jnp.concatenate([kp_ref[0], kc_ref[0]], axis=0)
    s3 = (_dot(q2, kcat, _NT) * scale).reshape(grp, w, 2 * w)
    qi = lax.broadcasted_iota(jnp.int32, s3.shape, 1)
    kj = lax.broadcasted_iota(jnp.int32, s3.shape, 2)
    diff = qi + w - kj
    allowed = (diff >= 0) & (diff < w) & ((kj >= w) | (n > 0))
    s3 = jnp.where(allowed, s3, NEG)
    sink = sink_ref[...]
    m = jnp.maximum(jnp.max(s3, axis=2, keepdims=True), sink)
    e = jnp.exp(s3 - m)
    es = jnp.exp(sink - m)
    z = jnp.sum(e, axis=2, keepdims=True) + es
    return q2, kcat, e / z, es / z


def _swa_fwd_call(q, k, v, sinks, name):
    qh, t, d = q.shape
    kv = k.shape[0]
    grp, w = qh // kv, SWA_WINDOW
    nblk = t // w
    scale = d ** -0.5

    def body(q_ref, kp_ref, kc_ref, vp_ref, vc_ref, sink_ref, o_ref):
        n = pl.program_id(1)
        _, _, p3, _ = _swa_probs(q_ref, kp_ref, kc_ref, sink_ref, n, w, scale)
        vcat = jnp.concatenate([vp_ref[0], vc_ref[0]], axis=0)
        o = _dot(p3.reshape(grp * w, 2 * w), vcat, _NN)
        o_ref[...] = o.reshape(grp, w, d).astype(o_ref.dtype)

    qspec = pl.BlockSpec((grp, w, d), lambda h, n: (h, n, 0))
    prev = pl.BlockSpec((1, w, d), lambda h, n: (h, jnp.maximum(n - 1, 0), 0))
    cur = pl.BlockSpec((1, w, d), lambda h, n: (h, n, 0))
    return pl.pallas_call(
        body, name=name, grid=(kv, nblk), out_shape=jax.ShapeDtypeStruct((qh, t, d), jnp.bfloat16),
        in_specs=[qspec, prev, cur, prev, cur, pl.BlockSpec((grp, 1, 1), lambda h, n: (h, 0, 0))],
        out_specs=qspec,
    )(q, k, k, v, v, sinks)


def _swa_bwd_call(q, k, v, sinks, do, name):
    qh, t, d = q.shape
    kv = k.shape[0]
    grp, w = qh // kv, SWA_WINDOW
    nblk = t // w
    scale = d ** -0.5

    def body(q_ref, kp_ref, kc_ref, vp_ref, vc_ref, sink_ref, do_ref, dq_ref, dk_ref, dv_ref, ds_ref,
             carry_k, carry_v, part_k, part_v):
        n = pl.program_id(1)

        @pl.when(n == 0)
        def _():
            carry_k[...] = jnp.zeros_like(carry_k)
            carry_v[...] = jnp.zeros_like(carry_v)
            ds_ref[...] = jnp.zeros_like(ds_ref)

        @pl.when(n < nblk)
        def _():
            q2, kcat, p3, ps = _swa_probs(q_ref, kp_ref, kc_ref, sink_ref, n, w, scale)
            vcat = jnp.concatenate([vp_ref[0], vc_ref[0]], axis=0)
            do2 = do_ref[...].reshape(grp * w, d)
            dp3 = _dot(do2, vcat, _NT).reshape(grp, w, 2 * w)
            rs = jnp.sum(p3 * dp3, axis=2, keepdims=True)
            ds2 = (p3 * (dp3 - rs)).reshape(grp * w, 2 * w)
            dsink = -jnp.sum(ps * rs, axis=1, keepdims=True)
            ds_ref[...] += jnp.broadcast_to(dsink, ds_ref.shape)
            dq_ref[...] = (_dot(ds2, kcat, _NN) * scale).reshape(grp, w, d)
            part_k[...] = _dot(ds2, q2, _TN) * scale
            part_v[...] = _dot(p3.reshape(grp * w, 2 * w), do2, _TN)

        @pl.when(n == nblk)
        def _():
            part_k[...] = jnp.zeros_like(part_k)
            part_v[...] = jnp.zeros_like(part_v)

        dk_ref[0] = carry_k[...] + part_k[0:w, :]
        dv_ref[0] = carry_v[...] + part_v[0:w, :]
        carry_k[...] = part_k[w:2 * w, :]
        carry_v[...] = part_v[w:2 * w, :]

    last = nblk - 1
    qspec = pl.BlockSpec((grp, w, d), lambda h, n: (h, jnp.minimum(n, last), 0))
    prev = pl.BlockSpec((1, w, d), lambda h, n: (h, jnp.maximum(jnp.minimum(n, last) - 1, 0), 0))
    cur = pl.BlockSpec((1, w, d), lambda h, n: (h, jnp.minimum(n, last), 0))
    kout = pl.BlockSpec((1, w, d), lambda h, n: (h, jnp.maximum(n - 1, 0), 0))
    return pl.pallas_call(
        body, name=name, grid=(kv, nblk + 1),
        out_shape=[jax.ShapeDtypeStruct((qh, t, d), jnp.float32), jax.ShapeDtypeStruct((kv, t, d), jnp.float32),
                   jax.ShapeDtypeStruct((kv, t, d), jnp.float32), jax.ShapeDtypeStruct((qh, 1, 128), jnp.float32)],
        in_specs=[qspec, prev, cur, prev, cur, pl.BlockSpec((grp, 1, 1), lambda h, n: (h, 0, 0)), qspec],
        out_specs=[qspec, kout, kout, pl.BlockSpec((grp, 1, 128), lambda h, n: (h, 0, 0))],
        scratch_shapes=[pltpu.VMEM((w, d), jnp.float32), pltpu.VMEM((w, d), jnp.float32),
                        pltpu.VMEM((2 * w, d), jnp.float32), pltpu.VMEM((2 * w, d), jnp.float32)],
        compiler_params=pltpu.CompilerParams(dimension_semantics=("parallel", "arbitrary")),
    )(q, k, k, v, v, sinks, do)


def _swa_split(z, positions, qh, kv, d):
    q, k, v = jnp.split(z, [qh * d, qh * d + kv * d], axis=-1)
    t = z.shape[0]
    q = _partial_rope(q.reshape(t, qh, d), positions).transpose(1, 0, 2)
    k = _partial_rope(k.reshape(t, kv, d), positions).transpose(1, 0, 2)
    v = v.reshape(t, kv, d).transpose(1, 0, 2)
    return q, k, v


@functools.partial(jax.custom_vjp, nondiff_argnums=(5,))
def _swa_block(h, positions, w_in, w_out, sinks, name):
    return _swa_block_fwd(h, positions, w_in, w_out, sinks, name)[0]


def _swa_block_fwd(h, positions, w_in, w_out, sinks, name):
    d = SWA_HEAD_DIM
    qh = h.shape[1] // d
    kv = qh // SWA_GROUP
    hb = h.astype(jnp.bfloat16)
    z = _matmul(hb, w_in, "nt", name + "_in")
    q, k, v = [a.astype(jnp.bfloat16) for a in _swa_split(z, positions, qh, kv, d)]
    o = _swa_fwd_call(q, k, v, sinks.reshape(qh, 1, 1), name + "_attn")
    a = _from_heads(o)
    return _matmul(a, w_out, "nn", name + "_out"), (hb, positions, z, q, k, v, a, w_in, w_out, sinks)


def _swa_block_bwd(name, saved, dy):
    hb, positions, z, q, k, v, a, w_in, w_out, sinks = saved
    d = SWA_HEAD_DIM
    qh = q.shape[0]
    kv = k.shape[0]
    dyb = dy.astype(jnp.bfloat16)
    da = _matmul(dyb, w_out, "nt", name + "_out_dx")
    dw_out = _matmul(a, dyb, "tn", name + "_out_dw", jnp.bfloat16)
    do = _to_heads(da, qh)
    dq, dk, dv, dsinks = _swa_bwd_call(q, k, v, sinks.reshape(qh, 1, 1), do, name + "_attn_bwd")
    _, split_vjp = jax.vjp(lambda zz: _swa_split(zz, positions, qh, kv, d), z)
    dz = split_vjp((dq, dk, dv))[0].astype(jnp.bfloat16)
    dh = _matmul(dz, w_in, "nn", name + "_in_dx")
    dw_in = _matmul(dz, hb, "tn", name + "_in_dw", jnp.bfloat16)
    return dh, None, dw_in, dw_out, dsinks[:, 0, 0].reshape(sinks.shape)


_swa_block.defvjp(_swa_block_fwd, _swa_block_bwd)


def _fox_specs(hp, nb, tb, d):
    blk = pl.BlockSpec((hp, 1, tb, d), lambda h, i: (h, i, 0, 0))
    whole = pl.BlockSpec((hp, nb, tb, d), lambda h, i: (h, 0, 0, 0))
    rowblk = pl.BlockSpec((hp, 1, 1, tb), lambda h, i: (h, i, 0, 0))
    rowwhole = pl.BlockSpec((hp, nb, 1, tb), lambda h, i: (h, 0, 0, 0))
    return blk, whole, rowblk, rowwhole


def _fox_fwd_call(q, k, v, g, bias, name):
    hh, nb, tb, d = q.shape
    scale = d ** -0.5

    hp = FOX_HEADS_PER_STEP

    def body(q_ref, k_ref, v_ref, g_ref, b_ref, og_ref, o_ref, lse_ref):
        i = pl.program_id(1)
        qs = [q_ref[a, 0] for a in range(hp)]
        rows = lax.broadcasted_iota(jnp.int32, (tb, tb), 0)
        cols = lax.broadcasted_iota(jnp.int32, (tb, tb), 1)

        def step(kb, carry, masked):
            out = []
            for a in range(hp):
                m, l, acc = carry[a]
                s = _dot(qs[a], k_ref[a, kb], _NT) * scale + b_ref[a, kb]
                if masked:
                    s = jnp.where(cols <= rows, s, NEG)
                m_new = jnp.maximum(m, jnp.max(s, axis=1, keepdims=True))
                alpha = jnp.exp(m - m_new)
                p = jnp.exp(s - m_new)
                out.append((m_new, alpha * l + jnp.sum(p, axis=1, keepdims=True),
                            alpha * acc + _dot(p, v_ref[a, kb], _NN)))
            return tuple(out)

        init = (jnp.full((tb, 1), NEG, jnp.float32), jnp.zeros((tb, 1), jnp.float32), jnp.zeros((tb, d), jnp.float32))
        carry = lax.fori_loop(0, i, lambda kb, c: step(kb, c, False), (init,) * hp)
        for a, (m, l, acc) in enumerate(step(i, carry, True)):
            o = acc / l
            o_ref[a, 0] = o
            og_ref[a, 0] = (o * _sigmoid(g_ref[a, 0])).astype(og_ref.dtype)
            lse_ref[a, 0] = _col_to_row(m + jnp.log(l))

    blk, whole, rowblk, rowwhole = _fox_specs(hp, nb, tb, d)
    return pl.pallas_call(
        body, name=name, grid=(hh // hp, nb),
        out_shape=[jax.ShapeDtypeStruct(q.shape, jnp.bfloat16), jax.ShapeDtypeStruct(q.shape, jnp.float32),
                   jax.ShapeDtypeStruct((hh, nb, 1, tb), jnp.float32)],
        in_specs=[blk, whole, whole, blk, rowwhole],
        out_specs=[blk, blk, rowblk],
    )(q, k, v, g, bias)


def _fox_dq_call(q, k, v, g, o, dog, bias, lse, name):
    hh, nb, tb, d = q.shape
    scale = d ** -0.5

    hp = FOX_HEADS_PER_STEP

    def body(q_ref, k_ref, v_ref, g_ref, o_ref, dog_ref, b_ref, lse_ref, dq_ref, dg_ref, do_ref, delta_ref):
        i = pl.program_id(1)
        rows = lax.broadcasted_iota(jnp.int32, (tb, tb), 0)
        cols = lax.broadcasted_iota(jnp.int32, (tb, tb), 1)
        qs, dos, deltas, lses = [], [], [], []
        for a in range(hp):
            sig = _sigmoid(g_ref[a, 0])
            ov, dogv = o_ref[a, 0], dog_ref[a, 0]
            do = dogv * sig
            dg_ref[a, 0] = dogv * ov * sig * (1.0 - sig)
            do_ref[a, 0] = do.astype(do_ref.dtype)
            qs.append(q_ref[a, 0])
            dos.append(do.astype(jnp.bfloat16))
            deltas.append(jnp.sum(do * ov, axis=1, keepdims=True))
            lses.append(_row_to_col(lse_ref[a, 0]))

        def step(kb, carry, masked):
            out = []
            for a in range(hp):
                dq, spdp, sp = carry[a]
                s = _dot(qs[a], k_ref[a, kb], _NT) * scale + b_ref[a, kb]
                p = jnp.exp(s - lses[a])
                if masked:
                    p = jnp.where(cols <= rows, p, 0.0)
                dp = _dot(dos[a], v_ref[a, kb], _NT)
                out.append((dq + _dot(p * (dp - deltas[a]), k_ref[a, kb], _NN),
                            spdp + jnp.sum(p * dp, axis=1, keepdims=True), sp + jnp.sum(p, axis=1, keepdims=True)))
            return tuple(out)

        zcol = jnp.zeros((tb, 1), jnp.float32)
        init = (jnp.zeros((tb, d), jnp.float32), zcol, zcol)
        carry = lax.fori_loop(0, i, lambda kb, c: step(kb, c, False), (init,) * hp)
        for a, (dq, spdp, sp) in enumerate(step(i, carry, True)):
            dq_ref[a, 0] = dq * scale
            delta_ref[a, 0] = _col_to_row(spdp / sp)

    blk, whole, rowblk, rowwhole = _fox_specs(hp, nb, tb, d)
    return pl.pallas_call(
        body, name=name, grid=(hh // hp, nb),
        out_shape=[jax.ShapeDtypeStruct(q.shape, jnp.float32), jax.ShapeDtypeStruct(q.shape, jnp.float32),
                   jax.ShapeDtypeStruct(q.shape, jnp.bfloat16), jax.ShapeDtypeStruct((hh, nb, 1, tb), jnp.float32)],
        in_specs=[blk, whole, whole, blk, blk, blk, rowwhole, rowblk],
        out_specs=[blk, blk, blk, rowblk],
    )(q, k, v, g, o, dog, bias, lse)


def _fox_dkv_call(q, k, v, do, bias, lse, delta, name):
    hh, nb, tb, d = q.shape
    scale = d ** -0.5

    hp = FOX_HEADS_PER_STEP

    def body(q_ref, k_ref, v_ref, do_ref, b_ref, lse_ref, delta_ref, dk_ref, dv_ref, db_ref):
        j = pl.program_id(1)
        ks = [k_ref[a, 0] for a in range(hp)]
        vs = [v_ref[a, 0] for a in range(hp)]
        bcols = [_row_to_col(b_ref[a, 0]) for a in range(hp)]
        rows = lax.broadcasted_iota(jnp.int32, (tb, tb), 0)
        cols = lax.broadcasted_iota(jnp.int32, (tb, tb), 1)

        def step(qb, carry, masked):
            out = []
            for a in range(hp):
                dk, dv, db = carry[a]
                st = _dot(ks[a], q_ref[a, qb], _NT) * scale + bcols[a] - lse_ref[a, qb]
                pt = jnp.exp(st)
                if masked:
                    pt = jnp.where(rows <= cols, pt, 0.0)
                dpt = _dot(vs[a], do_ref[a, qb], _NT)
                dst = pt * (dpt - delta_ref[a, qb])
                out.append((dk + _dot(dst, q_ref[a, qb], _NN), dv + _dot(pt, do_ref[a, qb], _NN),
                            db + jnp.sum(dst, axis=1, keepdims=True)))
            return tuple(out)

        init = (jnp.zeros((tb, d), jnp.float32), jnp.zeros((tb, d), jnp.float32), jnp.zeros((tb, 1), jnp.float32))
        carry = step(j, (init,) * hp, True)
        for a, (dk, dv, db) in enumerate(lax.fori_loop(j + 1, nb, lambda qb, c: step(qb, c, False), carry)):
            dk_ref[a, 0] = dk * scale
            dv_ref[a, 0] = dv
            db_ref[a, 0] = _col_to_row(db)

    blk, whole, rowblk, rowwhole = _fox_specs(hp, nb, tb, d)
    return pl.pallas_call(
        body, name=name, grid=(hh // hp, nb),
        out_shape=[jax.ShapeDtypeStruct(q.shape, jnp.float32), jax.ShapeDtypeStruct(q.shape, jnp.float32),
                   jax.ShapeDtypeStruct((hh, nb, 1, tb), jnp.float32)],
        in_specs=[whole, blk, blk, whole, rowblk, rowwhole, rowwhole],
        out_specs=[blk, blk, rowblk],
    )(q, k, v, do, bias, lse, delta)


def _fox_bias_fwd_call(fl, b_f, name):
    t, lanes = fl.shape
    tr = _tile(t, ROW_BLOCK)

    def body(fl_ref, b_ref, o_ref, carry):
        i = pl.program_id(0)

        @pl.when(i == 0)
        def _():
            carry[...] = jnp.zeros_like(carry)

        xv = fl_ref[...] + b_ref[...]
        ls = jnp.minimum(xv, 0.0) - jnp.log(1.0 + jnp.exp(-jnp.abs(xv)))
        pos = lax.broadcasted_iota(jnp.int32, (tr, lanes), 0)
        o_ref[...] = -(_group_cumsum(ls, tr, pos) + carry[...])
        carry[...] += jnp.sum(ls, axis=0, keepdims=True)

    blk = pl.BlockSpec((tr, lanes), lambda i: (i, 0))
    return pl.pallas_call(
        body, name=name, grid=(t // tr,), out_shape=jax.ShapeDtypeStruct((t, lanes), jnp.float32),
        in_specs=[blk, pl.BlockSpec((1, lanes), lambda i: (0, 0))], out_specs=blk,
        scratch_shapes=[pltpu.VMEM((1, lanes), jnp.float32)],
        compiler_params=pltpu.CompilerParams(dimension_semantics=("arbitrary",)),
    )(fl, b_f)


def _fox_bias_bwd_call(dbias, fl, b_f, name):
    t, lanes = fl.shape
    tr = _tile(t, ROW_BLOCK)
    last = t // tr - 1

    def body(db_ref, fl_ref, b_ref, dfl_ref, dbf_ref, carry):
        i = pl.program_id(0)

        @pl.when(i == 0)
        def _():
            carry[...] = jnp.zeros_like(carry)
            dbf_ref[...] = jnp.zeros_like(dbf_ref)

        dbv = db_ref[...]
        pos = lax.broadcasted_iota(jnp.int32, (tr, lanes), 0)
        dls = -(_group_rcumsum(dbv, tr, pos) + carry[...])
        carry[...] += jnp.sum(dbv, axis=0, keepdims=True)
        dfl = dls * _sigmoid(-(fl_ref[...] + b_ref[...]))
        dfl_ref[...] = dfl
        dbf_ref[...] += jnp.sum(dfl, axis=0, keepdims=True)

    blk = pl.BlockSpec((tr, lanes), lambda i: (last - i, 0))
    one = pl.BlockSpec((1, lanes), lambda i: (0, 0))
    return pl.pallas_call(
        body, name=name, grid=(t // tr,),
        out_shape=[jax.ShapeDtypeStruct((t, lanes), jnp.float32), jax.ShapeDtypeStruct((1, lanes), jnp.float32)],
        in_specs=[blk, blk, one], out_specs=[blk, one],
        scratch_shapes=[pltpu.VMEM((1, lanes), jnp.float32)],
        compiler_params=pltpu.CompilerParams(dimension_semantics=("arbitrary",)),
    )(dbias, fl, b_f)


@functools.partial(jax.custom_vjp, nondiff_argnums=(5,))
def _fox_block(h, w_qkvg, w_f, b_f, w_out, name):
    return _fox_block_fwd(h, w_qkvg, w_f, b_f, w_out, name)[0]


def _fox_block_fwd(h, w_qkvg, w_f, b_f, w_out, name):
    t, dm = h.shape
    d = FOX_HEAD_DIM
    heads = dm // d
    lanes = w_f.shape[0]
    tb = _tile(t, ATT_BLOCK)
    nb = t // tb
    hb = h.astype(jnp.bfloat16)
    z = _matmul(hb, w_qkvg, "nt", name + "_in")
    fl = _matmul(hb, w_f, "nt", name + "_inf")
    z4 = z.reshape(t, 4, heads, d).transpose(1, 2, 0, 3).reshape(4, heads, nb, tb, d)
    q, k, v = [z4[i].astype(jnp.bfloat16) for i in range(3)]
    g = z4[3]
    b_fp = jnp.pad(b_f, (0, lanes - heads)).reshape(1, lanes)
    bias = _fox_bias_fwd_call(fl, b_fp, name + "_bias")[:, :heads].T.reshape(heads, nb, 1, tb)
    og, o, lse = _fox_fwd_call(q, k, v, g, bias, name + "_attn")
    a = _from_heads(og.reshape(heads, t, d))
    return _matmul(a, w_out, "nn", name + "_out"), (hb, q, k, v, g, bias, o, lse, a, fl, b_fp, w_qkvg, w_f, w_out)


def _fox_block_bwd(name, saved, dy):
    hb, q, k, v, g, bias, o, lse, a, fl, b_fp, w_qkvg, w_f, w_out = saved
    heads, nb, tb, d = q.shape
    t = nb * tb
    lanes = w_f.shape[0]
    dyb = dy.astype(jnp.bfloat16)
    da = _matmul(dyb, w_out, "nt", name + "_out_dx")
    dw_out = _matmul(a, dyb, "tn", name + "_out_dw", jnp.bfloat16)
    dog = _to_heads(da, heads).reshape(heads, nb, tb, d)
    dq, dg, do, delta = _fox_dq_call(q, k, v, g, o, dog, bias, lse, name + "_attn_dq")
    dk, dv, dbias = _fox_dkv_call(q, k, v, do, bias, lse, delta, name + "_attn_dkv")
    dz = jnp.stack([dq, dk, dv, dg]).reshape(4, heads, t, d).transpose(2, 0, 1, 3).reshape(t, 4 * heads * d)
    dz = dz.astype(jnp.bfloat16)
    dbias_rows = jnp.pad(dbias.reshape(heads, t).T, ((0, 0), (0, lanes - heads)))
    dfl, db_fp = _fox_bias_bwd_call(dbias_rows, fl, b_fp, name + "_bias_bwd")
    dflb = dfl.astype(jnp.bfloat16)
    dh = _matmul(dz, w_qkvg, "nn", name + "_in_dx") + _matmul(dflb, w_f, "nn", name + "_inf_dx")
    dw_qkvg = _matmul(dz, hb, "tn", name + "_in_dw", jnp.bfloat16)
    dw_f = _matmul(dflb, hb, "tn", name + "_inf_dw", jnp.bfloat16)
    return dh, dw_qkvg, dw_f, db_fp[0, :heads], dw_out


_fox_block.defvjp(_fox_block_fwd, _fox_block_bwd)


def _group_cumsum(x, c, pos):
    sh = 1
    while sh < c:
        x = x + jnp.where(pos >= sh, pltpu.roll(x, sh, 0), 0.0)
        sh *= 2
    return x


def _group_rcumsum(x, c, pos):
    n = x.shape[0]
    sh = 1
    while sh < c:
        x = x + jnp.where(pos < c - sh, pltpu.roll(x, n - sh, 0), 0.0)
        sh *= 2
    return x


def _hgrn_prep(zq_ref, zf_ref, zi_ref, lb_ref, b_scr, rows, c):
    nb = rows // c
    zq, zf = zq_ref[...], zf_ref[...]
    lbv = lb_ref[...]
    sq = _sigmoid(zq)
    sf = _sigmoid(zf)
    f = lbv + (1.0 - lbv) * sf
    pos = lax.broadcasted_iota(jnp.int32, (rows, 128), 0) % c
    b = _group_cumsum(jnp.log(f), c, pos)
    sh = (nb, c, 128)
    q3, k3, v3, b3 = (zq * sq).reshape(sh), (1.0 - f).reshape(sh), zi_ref[...].reshape(sh), b.reshape(sh)
    b_scr[...] = b3
    glast = b_scr[:, c - 1:c, :]
    eb = jnp.exp(b3)
    ek = jnp.exp(glast - b3)
    return dict(zq=zq, sq=sq, sf=sf, f=f, pos=pos, q3=q3, k3=k3, v3=v3, b3=b3, eb=eb, ek=ek,
                qt=q3 * eb, kh=k3 * ek, dec=jnp.exp(glast))


def _hgrn_fwd_call(z, lb, norm_g, name):
    t, w4 = z.shape
    w = w4 // 4
    heads = w // HGRN_EXPAND
    rows, c = _tile(t, HGRN_ROWS), HGRN_SUB
    nb, ntb = rows // c, t // rows

    def body(zq_ref, zf_ref, zi_ref, zg_ref, lb_ref, ng_ref, og_ref, o_ref, ck_ref, st_ref, b_scr, k_scr, v_scr):
        tb = pl.program_id(1)

        @pl.when(tb == 0)
        def _():
            st_ref[...] = jnp.zeros_like(st_ref)

        ck_ref[0, 0] = st_ref[...]
        p = _hgrn_prep(zq_ref, zf_ref, zi_ref, lb_ref, b_scr, rows, c)
        k_scr[...] = p["k3"]
        v_scr[...] = p["v3"]
        tpos = lax.broadcasted_iota(jnp.int32, (nb, c, 128), 1)
        od = jnp.zeros((nb, c, 128), jnp.float32)
        for s in range(c):
            bs, ks, vs = b_scr[:, s:s + 1, :], k_scr[:, s:s + 1, :], v_scr[:, s:s + 1, :]
            e = jnp.exp(jnp.minimum(p["b3"] - bs, 0.0))
            a = jnp.sum(jnp.where(tpos >= s, p["q3"] * ks * e, 0.0), axis=2, keepdims=True)
            od = od + a * vs
        st = st_ref[...]
        for i in range(nb):
            o_ref[pl.ds(i * c, c), :] = _dot(p["qt"][i], st, _NT) + od[i]
            st = st * p["dec"][i] + _dot(p["v3"][i], p["kh"][i], _TN)
        st_ref[...] = st
        o = o_ref[...]
        rstd = lax.rsqrt(jnp.mean(o * o, axis=1, keepdims=True) + RMS_EPS)
        zg = zg_ref[...]
        og_ref[...] = (o * rstd * ng_ref[...] * (zg * _sigmoid(zg))).astype(og_ref.dtype)

    def col(off):
        return pl.BlockSpec((rows, 128), lambda h, tb: (tb, h + off))

    out = pl.BlockSpec((rows, 128), lambda h, tb: (tb, h))
    return pl.pallas_call(
        body, name=name, grid=(heads, ntb),
        out_shape=[jax.ShapeDtypeStruct((t, w), jnp.bfloat16), jax.ShapeDtypeStruct((t, w), jnp.float32),
                   jax.ShapeDtypeStruct((heads, ntb, 128, 128), jnp.float32)],
        in_specs=[col(0), col(heads), col(2 * heads), col(3 * heads), pl.BlockSpec((1, 128), lambda h, tb: (0, h)),
                  pl.BlockSpec((1, 128), lambda h, tb: (0, 0))],
        out_specs=[out, out, pl.BlockSpec((1, 1, 128, 128), lambda h, tb: (h, tb, 0, 0))],
        scratch_shapes=[pltpu.VMEM((128, 128), jnp.float32)] + [pltpu.VMEM((nb, c, 128), jnp.float32)] * 3,
        compiler_params=pltpu.CompilerParams(dimension_semantics=("parallel", "arbitrary")),
    )(z, z, z, z, lb, norm_g)


def _hgrn_bwd_call(z, lb, norm_g, o_raw, dog, ck, name):
    t, w4 = z.shape
    w = w4 // 4
    heads = w // HGRN_EXPAND
    rows, c = _tile(t, HGRN_ROWS), HGRN_SUB
    nb, ntb = rows // c, t // rows

    def body(zq_ref, zf_ref, zi_ref, zg_ref, lb_ref, ng_ref, o_ref, dog_ref, ck_ref,
             dzq_ref, dzf_ref, dzi_ref, dzg_ref, dlb_ref, dng_ref,
             dst_ref, s_store, b_scr, k_scr, v_scr, dqt_scr, dkh_scr, dv_scr, ddec_scr):
        tb = pl.program_id(1)

        @pl.when(tb == 0)
        def _():
            dst_ref[...] = jnp.zeros_like(dst_ref)
            dlb_ref[...] = jnp.zeros_like(dlb_ref)
            dng_ref[...] = jnp.zeros_like(dng_ref)

        p = _hgrn_prep(zq_ref, zf_ref, zi_ref, lb_ref, b_scr, rows, c)
        q3, k3, v3, b3 = p["q3"], p["k3"], p["v3"], p["b3"]

        o = o_ref[...]
        rstd = lax.rsqrt(jnp.mean(o * o, axis=1, keepdims=True) + RMS_EPS)
        xh = o * rstd
        zg = zg_ref[...]
        sg = _sigmoid(zg)
        ng = ng_ref[...]
        dogv = dog_ref[...]
        don = dogv * (zg * sg)
        dzg_ref[...] = (dogv * (xh * ng) * (sg * (1.0 + zg * (1.0 - sg)))).astype(dzg_ref.dtype)
        dng_ref[0] += jnp.sum(don * xh, axis=0, keepdims=True)
        dyg = don * ng
        do = rstd * (dyg - xh * jnp.mean(dyg * xh, axis=1, keepdims=True))
        do3 = do.reshape(nb, c, 128)

        st = ck_ref[0, 0]
        for i in range(nb):
            s_store[i] = st
            st = st * p["dec"][i] + _dot(v3[i], p["kh"][i], _TN)

        dst = dst_ref[...]
        for i in reversed(range(nb)):
            sl = pl.ds(i * c, c)
            st_i = s_store[i]
            dqt_scr[sl, :] = _dot(do3[i], st_i, _NN)
            dv_scr[sl, :] = _dot(p["kh"][i], dst, _NT)
            dkh_scr[sl, :] = _dot(v3[i], dst, _NN)
            ddec_scr[i] = jnp.broadcast_to(jnp.sum(st_i * dst, axis=0, keepdims=True), (8, 128))
            dst = dst * p["dec"][i] + _dot(do3[i], p["qt"][i], _TN)
        dst_ref[...] = dst

        k_scr[...] = k3
        v_scr[...] = v3
        tpos = lax.broadcasted_iota(jnp.int32, (nb, c, 128), 1)
        zero = jnp.zeros((nb, c, 128), jnp.float32)
        dqd, dkd, dvd, dbd = zero, zero, zero, zero
        for s in range(c):
            bs, ks, vs = b_scr[:, s:s + 1, :], k_scr[:, s:s + 1, :], v_scr[:, s:s + 1, :]
            e = jnp.where(tpos >= s, jnp.exp(jnp.minimum(b3 - bs, 0.0)), 0.0)
            qe = q3 * e
            a = jnp.sum(qe * ks, axis=2, keepdims=True)
            da = jnp.sum(do3 * vs, axis=2, keepdims=True)
            y = da * (ks * e)
            dqd = dqd + y
            dbd = dbd + y * q3
            dks = jnp.sum(da * qe, axis=1, keepdims=True)
            dvs = jnp.sum(a * do3, axis=1, keepdims=True)
            at_s = tpos == s
            dkd = dkd + jnp.where(at_s, dks, 0.0)
            dvd = dvd + jnp.where(at_s, dvs, 0.0)
            dbd = dbd - jnp.where(at_s, dks * ks, 0.0)

        dqt3 = dqt_scr[...].reshape(nb, c, 128)
        dkh3 = dkh_scr[...].reshape(nb, c, 128)
        ddec = ddec_scr[:, 0:1, :]
        dq_act = dqt3 * p["eb"] + dqd
        dk = dkh3 * p["ek"] + dkd
        khd = p["kh"] * dkh3
        db = p["qt"] * dqt3 - khd + dbd
        dglast = jnp.sum(khd, axis=1, keepdims=True) + ddec * p["dec"]
        db = db + jnp.where(tpos == c - 1, dglast, 0.0)
        dlogf = _group_rcumsum(db.reshape(rows, 128), c, p["pos"])

        zq, sq, sf, f = p["zq"], p["sq"], p["sf"], p["f"]
        lbv = lb_ref[...]
        dzq_ref[...] = (dq_act.reshape(rows, 128) * (sq * (1.0 + zq * (1.0 - sq)))).astype(dzq_ref.dtype)
        df = dlogf / f - dk.reshape(rows, 128)
        dzf_ref[...] = (df * (1.0 - lbv) * (sf * (1.0 - sf))).astype(dzf_ref.dtype)
        dlb_ref[...] += jnp.sum(df * (1.0 - sf), axis=0, keepdims=True)
        dzi_ref[...] = (dv_scr[...] + dvd.reshape(rows, 128)).astype(dzi_ref.dtype)

    last = ntb - 1

    def col(off):
        return pl.BlockSpec((rows, 128), lambda h, tb: (last - tb, h + off))

    out = pl.BlockSpec((rows, 128), lambda h, tb: (last - tb, h))
    lbs = pl.BlockSpec((1, 128), lambda h, tb: (0, h))
    big = pltpu.VMEM((rows, 128), jnp.float32)
    return pl.pallas_call(
        body, name=name, grid=(heads, ntb),
        out_shape=[jax.ShapeDtypeStruct((t, w), jnp.bfloat16)] * 4
        + [jax.ShapeDtypeStruct((1, w), jnp.float32), jax.ShapeDtypeStruct((heads, 1, 128), jnp.float32)],
        in_specs=[col(0), col(heads), col(2 * heads), col(3 * heads), lbs, pl.BlockSpec((1, 128), lambda h, tb: (0, 0)),
                  out, out, pl.BlockSpec((1, 1, 128, 128), lambda h, tb: (h, last - tb, 0, 0))],
        out_specs=[out, out, out, out, lbs, pl.BlockSpec((1, 1, 128), lambda h, tb: (h, 0, 0))],
        scratch_shapes=[pltpu.VMEM((128, 128), jnp.float32), pltpu.VMEM((nb, 128, 128), jnp.float32)]
        + [pltpu.VMEM((nb, c, 128), jnp.float32)] * 3 + [big] * 3 + [pltpu.VMEM((nb, 8, 128), jnp.float32)],
        compiler_params=pltpu.CompilerParams(dimension_semantics=("parallel", "arbitrary")),
    )(z, z, z, z, lb, norm_g, o_raw, dog, ck)


@functools.partial(jax.custom_vjp, nondiff_argnums=(5,))
def _hgrn_block(h, w_in, w_out, norm_g, lb, name):
    return _hgrn_block_fwd(h, w_in, w_out, norm_g, lb, name)[0]


def _hgrn_block_fwd(h, w_in, w_out, norm_g, lb, name):
    hb = h.astype(jnp.bfloat16)
    z = _matmul(hb, w_in, "nn", name + "_in")
    og, o_raw, ck = _hgrn_fwd_call(z, lb.reshape(1, -1), norm_g.reshape(1, -1), name + "_scan")
    return _matmul(og, w_out, "nn", name + "_out"), (hb, z, og, o_raw, ck, w_in, w_out, norm_g, lb)


def _hgrn_block_bwd(name, saved, dy):
    hb, z, og, o_raw, ck, w_in, w_out, norm_g, lb = saved
    dyb = dy.astype(jnp.bfloat16)
    dog = _matmul(dyb, w_out, "nt", name + "_out_dx")
    dw_out = _matmul(og, dyb, "tn", name + "_out_dw", jnp.bfloat16)
    dzq, dzf, dzi, dzg, dlb, dng = _hgrn_bwd_call(z, lb.reshape(1, -1), norm_g.reshape(1, -1), o_raw, dog, ck,
                                                  name + "_scan_bwd")
    dz = jnp.concatenate([dzq, dzf, dzi, dzg], axis=1)
    dh = _matmul(dz, w_in, "nt", name + "_in_dx")
    dw_in = _matmul(hb, dz, "tn", name + "_in_dw", jnp.bfloat16)
    return dh, dw_in, dw_out, jnp.sum(dng, axis=0).reshape(norm_g.shape), dlb.reshape(lb.shape)


_hgrn_block.defvjp(_hgrn_block_fwd, _hgrn_block_bwd)


def _layer_plan(sharded):
    mixers = [
        [("hgrn_w_in", "col"), ("hgrn_w_out", "row")],
        [("swa_w_in", "colT"), ("swa_w_out", "row")],
        [("sc_w_in", "col"), ("sc_conv_w", "exact"), ("sc_w_out", "row")],
        [("fox_w_in", "colT"), ("fox_w_out", "row")],
    ]
    depth = sharded["ffn_w_up"].shape[0]
    plan = []
    for i in range(depth):
        m, j = i % len(mixers), i // len(mixers)
        entries = [(n, j, kind) for n, kind in mixers[m]]
        entries += [("ffn_w_up", i, "col"), ("ffn_conv_w", i, "exact"), ("ffn_w_down", i, "row")]
        plan.append(entries)
    return plan


def _local_loss(sharded, repl, x, positions, loss_target):
    x = x[0]
    loss_target = loss_target[0]
    d = x.shape[-1]
    plan = _layer_plan(sharded)
    lb_table = jnp.cumsum(jax.nn.softmax(repl["hgrn_lb_param"], axis=0), axis=0)
    for i, entries in enumerate(plan):
        shards = tuple(sharded[n][j] for n, j, _ in entries)
        kinds = tuple(kind for _, _, kind in entries)
        full = _make_layer_gather([s.shape for s in shards], kinds, "layer%d" % i)(shards)
        wts = {n: f for (n, _, _), f in zip(entries, full)}
        m, j = i % 4, i // 4
        tag = "l%d" % i
        hn = _norm(x, repl["mix_pre_g"][i], tag + "_mix_pre")
        if m == 0:
            y = _hgrn_block(hn, wts["hgrn_w_in"], wts["hgrn_w_out"], repl["hgrn_norm_g"][j], lb_table[i], tag + "_hgrn")
        elif m == 1:
            y = _swa_block(hn, positions, wts["swa_w_in"], wts["swa_w_out"], repl["swa_sinks"][j], tag + "_swa")
        elif m == 2:
            y = _sconv_block(hn, wts["sc_w_in"], wts["sc_conv_w"], wts["sc_w_out"], tag + "_sc")
        else:
            w_in = wts["fox_w_in"]
            heads = d // FOX_HEAD_DIM
            w_qkvg = jnp.concatenate([w_in[:3 * d], w_in[3 * d + heads:]], axis=0)
            w_f = jnp.pad(w_in[3 * d:3 * d + heads], ((0, 128 - heads), (0, 0)))
            y = _fox_block(hn, w_qkvg, w_f, repl["fox_b_f"][j], wts["fox_w_out"], tag + "_fox")
        x = _norm_add(y, repl["mix_post_g"][i], x, tag + "_mix_post")
        hn = _norm(x, repl["ffn_pre_g"][i], tag + "_ffn_pre")
        y = _ffn_block(hn, wts["ffn_w_up"], wts["ffn_conv_w"], repl["ffn_conv_b"][i], wts["ffn_w_down"], tag + "_ffn")
        x = _norm_add(y, repl["ffn_post_g"][i], x, tag + "_ffn_post")
    err = jnp.square(x - loss_target)
    return 0.5 * jnp.sum(jnp.mean(err, axis=-1))


SHARDED = ["hgrn_w_in", "hgrn_w_out", "swa_w_in", "swa_w_out", "sc_w_in", "sc_conv_w", "sc_w_out",
           "fox_w_in", "fox_w_out", "ffn_w_up", "ffn_conv_w", "ffn_w_down"]
REPLICATED = ["mix_pre_g", "mix_post_g", "ffn_pre_g", "ffn_post_g", "hgrn_norm_g", "hgrn_lb_param",
              "swa_sinks", "fox_b_f", "ffn_conv_b"]
WEIGHTS = ["mix_pre_g", "mix_post_g", "ffn_pre_g", "ffn_post_g", "hgrn_w_in", "hgrn_w_out", "hgrn_norm_g",
           "hgrn_lb_param", "swa_w_in", "swa_w_out", "swa_sinks", "sc_w_in", "sc_conv_w", "sc_w_out",
           "fox_w_in", "fox_b_f", "fox_w_out", "ffn_w_up", "ffn_conv_w", "ffn_conv_b", "ffn_w_down"]


def _sum_replicated(loss, grads):
    parts = [loss.reshape(1)] + [grads[n].reshape(-1) for n in REPLICATED]
    flat = jnp.concatenate(parts)
    n = flat.shape[0]
    cols = 1024
    rows = -(-n // cols)
    rows += (-rows) % 8
    flat = jnp.pad(flat, (0, rows * cols - n)).reshape(rows, cols)
    total = _sum_blocks(_all_gather(flat, "small_gather"), "small_sum").reshape(-1)
    out, off = {}, 1
    for name in REPLICATED:
        size = grads[name].size
        out[name] = total[off:off + size].reshape(grads[name].shape)
        off += size
    return total[0], out


def kernel(x, positions, mix_pre_g, mix_post_g, ffn_pre_g, ffn_post_g, hgrn_w_in, hgrn_w_out, hgrn_norm_g, hgrn_lb_param, swa_w_in, swa_w_out, swa_sinks, sc_w_in, sc_conv_w, sc_w_out, fox_w_in, fox_b_f, fox_w_out, ffn_w_up, ffn_conv_w, ffn_conv_b, ffn_w_down, loss_target, m_mix_pre_g, m_mix_post_g, m_ffn_pre_g, m_ffn_post_g, m_hgrn_w_in, m_hgrn_w_out, m_hgrn_norm_g, m_hgrn_lb_param, m_swa_w_in, m_swa_w_out, m_swa_sinks, m_sc_w_in, m_sc_conv_w, m_sc_w_out, m_fox_w_in, m_fox_b_f, m_fox_w_out, m_ffn_w_up, m_ffn_conv_w, m_ffn_conv_b, m_ffn_w_down, v_mix_pre_g, v_mix_post_g, v_ffn_pre_g, v_ffn_post_g, v_hgrn_w_in, v_hgrn_w_out, v_hgrn_norm_g, v_hgrn_lb_param, v_swa_w_in, v_swa_w_out, v_swa_sinks, v_sc_w_in, v_sc_conv_w, v_sc_w_out, v_fox_w_in, v_fox_b_f, v_fox_w_out, v_ffn_w_up, v_ffn_conv_w, v_ffn_conv_b, v_ffn_w_down):
    given = dict(locals())
    sharded = {n: given[n] for n in SHARDED}
    repl = {n: given[n] for n in REPLICATED}
    loss, (g_sharded, g_repl, grad_x) = jax.value_and_grad(_local_loss, argnums=(0, 1, 2))(
        sharded, repl, x, positions, loss_target)
    loss, g_repl = _sum_replicated(loss, g_repl)
    grads = {**g_sharded, **g_repl}
    delta, new_m, new_v = {}, {}, {}
    for n in WEIGHTS:
        delta[n], new_m[n], new_v[n] = _adamw(given[n], grads[n], given["m_" + n], given["v_" + n], "adamw_" + n)
    return (loss, grad_x, *[grads[n] for n in WEIGHTS], *[delta[n] for n in WEIGHTS],
            *[new_m[n] for n in WEIGHTS], *[new_v[n] for n in WEIGHTS])
```

```python
import functools
import math

import jax
import jax.numpy as jnp
from jax import lax
from jax.experimental import pallas as pl
from jax.experimental.pallas import tpu as pltpu

N_DEV = 8
MESH = pl.DeviceIdType.MESH
HBM_SPEC = pl.BlockSpec(memory_space=pltpu.HBM)
PACK_COLS = 512
PACK_ROW_BLOCK = 512

MM_TILE = 1024
MM_K_SINGLE = 2048
MM_K_TILE = 1536
MM_VMEM_BYTES = 48 * 1024 * 1024
ROW_BLOCK = 256
COL_BLOCK = 512
HALO = 8
ATT_BLOCK = 256
FOX_HEADS_PER_STEP = 2
HGRN_SUB = 16
HGRN_ROWS = 256
NEG = -1e30

RMS_EPS = 1e-6
HGRN_EXPAND = 128
SWA_HEAD_DIM = 64
SWA_GROUP = 8
SWA_WINDOW = 128
FOX_HEAD_DIM = 64
ROPE_THETA = 500000.0
ROT_DIM = SWA_HEAD_DIM // 4

ADAM_LR = 0.001
ADAM_B1 = 0.9
ADAM_B2 = 0.999
ADAM_EPS = 1e-08
ADAM_WD = 0.01
ADAM_STEP = 10

_NT = (((1,), (1,)), ((), ()))
_NN = (((1,), (0,)), ((), ()))
_TN = (((0,), (0,)), ((), ()))


def _tile(dim, pref):
    if dim <= pref:
        return dim
    t = pref - pref % 128
    while t >= 128:
        if dim % t == 0:
            return t
        t -= 128
    return dim


def _dot(a, b, dims):
    return lax.dot_general(a.astype(jnp.bfloat16), b.astype(jnp.bfloat16), dims, preferred_element_type=jnp.float32)


def _sigmoid(x):
    return 1.0 / (1.0 + jnp.exp(-x))


def _col_to_row(c):
    n = c.shape[0]
    eye = lax.broadcasted_iota(jnp.int32, (n, n), 0) == lax.broadcasted_iota(jnp.int32, (n, n), 1)
    return jnp.sum(jnp.where(eye, c, 0.0), axis=0, keepdims=True)


def _row_to_col(r):
    n = r.shape[1]
    eye = lax.broadcasted_iota(jnp.int32, (n, n), 0) == lax.broadcasted_iota(jnp.int32, (n, n), 1)
    return jnp.sum(jnp.where(eye, r, 0.0), axis=1, keepdims=True)


def _all_gather(x, name):
    r, c = x.shape

    def body(x_ref, out_ref, send_sems, recv_sems, local_sem):
        mx, my, mc = lax.axis_index("x"), lax.axis_index("y"), lax.axis_index("c")
        me, sibling = (mx, my, mc), (mx, my, 1 - mc)
        chips = [(1 - mx, my), (mx, 1 - my), (1 - mx, 1 - my)]

        def rows(px, py, pc):
            return out_ref.at[4 * px + 2 * py + pc]

        def copy(k, block, to, src=None):
            return pltpu.make_async_remote_copy(
                src_ref=rows(*block) if src is None else src, dst_ref=rows(*block),
                send_sem=send_sems.at[k], recv_sem=recv_sems.at[k], device_id=to, device_id_type=MESH)

        mine = pltpu.make_async_copy(x_ref, rows(*me), local_sem)
        mine.start()
        first = [copy(0, me, sibling, src=x_ref)]
        first += [copy(1 + j, me, (*chip, mc), src=x_ref) for j, chip in enumerate(chips)]
        for cp in first:
            cp.start()
        passed = [copy(4 + j, (*chip, mc), sibling) for j, chip in enumerate(chips)]
        for j, chip in enumerate(chips):
            copy(1 + j, (*chip, mc), me).wait_recv()
            passed[j].start()
        copy(0, sibling, me).wait_recv()
        for j, chip in enumerate(chips):
            copy(4 + j, (*chip, 1 - mc), me).wait_recv()
        for cp in first + passed:
            cp.wait_send()
        mine.wait()

    return pl.pallas_call(
        body, name=name, out_shape=jax.ShapeDtypeStruct((N_DEV, r, c), x.dtype),
        in_specs=[HBM_SPEC], out_specs=HBM_SPEC,
        scratch_shapes=[pltpu.SemaphoreType.DMA((7,)), pltpu.SemaphoreType.DMA((7,)), pltpu.SemaphoreType.DMA],
    )(x)


def _pair_exchange(x, name):
    n = x.shape[0] // 2

    def body(x_ref, out_ref, send_sems, recv_sems):
        mc = lax.axis_index("c")
        sibling = (lax.axis_index("x"), lax.axis_index("y"), 1 - mc)
        copies = [pltpu.make_async_remote_copy(
            src_ref=x_ref.at[2 * j + 1 - mc], dst_ref=out_ref.at[j], send_sem=send_sems.at[j],
            recv_sem=recv_sems.at[j], device_id=sibling, device_id_type=MESH) for j in range(n)]
        for cp in copies:
            cp.start()
        for cp in copies:
            cp.wait()

    return pl.pallas_call(
        body, name=name, out_shape=jax.ShapeDtypeStruct((n,) + x.shape[1:], x.dtype),
        in_specs=[HBM_SPEC], out_specs=HBM_SPEC,
        scratch_shapes=[pltpu.SemaphoreType.DMA((n,)), pltpu.SemaphoreType.DMA((n,))],
    )(x)


def _chip_exchange(x, name):
    flips = [(1, 0), (0, 1), (1, 1)]

    def body(x_ref, out_ref, send_sems, recv_sems, local_sem):
        mx, my, mc = lax.axis_index("x"), lax.axis_index("y"), lax.axis_index("c")
        me = 2 * mx + my

        def chip(flip):
            return ((1 - mx) if flip[0] else mx, (1 - my) if flip[1] else my)

        def copy(k, dst_block):
            px, py = chip(flips[k])
            return pltpu.make_async_remote_copy(
                src_ref=x_ref.at[2 * px + py], dst_ref=out_ref.at[dst_block],
                send_sem=send_sems.at[k], recv_sem=recv_sems.at[k], device_id=(px, py, mc), device_id_type=MESH)

        mine = pltpu.make_async_copy(x_ref.at[me], out_ref.at[me], local_sem)
        mine.start()
        sends = [copy(k, me) for k in range(len(flips))]
        for cp in sends:
            cp.start()
        for k in range(len(flips)):
            px, py = chip(flips[k])
            copy(k, 2 * px + py).wait_recv()
        for cp in sends:
            cp.wait_send()
        mine.wait()

    return pl.pallas_call(
        body, name=name, out_shape=jax.ShapeDtypeStruct(x.shape, x.dtype),
        in_specs=[HBM_SPEC], out_specs=HBM_SPEC,
        scratch_shapes=[pltpu.SemaphoreType.DMA((3,)), pltpu.SemaphoreType.DMA((3,)), pltpu.SemaphoreType.DMA],
    )(x)


def _sum_blocks(x, name, out_dtype=jnp.float32):
    n, r, c = x.shape
    tr = PACK_ROW_BLOCK if r % PACK_ROW_BLOCK == 0 else r

    def body(x_ref, o_ref):
        acc = x_ref[0].astype(jnp.float32)
        for j in range(1, n):
            acc = acc + x_ref[j].astype(jnp.float32)
        o_ref[...] = acc.astype(o_ref.dtype)

    return pl.pallas_call(
        body, name=name, grid=(r // tr,), out_shape=jax.ShapeDtypeStruct((r, c), out_dtype),
        in_specs=[pl.BlockSpec((n, tr, c), lambda i: (0, i, 0))],
        out_specs=pl.BlockSpec((tr, c), lambda i: (i, 0)),
    )(x)


def _pair_sum(x, got, name):
    n, _, r, c = x.shape
    tr = PACK_ROW_BLOCK if r % PACK_ROW_BLOCK == 0 else r

    def body(x_ref, g_ref, o_ref):
        mine = jnp.where(lax.axis_index("c") == 0, x_ref[0, 0], x_ref[0, 1])
        o_ref[0] = (mine.astype(jnp.float32) + g_ref[0].astype(jnp.float32)).astype(o_ref.dtype)

    spec = pl.BlockSpec((1, tr, c), lambda j, i: (j, i, 0))
    return pl.pallas_call(
        body, name=name, grid=(n, r // tr), out_shape=jax.ShapeDtypeStruct(got.shape, got.dtype),
        in_specs=[pl.BlockSpec((1, 2, tr, c), lambda j, i: (j, 0, i, 0)), spec], out_specs=spec,
    )(x, got)


def _reduce_scatter(x, name):
    _, r, c = x.shape
    paired = _pair_sum(x.reshape(4, 2, r, c), _pair_exchange(x, name + "_pair"), name + "_pair_sum")
    return _sum_blocks(_chip_exchange(paired, name + "_chips"), name + "_sum")


def _matmul(a, b, mode, name, out_dtype=jnp.float32):
    if mode == "nn":
        (m, k), (_, n) = a.shape, b.shape
    elif mode == "nt":
        (m, k), (n, _) = a.shape, b.shape
    else:
        (k, m), (_, n) = a.shape, b.shape
    tm, tn = _tile(m, MM_TILE), _tile(n, MM_TILE)
    tk = k if k <= MM_K_SINGLE else _tile(k, MM_K_TILE)
    nk = k // tk
    dims = {"nn": _NN, "nt": _NT, "tn": _TN}[mode]
    rows_inner = nk == 1 and (n // tn) * a.size < (m // tm) * b.size

    def body(a_ref, b_ref, o_ref, *scratch):
        part = lax.dot_general(a_ref[...], b_ref[...], dims, preferred_element_type=jnp.float32)
        if nk == 1:
            o_ref[...] = part.astype(o_ref.dtype)
            return
        acc_ref, = scratch
        kk = pl.program_id(2)

        @pl.when(kk == 0)
        def _():
            acc_ref[...] = part

        @pl.when(kk > 0)
        def _():
            acc_ref[...] += part

        @pl.when(kk == nk - 1)
        def _():
            o_ref[...] = acc_ref[...].astype(o_ref.dtype)

    def ij(g0, g1):
        return (g1, g0) if rows_inner else (g0, g1)

    def a_map(g0, g1, kk):
        i, _ = ij(g0, g1)
        return (kk, i) if mode == "tn" else (i, kk)

    def b_map(g0, g1, kk):
        _, j = ij(g0, g1)
        return (j, kk) if mode == "nt" else (kk, j)

    def o_map(g0, g1, kk):
        return ij(g0, g1)

    grid = (n // tn, m // tm, nk) if rows_inner else (m // tm, n // tn, nk)
    return pl.pallas_call(
        body, name=name, grid=grid, out_shape=jax.ShapeDtypeStruct((m, n), out_dtype),
        in_specs=[pl.BlockSpec((tk, tm) if mode == "tn" else (tm, tk), a_map),
                  pl.BlockSpec((tn, tk) if mode == "nt" else (tk, tn), b_map)],
        out_specs=pl.BlockSpec((tm, tn), o_map),
        scratch_shapes=[pltpu.VMEM((tm, tn), jnp.float32)] if nk > 1 else [],
        compiler_params=pltpu.CompilerParams(dimension_semantics=("parallel", "parallel", "arbitrary"),
                                             vmem_limit_bytes=MM_VMEM_BYTES),
    )(a, b)


def _pad_rows(flat, axis):
    n = flat.shape[axis]
    unit = PACK_COLS * PACK_ROW_BLOCK
    pad = (-n) % unit
    if pad:
        widths = [(0, 0)] * flat.ndim
        widths[axis] = (0, pad)
        flat = jnp.pad(flat, widths)
    return flat


def _gather_pack(shards, kinds):
    parts = []
    for s, kind in zip(shards, kinds):
        if kind == "exact":
            parts.append(lax.bitcast_convert_type(s, jnp.bfloat16).reshape(-1))
        elif kind == "colT":
            parts.append(s.T.astype(jnp.bfloat16).reshape(-1))
        else:
            parts.append(s.astype(jnp.bfloat16).reshape(-1))
    return _pad_rows(jnp.concatenate(parts), 0).reshape(-1, PACK_COLS)


def _gather_unpack(gathered, shapes, kinds):
    flat = gathered.reshape(N_DEV, -1)
    out, off = [], 0
    for shape, kind in zip(shapes, kinds):
        size = math.prod(shape)
        if kind == "colT":
            out.append(flat[:, off:off + size].reshape(N_DEV * shape[1], shape[0]))
            off += size
            continue
        if kind == "exact":
            piece = flat[:, off:off + 2 * size].reshape((N_DEV,) + tuple(shape) + (2,))
            piece = lax.bitcast_convert_type(piece, jnp.float32)
            off += 2 * size
            kind = "col"
        else:
            piece = flat[:, off:off + size].reshape((N_DEV,) + tuple(shape))
            off += size
        if kind == "col":
            piece = jnp.moveaxis(piece, 0, -2)
            piece = piece.reshape(piece.shape[:-2] + (N_DEV * shape[-1],))
        else:
            piece = piece.reshape((N_DEV * shape[0],) + tuple(shape[1:]))
        out.append(piece)
    return tuple(out)


def _scatter_pack(cts, shapes, kinds):
    parts = []
    for ct, shape, kind in zip(cts, shapes, kinds):
        ct = ct.astype(jnp.bfloat16)
        if kind in ("row", "colT"):
            piece = ct.reshape(N_DEV, -1)
        else:
            piece = ct.reshape(tuple(shape[:-1]) + (N_DEV, shape[-1]))
            piece = jnp.moveaxis(piece, -2, 0)
        parts.append(piece.reshape(N_DEV, -1))
    return _pad_rows(jnp.concatenate(parts, axis=1), 1).reshape(N_DEV, -1, PACK_COLS)


def _scatter_unpack(summed, shapes, kinds):
    flat = summed.reshape(-1)
    out, off = [], 0
    for shape, kind in zip(shapes, kinds):
        size = math.prod(shape)
        piece = flat[off:off + size]
        out.append(piece.reshape(shape[1], shape[0]).T if kind == "colT" else piece.reshape(shape))
        off += size
    return tuple(out)


def _make_layer_gather(shapes, kinds, tag):
    shapes = tuple(tuple(s) for s in shapes)

    def impl(shards):
        return _gather_unpack(_all_gather(_gather_pack(shards, kinds), tag + "_gather"), shapes, kinds)

    gather = jax.custom_vjp(impl)

    def fwd(shards):
        return impl(shards), None

    def bwd(_, cts):
        packed = _scatter_pack(cts, shapes, kinds)
        return (_scatter_unpack(_reduce_scatter(packed, tag + "_scatter"), shapes, kinds),)

    gather.defvjp(fwd, bwd)
    return gather


def _adamw(w, g, m, v, name):
    shape = w.shape
    cols = shape[-1]
    rows = math.prod(shape[:-1])
    tr = rows
    if rows > 512:
        for cand in (256, 128, 64, 32, 16, 8):
            if rows % cand == 0:
                tr = cand
                break
    c1 = 1.0 - ADAM_B1 ** ADAM_STEP
    c2 = 1.0 - ADAM_B2 ** ADAM_STEP

    def body(w_ref, g_ref, m_ref, v_ref, d_ref, nm_ref, nv_ref):
        gg = g_ref[...]
        nm = ADAM_B1 * m_ref[...] + (1.0 - ADAM_B1) * gg
        nv = ADAM_B2 * v_ref[...] + (1.0 - ADAM_B2) * (gg * gg)
        m_hat = nm / c1
        v_hat = nv / c2
        d_ref[...] = -ADAM_LR * (m_hat / (jnp.sqrt(v_hat) + ADAM_EPS) + ADAM_WD * w_ref[...])
        nm_ref[...] = nm
        nv_ref[...] = nv

    spec = pl.BlockSpec((tr, cols), lambda i: (i, 0))
    outs = pl.pallas_call(
        body, name=name, grid=(rows // tr,), out_shape=[jax.ShapeDtypeStruct((rows, cols), jnp.float32)] * 3,
        in_specs=[spec] * 4, out_specs=[spec] * 3,
    )(*[t.reshape(rows, cols) for t in (w, g, m, v)])
    return tuple(o.reshape(shape) for o in outs)


def _rms_fwd_call(x, g, res, name):
    t, d = x.shape
    tr = _tile(t, ROW_BLOCK)
    with_res = res is not None

    def body(*refs):
        if with_res:
            x_ref, g_ref, r_ref, y_ref = refs
        else:
            x_ref, g_ref, y_ref = refs
        xv = x_ref[...]
        rstd = lax.rsqrt(jnp.mean(xv * xv, axis=-1, keepdims=True) + RMS_EPS)
        y = xv * rstd * g_ref[...]
        if with_res:
            y = y + r_ref[...]
        y_ref[...] = y

    row = pl.BlockSpec((tr, d), lambda i: (i, 0))
    gspec = pl.BlockSpec((1, d), lambda i: (0, 0))
    ins = [x, g] + ([res] if with_res else [])
    return pl.pallas_call(
        body, name=name, grid=(t // tr,), out_shape=jax.ShapeDtypeStruct((t, d), jnp.float32),
        in_specs=[row, gspec] + ([row] if with_res else []), out_specs=row,
    )(*ins)


def _rms_bwd_call(x, g, dy, name):
    t, d = x.shape
    tr = _tile(t, ROW_BLOCK)

    def body(x_ref, g_ref, dy_ref, dx_ref, dg_ref):
        i = pl.program_id(0)
        xv = x_ref[...]
        rstd = lax.rsqrt(jnp.mean(xv * xv, axis=-1, keepdims=True) + RMS_EPS)
        xh = xv * rstd
        dyv = dy_ref[...]
        dyg = dyv * g_ref[...]
        dx_ref[...] = rstd * (dyg - xh * jnp.mean(dyg * xh, axis=-1, keepdims=True))

        @pl.when(i == 0)
        def _():
            dg_ref[...] = jnp.zeros_like(dg_ref)

        dg_ref[...] += jnp.sum(dyv * xh, axis=0, keepdims=True)

    row = pl.BlockSpec((tr, d), lambda i: (i, 0))
    gspec = pl.BlockSpec((1, d), lambda i: (0, 0))
    return pl.pallas_call(
        body, name=name, grid=(t // tr,),
        out_shape=[jax.ShapeDtypeStruct((t, d), jnp.float32), jax.ShapeDtypeStruct((1, d), jnp.float32)],
        in_specs=[row, gspec, row], out_specs=[row, gspec],
        compiler_params=pltpu.CompilerParams(dimension_semantics=("arbitrary",)),
    )(x, g, dy)


@functools.partial(jax.custom_vjp, nondiff_argnums=(3,))
def _norm_add(y, g, res, name):
    return _rms_fwd_call(y, g.reshape(1, -1), res, name + "_fwd")


def _norm_add_fwd(y, g, res, name):
    return _rms_fwd_call(y, g.reshape(1, -1), res, name + "_fwd"), (y, g)


def _norm_add_bwd(name, saved, dout):
    y, g = saved
    dy, dg = _rms_bwd_call(y, g.reshape(1, -1), dout, name + "_bwd")
    return dy, dg.reshape(g.shape), dout


_norm_add.defvjp(_norm_add_fwd, _norm_add_bwd)


@functools.partial(jax.custom_vjp, nondiff_argnums=(2,))
def _norm(x, g, name):
    return _rms_fwd_call(x, g.reshape(1, -1), None, name + "_fwd")


def _norm_fwd(x, g, name):
    return _rms_fwd_call(x, g.reshape(1, -1), None, name + "_fwd"), (x, g)


def _norm_bwd(name, saved, dout):
    x, g = saved
    dx, dg = _rms_bwd_call(x, g.reshape(1, -1), dout, name + "_bwd")
    return dx, dg.reshape(g.shape)


_norm.defvjp(_norm_fwd, _norm_bwd)


def _shift_down(x, before1, before2, row):
    xm1 = jnp.where(row == 0, before1, pltpu.roll(x, 1, 0))
    xm2 = jnp.where(row == 0, before2, jnp.where(row == 1, before1, pltpu.roll(x, 2, 0)))
    return xm1, xm2


def _shift_up(x, after1, after2, row):
    n = x.shape[0]
    xp1 = jnp.where(row == n - 1, after1, pltpu.roll(x, n - 1, 0))
    xp2 = jnp.where(row == n - 1, after2, jnp.where(row == n - 2, after1, pltpu.roll(x, n - 2, 0)))
    return xp1, xp2


def _ffn_act_fwd_call(u, conv_w, conv_b, name):
    t, f2 = u.shape
    f = f2 // 2
    tr, tc = _tile(t, ROW_BLOCK), _tile(f, COL_BLOCK)
    nj, hb = f // tc, tr // HALO

    def body(ug_ref, uu_ref, hg_ref, hu_ref, wg_ref, wu_ref, bg_ref, bu_ref, cg_ref, cu_ref, a_ref):
        i = pl.program_id(0)
        row = lax.broadcasted_iota(jnp.int32, (tr, tc), 0)

        def conv(x_ref, h_ref, w_ref, b_ref):
            xv = x_ref[...]
            h1 = jnp.where(i > 0, h_ref[7:8, :], 0.0)
            h2 = jnp.where(i > 0, h_ref[6:7, :], 0.0)
            xm1, xm2 = _shift_down(xv, h1, h2, row)
            return w_ref[0:1, :] * xm2 + w_ref[1:2, :] * xm1 + w_ref[2:3, :] * xv + b_ref[...]

        cg = conv(ug_ref, hg_ref, wg_ref, bg_ref)
        cu = conv(uu_ref, hu_ref, wu_ref, bu_ref)
        cg_ref[...] = cg
        cu_ref[...] = cu
        a_ref[...] = (cg * _sigmoid(cg) * cu).astype(jnp.bfloat16)

    def blk(off):
        return pl.BlockSpec((tr, tc), lambda i, j: (i, j + off))

    def halo(off):
        return pl.BlockSpec((HALO, tc), lambda i, j: (jnp.maximum(i * hb - 1, 0), j + off))

    def wspec(rows, off):
        return pl.BlockSpec((rows, tc), lambda i, j: (0, j + off))

    out_blk = pl.BlockSpec((tr, tc), lambda i, j: (i, j))
    return pl.pallas_call(
        body, name=name, grid=(t // tr, nj),
        out_shape=[jax.ShapeDtypeStruct((t, f), jnp.float32), jax.ShapeDtypeStruct((t, f), jnp.float32),
                   jax.ShapeDtypeStruct((t, f), jnp.bfloat16)],
        in_specs=[blk(0), blk(nj), halo(0), halo(nj), wspec(3, 0), wspec(3, nj), wspec(1, 0), wspec(1, nj)],
        out_specs=[out_blk, out_blk, out_blk],
    )(u, u, u, u, conv_w, conv_w, conv_b, conv_b)


def _ffn_act_bwd_call(da, cg, cu, u, conv_w, name):
    t, f = da.shape
    tr, tc = _tile(t, ROW_BLOCK), _tile(f, COL_BLOCK)
    nj, hb, ni = f // tc, tr // HALO, t // tr

    def body(da_ref, cg_ref, cu_ref, nda_ref, ncg_ref, ncu_ref, u_ref, hu_ref, w_ref, du_ref, dw_ref, db_ref):
        j, i = pl.program_id(0), pl.program_id(1)
        is_gate = j < nj
        row = lax.broadcasted_iota(jnp.int32, (tr, tc), 0)

        def duc_of(dav, cgv, cuv):
            sg = _sigmoid(cgv)
            d_gate = dav * cuv * (sg * (1.0 + cgv * (1.0 - sg)))
            d_up = dav * (cgv * sg)
            return jnp.where(is_gate, d_gate, d_up)

        duc = duc_of(da_ref[...], cg_ref[...], cu_ref[...])
        n1 = jnp.where(i < ni - 1, duc_of(nda_ref[0:1, :], ncg_ref[0:1, :], ncu_ref[0:1, :]), 0.0)
        n2 = jnp.where(i < ni - 1, duc_of(nda_ref[1:2, :], ncg_ref[1:2, :], ncu_ref[1:2, :]), 0.0)
        dp1, dp2 = _shift_up(duc, n1, n2, row)
        du_ref[...] = (w_ref[2:3, :] * duc + w_ref[1:2, :] * dp1 + w_ref[0:1, :] * dp2).astype(jnp.bfloat16)

        uv = u_ref[...]
        h1 = jnp.where(i > 0, hu_ref[7:8, :], 0.0)
        h2 = jnp.where(i > 0, hu_ref[6:7, :], 0.0)
        um1, um2 = _shift_down(uv, h1, h2, row)

        @pl.when(i == 0)
        def _():
            dw_ref[...] = jnp.zeros_like(dw_ref)
            db_ref[...] = jnp.zeros_like(db_ref)

        rows8 = lax.broadcasted_iota(jnp.int32, (HALO, tc), 0)
        s0 = jnp.sum(duc * um2, axis=0, keepdims=True)
        s1 = jnp.sum(duc * um1, axis=0, keepdims=True)
        s2 = jnp.sum(duc * uv, axis=0, keepdims=True)
        dw_ref[...] += jnp.where(rows8 == 0, s0, jnp.where(rows8 == 1, s1, jnp.where(rows8 == 2, s2, 0.0)))
        db_ref[...] += jnp.sum(duc, axis=0, keepdims=True)

    half = pl.BlockSpec((tr, tc), lambda j, i: (i, j % nj))
    nhalf = pl.BlockSpec((HALO, tc), lambda j, i: (jnp.minimum((i + 1) * hb, t // HALO - 1), j % nj))
    full = pl.BlockSpec((tr, tc), lambda j, i: (i, j))
    hfull = pl.BlockSpec((HALO, tc), lambda j, i: (jnp.maximum(i * hb - 1, 0), j))
    return pl.pallas_call(
        body, name=name, grid=(2 * nj, ni),
        out_shape=[jax.ShapeDtypeStruct((t, 2 * f), jnp.bfloat16), jax.ShapeDtypeStruct((HALO, 2 * f), jnp.float32),
                   jax.ShapeDtypeStruct((1, 2 * f), jnp.float32)],
        in_specs=[half, half, half, nhalf, nhalf, nhalf, full, hfull, pl.BlockSpec((3, tc), lambda j, i: (0, j))],
        out_specs=[full, pl.BlockSpec((HALO, tc), lambda j, i: (0, j)), pl.BlockSpec((1, tc), lambda j, i: (0, j))],
        compiler_params=pltpu.CompilerParams(dimension_semantics=("parallel", "arbitrary")),
    )(da, cg, cu, da, cg, cu, u, u, conv_w)


@functools.partial(jax.custom_vjp, nondiff_argnums=(5,))
def _ffn_block(h, w_up, conv_w, conv_b, w_down, name):
    return _ffn_block_fwd(h, w_up, conv_w, conv_b, w_down, name)[0]


def _ffn_block_fwd(h, w_up, conv_w, conv_b, w_down, name):
    hb = h.astype(jnp.bfloat16)
    u = _matmul(hb, w_up, "nn", name + "_up")
    cg, cu, a = _ffn_act_fwd_call(u, conv_w, conv_b.reshape(1, -1), name + "_act")
    return _matmul(a, w_down, "nn", name + "_down"), (hb, u, cg, cu, a, w_up, conv_w, conv_b, w_down)


def _ffn_block_bwd(name, saved, dy):
    hb, u, cg, cu, a, w_up, conv_w, conv_b, w_down = saved
    dyb = dy.astype(jnp.bfloat16)
    da = _matmul(dyb, w_down, "nt", name + "_down_dx")
    dw_down = _matmul(a, dyb, "tn", name + "_down_dw", jnp.bfloat16)
    du, dcw, dcb = _ffn_act_bwd_call(da, cg, cu, u, conv_w, name + "_act_bwd")
    dh = _matmul(du, w_up, "nt", name + "_up_dx")
    dw_up = _matmul(hb, du, "tn", name + "_up_dw", jnp.bfloat16)
    return dh, dw_up, dcw[:3], dcb.reshape(conv_b.shape), dw_down


_ffn_block.defvjp(_ffn_block_fwd, _ffn_block_bwd)


def _sconv_fwd_call(z, conv_w, name):
    t, d3 = z.shape
    d = d3 // 3
    tr, tc = _tile(t, ROW_BLOCK), _tile(d, COL_BLOCK)
    nj, hb = d // tc, tr // HALO

    def body(b_ref, c_ref, x_ref, hc_ref, hx_ref, w_ref, a_ref):
        i = pl.program_id(0)
        row = lax.broadcasted_iota(jnp.int32, (tr, tc), 0)
        p = c_ref[...] * x_ref[...]
        h1 = jnp.where(i > 0, hc_ref[7:8, :] * hx_ref[7:8, :], 0.0)
        h2 = jnp.where(i > 0, hc_ref[6:7, :] * hx_ref[6:7, :], 0.0)
        pm1, pm2 = _shift_down(p, h1, h2, row)
        pc = w_ref[0:1, :] * pm2 + w_ref[1:2, :] * pm1 + w_ref[2:3, :] * p
        a_ref[...] = (b_ref[...] * pc).astype(jnp.bfloat16)

    def blk(off):
        return pl.BlockSpec((tr, tc), lambda i, j: (i, j + off))

    def halo(off):
        return pl.BlockSpec((HALO, tc), lambda i, j: (jnp.maximum(i * hb - 1, 0), j + off))

    return pl.pallas_call(
        body, name=name, grid=(t // tr, nj), out_shape=jax.ShapeDtypeStruct((t, d), jnp.bfloat16),
        in_specs=[blk(0), blk(nj), blk(2 * nj), halo(nj), halo(2 * nj), pl.BlockSpec((3, tc), lambda i, j: (0, j))],
        out_specs=pl.BlockSpec((tr, tc), lambda i, j: (i, j)),
    )(z, z, z, z, z, conv_w)


def _sconv_bwd_call(da, z, conv_w, name):
    t, d = da.shape
    tr, tc = _tile(t, ROW_BLOCK), _tile(d, COL_BLOCK)
    nj, hb, ni = d // tc, tr // HALO, t // tr

    def body(da_ref, b_ref, c_ref, x_ref, hc_ref, hx_ref, nda_ref, nb_ref, w_ref, db_ref, dc_ref, dx_ref, dw_ref):
        i = pl.program_id(1)
        row = lax.broadcasted_iota(jnp.int32, (tr, tc), 0)
        cv, xv, dav = c_ref[...], x_ref[...], da_ref[...]
        p = cv * xv
        h1 = jnp.where(i > 0, hc_ref[7:8, :] * hx_ref[7:8, :], 0.0)
        h2 = jnp.where(i > 0, hc_ref[6:7, :] * hx_ref[6:7, :], 0.0)
        pm1, pm2 = _shift_down(p, h1, h2, row)
        pc = w_ref[0:1, :] * pm2 + w_ref[1:2, :] * pm1 + w_ref[2:3, :] * p
        db_ref[...] = (dav * pc).astype(jnp.bfloat16)
        dpc = dav * b_ref[...]
        n1 = jnp.where(i < ni - 1, nda_ref[0:1, :] * nb_ref[0:1, :], 0.0)
        n2 = jnp.where(i < ni - 1, nda_ref[1:2, :] * nb_ref[1:2, :], 0.0)
        dp1, dp2 = _shift_up(dpc, n1, n2, row)
        dp = w_ref[2:3, :] * dpc + w_ref[1:2, :] * dp1 + w_ref[0:1, :] * dp2
        dc_ref[...] = (dp * xv).astype(jnp.bfloat16)
        dx_ref[...] = (dp * cv).astype(jnp.bfloat16)

        @pl.when(i == 0)
        def _():
            dw_ref[...] = jnp.zeros_like(dw_ref)

        rows8 = lax.broadcasted_iota(jnp.int32, (HALO, tc), 0)
        s0 = jnp.sum(dpc * pm2, axis=0, keepdims=True)
        s1 = jnp.sum(dpc * pm1, axis=0, keepdims=True)
        s2 = jnp.sum(dpc * p, axis=0, keepdims=True)
        dw_ref[...] += jnp.where(rows8 == 0, s0, jnp.where(rows8 == 1, s1, jnp.where(rows8 == 2, s2, 0.0)))

    def blk(off):
        return pl.BlockSpec((tr, tc), lambda j, i: (i, j + off))

    def halo(off):
        return pl.BlockSpec((HALO, tc), lambda j, i: (jnp.maximum(i * hb - 1, 0), j + off))

    def nhalo(off):
        return pl.BlockSpec((HALO, tc), lambda j, i: (jnp.minimum((i + 1) * hb, t // HALO - 1), j + off))

    out = pl.BlockSpec((tr, tc), lambda j, i: (i, j))
    return pl.pallas_call(
        body, name=name, grid=(nj, ni),
        out_shape=[jax.ShapeDtypeStruct((t, d), jnp.bfloat16)] * 3 + [jax.ShapeDtypeStruct((HALO, d), jnp.float32)],
        in_specs=[blk(0), blk(0), blk(nj), blk(2 * nj), halo(nj), halo(2 * nj), nhalo(0), nhalo(0),
                  pl.BlockSpec((3, tc), lambda j, i: (0, j))],
        out_specs=[out, out, out, pl.BlockSpec((HALO, tc), lambda j, i: (0, j))],
        compiler_params=pltpu.CompilerParams(dimension_semantics=("parallel", "arbitrary")),
    )(da, z, z, z, z, z, da, z, conv_w)


@functools.partial(jax.custom_vjp, nondiff_argnums=(4,))
def _sconv_block(h, w_in, conv_w, w_out, name):
    return _sconv_block_fwd(h, w_in, conv_w, w_out, name)[0]


def _sconv_block_fwd(h, w_in, conv_w, w_out, name):
    hb = h.astype(jnp.bfloat16)
    z = _matmul(hb, w_in, "nn", name + "_in")
    a = _sconv_fwd_call(z, conv_w, name + "_gate")
    return _matmul(a, w_out, "nn", name + "_out"), (hb, z, a, w_in, conv_w, w_out)


def _sconv_block_bwd(name, saved, dy):
    hb, z, a, w_in, conv_w, w_out = saved
    dyb = dy.astype(jnp.bfloat16)
    da = _matmul(dyb, w_out, "nt", name + "_out_dx")
    dw_out = _matmul(a, dyb, "tn", name + "_out_dw", jnp.bfloat16)
    db, dc, dx, dcw = _sconv_bwd_call(da, z, conv_w, name + "_gate_bwd")
    dz = jnp.concatenate([db, dc, dx], axis=1)
    dh = _matmul(dz, w_in, "nt", name + "_in_dx")
    dw_in = _matmul(hb, dz, "tn", name + "_in_dw", jnp.bfloat16)
    return dh, dw_in, dcw[:3], dw_out


_sconv_block.defvjp(_sconv_block_fwd, _sconv_block_bwd)


def _to_heads(x, heads):
    t = x.shape[0]
    return x.reshape(t, heads, -1).transpose(1, 0, 2)


def _from_heads(x):
    h, t, d = x.shape
    return x.transpose(1, 0, 2).reshape(t, h * d)


def _partial_rope(x, positions):
    half = ROT_DIM // 2
    inv_freq = ROPE_THETA ** (-jnp.arange(half, dtype=jnp.float32) / half)
    ang = positions.astype(jnp.float32)[:, None] * inv_freq[None, :]
    cos = jnp.cos(ang)[:, None, :]
    sin = jnp.sin(ang)[:, None, :]
    x1, x2 = x[..., :half], x[..., half:ROT_DIM]
    return jnp.concatenate([x1 * cos - x2 * sin, x2 * cos + x1 * sin, x[..., ROT_DIM:]], axis=-1)


def _swa_probs(q_ref, kp_ref, kc_ref, sink_ref, n, w, scale):
    grp, d = q_ref.shape[0], q_ref.shape[2]
    q2 = q_ref[...].reshape(grp * w, d)
    kcat = jnp.concatenate([kp_ref[0], kc_ref[0]], axis=0)
    s3 = (_dot(q2, kcat, _NT) * scale).reshape(grp, w, 2 * w)
    qi = lax.broadcasted_iota(jnp.int32, s3.shape, 1)
    kj = lax.broadcasted_iota(jnp.int32, s3.shape, 2)
    diff = qi + w - kj
    allowed = (diff >= 0) & (diff < w) & ((kj >= w) | (n > 0))
    s3 = jnp.where(allowed, s3, NEG)
    sink = sink_ref[...]
    m = jnp.maximum(jnp.max(s3, axis=2, keepdims=True), sink)
    e = jnp.exp(s3 - m)
    es = jnp.exp(sink - m)
    z = jnp.sum(e, axis=2, keepdims=True) + es
    return q2, kcat, e / z, es / z


def _swa_fwd_call(q, k, v, sinks, name):
    qh, t, d = q.shape
    kv = k.shape[0]
    grp, w = qh // kv, SWA_WINDOW
    nblk = t // w
    scale = d ** -0.5

    def body(q_ref, kp_ref, kc_ref, vp_ref, vc_ref, sink_ref, o_ref):
        n = pl.program_id(1)
        _, _, p3, _ = _swa_probs(q_ref, kp_ref, kc_ref, sink_ref, n, w, scale)
        vcat = jnp.concatenate([vp_ref[0], vc_ref[0]], axis=0)
        o = _dot(p3.reshape(grp * w, 2 * w), vcat, _NN)
        o_ref[...] = o.reshape(grp, w, d).astype(o_ref.dtype)

    qspec = pl.BlockSpec((grp, w, d), lambda h, n: (h, n, 0))
    prev = pl.BlockSpec((1, w, d), lambda h, n: (h, jnp.maximum(n - 1, 0), 0))
    cur = pl.BlockSpec((1, w, d), lambda h, n: (h, n, 0))
    return pl.pallas_call(
        body, name=name, grid=(kv, nblk), out_shape=jax.ShapeDtypeStruct((qh, t, d), jnp.bfloat16),
        in_specs=[qspec, prev, cur, prev, cur, pl.BlockSpec((grp, 1, 1), lambda h, n: (h, 0, 0))],
        out_specs=qspec,
    )(q, k, k, v, v, sinks)


def _swa_bwd_call(q, k, v, sinks, do, name):
    qh, t, d = q.shape
    kv = k.shape[0]
    grp, w = qh // kv, SWA_WINDOW
    nblk = t // w
    scale = d ** -0.5

    def body(q_ref, kp_ref, kc_ref, vp_ref, vc_ref, sink_ref, do_ref, dq_ref, dk_ref, dv_ref, ds_ref,
             carry_k, carry_v, part_k, part_v):
        n = pl.program_id(1)

        @pl.when(n == 0)
        def _():
            carry_k[...] = jnp.zeros_like(carry_k)
            carry_v[...] = jnp.zeros_like(carry_v)
            ds_ref[...] = jnp.zeros_like(ds_ref)

        @pl.when(n < nblk)
        def _():
            q2, kcat, p3, ps = _swa_probs(q_ref, kp_ref, kc_ref, sink_ref, n, w, scale)
            vcat = jnp.concatenate([vp_ref[0], vc_ref[0]], axis=0)
            do2 = do_ref[...].reshape(grp * w, d)
            dp3 = _dot(do2, vcat, _NT).reshape(grp, w, 2 * w)
            rs = jnp.sum(p3 * dp3, axis=2, keepdims=True)
            ds2 = (p3 * (dp3 - rs)).reshape(grp * w, 2 * w)
            dsink = -jnp.sum(ps * rs, axis=1, keepdims=True)
            ds_ref[...] += jnp.broadcast_to(dsink, ds_ref.shape)
            dq_ref[...] = (_dot(ds2, kcat, _NN) * scale).reshape(grp, w, d)
            part_k[...] = _dot(ds2, q2, _TN) * scale
            part_v[...] = _dot(p3.reshape(grp * w, 2 * w), do2, _TN)

        @pl.when(n == nblk)
        def _():
            part_k[...] = jnp.zeros_like(part_k)
            part_v[...] = jnp.zeros_like(part_v)

        dk_ref[0] = carry_k[...] + part_k[0:w, :]
        dv_ref[0] = carry_v[...] + part_v[0:w, :]
        carry_k[...] = part_k[w:2 * w, :]
        carry_v[...] = part_v[w:2 * w, :]

    last = nblk - 1
    qspec = pl.BlockSpec((grp, w, d), lambda h, n: (h, jnp.minimum(n, last), 0))
    prev = pl.BlockSpec((1, w, d), lambda h, n: (h, jnp.maximum(jnp.minimum(n, last) - 1, 0), 0))
    cur = pl.BlockSpec((1, w, d), lambda h, n: (h, jnp.minimum(n, last), 0))
    kout = pl.BlockSpec((1, w, d), lambda h, n: (h, jnp.maximum(n - 1, 0), 0))
    return pl.pallas_call(
        body, name=name, grid=(kv, nblk + 1),
        out_shape=[jax.ShapeDtypeStruct((qh, t, d), jnp.float32), jax.ShapeDtypeStruct((kv, t, d), jnp.float32),
                   jax.ShapeDtypeStruct((kv, t, d), jnp.float32), jax.ShapeDtypeStruct((qh, 1, 128), jnp.float32)],
        in_specs=[qspec, prev, cur, prev, cur, pl.BlockSpec((grp, 1, 1), lambda h, n: (h, 0, 0)), qspec],
        out_specs=[qspec, kout, kout, pl.BlockSpec((grp, 1, 128), lambda h, n: (h, 0, 0))],
        scratch_shapes=[pltpu.VMEM((w, d), jnp.float32), pltpu.VMEM((w, d), jnp.float32),
                        pltpu.VMEM((2 * w, d), jnp.float32), pltpu.VMEM((2 * w, d), jnp.float32)],
        compiler_params=pltpu.CompilerParams(dimension_semantics=("parallel", "arbitrary")),
    )(q, k, k, v, v, sinks, do)


def _swa_split(z, positions, qh, kv, d):
    q, k, v = jnp.split(z, [qh * d, qh * d + kv * d], axis=-1)
    t = z.shape[0]
    q = _partial_rope(q.reshape(t, qh, d), positions).transpose(1, 0, 2)
    k = _partial_rope(k.reshape(t, kv, d), positions).transpose(1, 0, 2)
    v = v.reshape(t, kv, d).transpose(1, 0, 2)
    return q, k, v


@functools.partial(jax.custom_vjp, nondiff_argnums=(5,))
def _swa_block(h, positions, w_in, w_out, sinks, name):
    return _swa_block_fwd(h, positions, w_in, w_out, sinks, name)[0]


def _swa_block_fwd(h, positions, w_in, w_out, sinks, name):
    d = SWA_HEAD_DIM
    qh = h.shape[1] // d
    kv = qh // SWA_GROUP
    hb = h.astype(jnp.bfloat16)
    z = _matmul(hb, w_in, "nt", name + "_in")
    q, k, v = [a.astype(jnp.bfloat16) for a in _swa_split(z, positions, qh, kv, d)]
    o = _swa_fwd_call(q, k, v, sinks.reshape(qh, 1, 1), name + "_attn")
    a = _from_heads(o)
    return _matmul(a, w_out, "nn", name + "_out"), (hb, positions, z, q, k, v, a, w_in, w_out, sinks)


def _swa_block_bwd(name, saved, dy):
    hb, positions, z, q, k, v, a, w_in, w_out, sinks = saved
    d = SWA_HEAD_DIM
    qh = q.shape[0]
    kv = k.shape[0]
    dyb = dy.astype(jnp.bfloat16)
    da = _matmul(dyb, w_out, "nt", name + "_out_dx")
    dw_out = _matmul(a, dyb, "tn", name + "_out_dw", jnp.bfloat16)
    do = _to_heads(da, qh)
    dq, dk, dv, dsinks = _swa_bwd_call(q, k, v, sinks.reshape(qh, 1, 1), do, name + "_attn_bwd")
    _, split_vjp = jax.vjp(lambda zz: _swa_split(zz, positions, qh, kv, d), z)
    dz = split_vjp((dq, dk, dv))[0].astype(jnp.bfloat16)
    dh = _matmul(dz, w_in, "nn", name + "_in_dx")
    dw_in = _matmul(dz, hb, "tn", name + "_in_dw", jnp.bfloat16)
    return dh, None, dw_in, dw_out, dsinks[:, 0, 0].reshape(sinks.shape)


_swa_block.defvjp(_swa_block_fwd, _swa_block_bwd)


def _fox_specs(hp, nb, tb, d):
    blk = pl.BlockSpec((hp, 1, tb, d), lambda h, i: (h, i, 0, 0))
    whole = pl.BlockSpec((hp, nb, tb, d), lambda h, i: (h, 0, 0, 0))
    rowblk = pl.BlockSpec((hp, 1, 1, tb), lambda h, i: (h, i, 0, 0))
    rowwhole = pl.BlockSpec((hp, nb, 1, tb), lambda h, i: (h, 0, 0, 0))
    return blk, whole, rowblk, rowwhole


def _fox_fwd_call(q, k, v, g, bias, name):
    hh, nb, tb, d = q.shape
    scale = d ** -0.5

    hp = FOX_HEADS_PER_STEP

    def body(q_ref, k_ref, v_ref, g_ref, b_ref, og_ref, o_ref, lse_ref):
        i = pl.program_id(1)
        qs = [q_ref[a, 0] for a in range(hp)]
        rows = lax.broadcasted_iota(jnp.int32, (tb, tb), 0)
        cols = lax.broadcasted_iota(jnp.int32, (tb, tb), 1)

        def step(kb, carry, masked):
            out = []
            for a in range(hp):
                m, l, acc = carry[a]
                s = _dot(qs[a], k_ref[a, kb], _NT) * scale + b_ref[a, kb]
                if masked:
                    s = jnp.where(cols <= rows, s, NEG)
                m_new = jnp.maximum(m, jnp.max(s, axis=1, keepdims=True))
                alpha = jnp.exp(m - m_new)
                p = jnp.exp(s - m_new)
                out.append((m_new, alpha * l + jnp.sum(p, axis=1, keepdims=True),
                            alpha * acc + _dot(p, v_ref[a, kb], _NN)))
            return tuple(out)

        init = (jnp.full((tb, 1), NEG, jnp.float32), jnp.zeros((tb, 1), jnp.float32), jnp.zeros((tb, d), jnp.float32))
        carry = lax.fori_loop(0, i, lambda kb, c: step(kb, c, False), (init,) * hp)
        for a, (m, l, acc) in enumerate(step(i, carry, True)):
            o = acc / l
            o_ref[a, 0] = o
            og_ref[a, 0] = (o * _sigmoid(g_ref[a, 0])).astype(og_ref.dtype)
            lse_ref[a, 0] = _col_to_row(m + jnp.log(l))

    blk, whole, rowblk, rowwhole = _fox_specs(hp, nb, tb, d)
    return pl.pallas_call(
        body, name=name, grid=(hh // hp, nb),
        out_shape=[jax.ShapeDtypeStruct(q.shape, jnp.bfloat16), jax.ShapeDtypeStruct(q.shape, jnp.float32),
                   jax.ShapeDtypeStruct((hh, nb, 1, tb), jnp.float32)],
        in_specs=[blk, whole, whole, blk, rowwhole],
        out_specs=[blk, blk, rowblk],
    )(q, k, v, g, bias)


def _fox_dq_call(q, k, v, g, o, dog, bias, lse, name):
    hh, nb, tb, d = q.shape
    scale = d ** -0.5

    hp = FOX_HEADS_PER_STEP

    def body(q_ref, k_ref, v_ref, g_ref, o_ref, dog_ref, b_ref, lse_ref, dq_ref, dg_ref, do_ref, delta_ref):
        i = pl.program_id(1)
        rows = lax.broadcasted_iota(jnp.int32, (tb, tb), 0)
        cols = lax.broadcasted_iota(jnp.int32, (tb, tb), 1)
        qs, dos, deltas, lses = [], [], [], []
        for a in range(hp):
            sig = _sigmoid(g_ref[a, 0])
            ov, dogv = o_ref[a, 0], dog_ref[a, 0]
            do = dogv * sig
            dg_ref[a, 0] = dogv * ov * sig * (1.0 - sig)
            do_ref[a, 0] = do.astype(do_ref.dtype)
            qs.append(q_ref[a, 0])
            dos.append(do.astype(jnp.bfloat16))
            deltas.append(jnp.sum(do * ov, axis=1, keepdims=True))
            lses.append(_row_to_col(lse_ref[a, 0]))

        def step(kb, carry, masked):
            out = []
            for a in range(hp):
                dq, spdp, sp = carry[a]
                s = _dot(qs[a], k_ref[a, kb], _NT) * scale + b_ref[a, kb]
                p = jnp.exp(s - lses[a])
                if masked:
                    p = jnp.where(cols <= rows, p, 0.0)
                dp = _dot(dos[a], v_ref[a, kb], _NT)
                out.append((dq + _dot(p * (dp - deltas[a]), k_ref[a, kb], _NN),
                            spdp + jnp.sum(p * dp, axis=1, keepdims=True), sp + jnp.sum(p, axis=1, keepdims=True)))
            return tuple(out)

        zcol = jnp.zeros((tb, 1), jnp.float32)
        init = (jnp.zeros((tb, d), jnp.float32), zcol, zcol)
        carry = lax.fori_loop(0, i, lambda kb, c: step(kb, c, False), (init,) * hp)
        for a, (dq, spdp, sp) in enumerate(step(i, carry, True)):
            dq_ref[a, 0] = dq * scale
            delta_ref[a, 0] = _col_to_row(spdp / sp)

    blk, whole, rowblk, rowwhole = _fox_specs(hp, nb, tb, d)
    return pl.pallas_call(
        body, name=name, grid=(hh // hp, nb),
        out_shape=[jax.ShapeDtypeStruct(q.shape, jnp.float32), jax.ShapeDtypeStruct(q.shape, jnp.float32),
                   jax.ShapeDtypeStruct(q.shape, jnp.bfloat16), jax.ShapeDtypeStruct((hh, nb, 1, tb), jnp.float32)],
        in_specs=[blk, whole, whole, blk, blk, blk, rowwhole, rowblk],
        out_specs=[blk, blk, blk, rowblk],
    )(q, k, v, g, o, dog, bias, lse)


def _fox_dkv_call(q, k, v, do, bias, lse, delta, name):
    hh, nb, tb, d = q.shape
    scale = d ** -0.5

    hp = FOX_HEADS_PER_STEP

    def body(q_ref, k_ref, v_ref, do_ref, b_ref, lse_ref, delta_ref, dk_ref, dv_ref, db_ref):
        j = pl.program_id(1)
        ks = [k_ref[a, 0] for a in range(hp)]
        vs = [v_ref[a, 0] for a in range(hp)]
        bcols = [_row_to_col(b_ref[a, 0]) for a in range(hp)]
        rows = lax.broadcasted_iota(jnp.int32, (tb, tb), 0)
        cols = lax.broadcasted_iota(jnp.int32, (tb, tb), 1)

        def step(qb, carry, masked):
            out = []
            for a in range(hp):
                dk, dv, db = carry[a]
                st = _dot(ks[a], q_ref[a, qb], _NT) * scale + bcols[a] - lse_ref[a, qb]
                pt = jnp.exp(st)
                if masked:
                    pt = jnp.where(rows <= cols, pt, 0.0)
                dpt = _dot(vs[a], do_ref[a, qb], _NT)
                dst = pt * (dpt - delta_ref[a, qb])
                out.append((dk + _dot(dst, q_ref[a, qb], _NN), dv + _dot(pt, do_ref[a, qb], _NN),
                            db + jnp.sum(dst, axis=1, keepdims=True)))
            return tuple(out)

        init = (jnp.zeros((tb, d), jnp.float32), jnp.zeros((tb, d), jnp.float32), jnp.zeros((tb, 1), jnp.float32))
        carry = step(j, (init,) * hp, True)
        for a, (dk, dv, db) in enumerate(lax.fori_loop(j + 1, nb, lambda qb, c: step(qb, c, False), carry)):
            dk_ref[a, 0] = dk * scale
            dv_ref[a, 0] = dv
            db_ref[a, 0] = _col_to_row(db)

    blk, whole, rowblk, rowwhole = _fox_specs(hp, nb, tb, d)
    return pl.pallas_call(
        body, name=name, grid=(hh // hp, nb),
        out_shape=[jax.ShapeDtypeStruct(q.shape, jnp.float32), jax.ShapeDtypeStruct(q.shape, jnp.float32),
                   jax.ShapeDtypeStruct((hh, nb, 1, tb), jnp.float32)],
        in_specs=[whole, blk, blk, whole, rowblk, rowwhole, rowwhole],
        out_specs=[blk, blk, rowblk],
    )(q, k, v, do, bias, lse, delta)


def _fox_bias_fwd_call(fl, b_f, name):
    t, lanes = fl.shape
    tr = _tile(t, ROW_BLOCK)

    def body(fl_ref, b_ref, o_ref, carry):
        i = pl.program_id(0)

        @pl.when(i == 0)
        def _():
            carry[...] = jnp.zeros_like(carry)

        xv = fl_ref[...] + b_ref[...]
        ls = jnp.minimum(xv, 0.0) - jnp.log(1.0 + jnp.exp(-jnp.abs(xv)))
        pos = lax.broadcasted_iota(jnp.int32, (tr, lanes), 0)
        o_ref[...] = -(_group_cumsum(ls, tr, pos) + carry[...])
        carry[...] += jnp.sum(ls, axis=0, keepdims=True)

    blk = pl.BlockSpec((tr, lanes), lambda i: (i, 0))
    return pl.pallas_call(
        body, name=name, grid=(t // tr,), out_shape=jax.ShapeDtypeStruct((t, lanes), jnp.float32),
        in_specs=[blk, pl.BlockSpec((1, lanes), lambda i: (0, 0))], out_specs=blk,
        scratch_shapes=[pltpu.VMEM((1, lanes), jnp.float32)],
        compiler_params=pltpu.CompilerParams(dimension_semantics=("arbitrary",)),
    )(fl, b_f)


def _fox_bias_bwd_call(dbias, fl, b_f, name):
    t, lanes = fl.shape
    tr = _tile(t, ROW_BLOCK)
    last = t // tr - 1

    def body(db_ref, fl_ref, b_ref, dfl_ref, dbf_ref, carry):
        i = pl.program_id(0)

        @pl.when(i == 0)
        def _():
            carry[...] = jnp.zeros_like(carry)
            dbf_ref[...] = jnp.zeros_like(dbf_ref)

        dbv = db_ref[...]
        pos = lax.broadcasted_iota(jnp.int32, (tr, lanes), 0)
        dls = -(_group_rcumsum(dbv, tr, pos) + carry[...])
        carry[...] += jnp.sum(dbv, axis=0, keepdims=True)
        dfl = dls * _sigmoid(-(fl_ref[...] + b_ref[...]))
        dfl_ref[...] = dfl
        dbf_ref[...] += jnp.sum(dfl, axis=0, keepdims=True)

    blk = pl.BlockSpec((tr, lanes), lambda i: (last - i, 0))
    one = pl.BlockSpec((1, lanes), lambda i: (0, 0))
    return pl.pallas_call(
        body, name=name, grid=(t // tr,),
        out_shape=[jax.ShapeDtypeStruct((t, lanes), jnp.float32), jax.ShapeDtypeStruct((1, lanes), jnp.float32)],
        in_specs=[blk, blk, one], out_specs=[blk, one],
        scratch_shapes=[pltpu.VMEM((1, lanes), jnp.float32)],
        compiler_params=pltpu.CompilerParams(dimension_semantics=("arbitrary",)),
    )(dbias, fl, b_f)


@functools.partial(jax.custom_vjp, nondiff_argnums=(5,))
def _fox_block(h, w_qkvg, w_f, b_f, w_out, name):
    return _fox_block_fwd(h, w_qkvg, w_f, b_f, w_out, name)[0]


def _fox_block_fwd(h, w_qkvg, w_f, b_f, w_out, name):
    t, dm = h.shape
    d = FOX_HEAD_DIM
    heads = dm // d
    lanes = w_f.shape[0]
    tb = _tile(t, ATT_BLOCK)
    nb = t // tb
    hb = h.astype(jnp.bfloat16)
    z = _matmul(hb, w_qkvg, "nt", name + "_in")
    fl = _matmul(hb, w_f, "nt", name + "_inf")
    z4 = z.reshape(t, 4, heads, d).transpose(1, 2, 0, 3).reshape(4, heads, nb, tb, d)
    q, k, v = [z4[i].astype(jnp.bfloat16) for i in range(3)]
    g = z4[3]
    b_fp = jnp.pad(b_f, (0, lanes - heads)).reshape(1, lanes)
    bias = _fox_bias_fwd_call(fl, b_fp, name + "_bias")[:, :heads].T.reshape(heads, nb, 1, tb)
    og, o, lse = _fox_fwd_call(q, k, v, g, bias, name + "_attn")
    a = _from_heads(og.reshape(heads, t, d))
    return _matmul(a, w_out, "nn", name + "_out"), (hb, q, k, v, g, bias, o, lse, a, fl, b_fp, w_qkvg, w_f, w_out)


def _fox_block_bwd(name, saved, dy):
    hb, q, k, v, g, bias, o, lse, a, fl, b_fp, w_qkvg, w_f, w_out = saved
    heads, nb, tb, d = q.shape
    t = nb * tb
    lanes = w_f.shape[0]
    dyb = dy.astype(jnp.bfloat16)
    da = _matmul(dyb, w_out, "nt", name + "_out_dx")
    dw_out = _matmul(a, dyb, "tn", name + "_out_dw", jnp.bfloat16)
    dog = _to_heads(da, heads).reshape(heads, nb, tb, d)
    dq, dg, do, delta = _fox_dq_call(q, k, v, g, o, dog, bias, lse, name + "_attn_dq")
    dk, dv, dbias = _fox_dkv_call(q, k, v, do, bias, lse, delta, name + "_attn_dkv")
    dz = jnp.stack([dq, dk, dv, dg]).reshape(4, heads, t, d).transpose(2, 0, 1, 3).reshape(t, 4 * heads * d)
    dz = dz.astype(jnp.bfloat16)
    dbias_rows = jnp.pad(dbias.reshape(heads, t).T, ((0, 0), (0, lanes - heads)))
    dfl, db_fp = _fox_bias_bwd_call(dbias_rows, fl, b_fp, name + "_bias_bwd")
    dflb = dfl.astype(jnp.bfloat16)
    dh = _matmul(dz, w_qkvg, "nn", name + "_in_dx") + _matmul(dflb, w_f, "nn", name + "_inf_dx")
    dw_qkvg = _matmul(dz, hb, "tn", name + "_in_dw", jnp.bfloat16)
    dw_f = _matmul(dflb, hb, "tn", name + "_inf_dw", jnp.bfloat16)
    return dh, dw_qkvg, dw_f, db_fp[0, :heads], dw_out


_fox_block.defvjp(_fox_block_fwd, _fox_block_bwd)


def _group_cumsum(x, c, pos):
    sh = 1
    while sh < c:
        x = x + jnp.where(pos >= sh, pltpu.roll(x, sh, 0), 0.0)
        sh *= 2
    return x


def _group_rcumsum(x, c, pos):
    n = x.shape[0]
    sh = 1
    while sh < c:
        x = x + jnp.where(pos < c - sh, pltpu.roll(x, n - sh, 0), 0.0)
        sh *= 2
    return x


def _hgrn_prep(zq_ref, zf_ref, zi_ref, lb_ref, b_scr, rows, c):
    nb = rows // c
    zq, zf = zq_ref[...], zf_ref[...]
    lbv = lb_ref[...]
    sq = _sigmoid(zq)
    sf = _sigmoid(zf)
    f = lbv + (1.0 - lbv) * sf
    pos = lax.broadcasted_iota(jnp.int32, (rows, 128), 0) % c
    b = _group_cumsum(jnp.log(f), c, pos)
    sh = (nb, c, 128)
    q3, k3, v3, b3 = (zq * sq).reshape(sh), (1.0 - f).reshape(sh), zi_ref[...].reshape(sh), b.reshape(sh)
    b_scr[...] = b3
    glast = b_scr[:, c - 1:c, :]
    eb = jnp.exp(b3)
    ek = jnp.exp(glast - b3)
    return dict(zq=zq, sq=sq, sf=sf, f=f, pos=pos, q3=q3, k3=k3, v3=v3, b3=b3, eb=eb, ek=ek,
                qt=q3 * eb, kh=k3 * ek, dec=jnp.exp(glast))


def _hgrn_fwd_call(z, lb, norm_g, name):
    t, w4 = z.shape
    w = w4 // 4
    heads = w // HGRN_EXPAND
    rows, c = _tile(t, HGRN_ROWS), HGRN_SUB
    nb, ntb = rows // c, t // rows

    def body(zq_ref, zf_ref, zi_ref, zg_ref, lb_ref, ng_ref, og_ref, o_ref, ck_ref, st_ref, b_scr, k_scr, v_scr):
        tb = pl.program_id(1)

        @pl.when(tb == 0)
        def _():
            st_ref[...] = jnp.zeros_like(st_ref)

        ck_ref[0, 0] = st_ref[...]
        p = _hgrn_prep(zq_ref, zf_ref, zi_ref, lb_ref, b_scr, rows, c)
        k_scr[...] = p["k3"]
        v_scr[...] = p["v3"]
        tpos = lax.broadcasted_iota(jnp.int32, (nb, c, 128), 1)
        od = jnp.zeros((nb, c, 128), jnp.float32)
        for s in range(c):
            bs, ks, vs = b_scr[:, s:s + 1, :], k_scr[:, s:s + 1, :], v_scr[:, s:s + 1, :]
            e = jnp.exp(jnp.minimum(p["b3"] - bs, 0.0))
            a = jnp.sum(jnp.where(tpos >= s, p["q3"] * ks * e, 0.0), axis=2, keepdims=True)
            od = od + a * vs
        st = st_ref[...]
        for i in range(nb):
            o_ref[pl.ds(i * c, c), :] = _dot(p["qt"][i], st, _NT) + od[i]
            st = st * p["dec"][i] + _dot(p["v3"][i], p["kh"][i], _TN)
        st_ref[...] = st
        o = o_ref[...]
        rstd = lax.rsqrt(jnp.mean(o * o, axis=1, keepdims=True) + RMS_EPS)
        zg = zg_ref[...]
        og_ref[...] = (o * rstd * ng_ref[...] * (zg * _sigmoid(zg))).astype(og_ref.dtype)

    def col(off):
        return pl.BlockSpec((rows, 128), lambda h, tb: (tb, h + off))

    out = pl.BlockSpec((rows, 128), lambda h, tb: (tb, h))
    return pl.pallas_call(
        body, name=name, grid=(heads, ntb),
        out_shape=[jax.ShapeDtypeStruct((t, w), jnp.bfloat16), jax.ShapeDtypeStruct((t, w), jnp.float32),
                   jax.ShapeDtypeStruct((heads, ntb, 128, 128), jnp.float32)],
        in_specs=[col(0), col(heads), col(2 * heads), col(3 * heads), pl.BlockSpec((1, 128), lambda h, tb: (0, h)),
                  pl.BlockSpec((1, 128), lambda h, tb: (0, 0))],
        out_specs=[out, out, pl.BlockSpec((1, 1, 128, 128), lambda h, tb: (h, tb, 0, 0))],
        scratch_shapes=[pltpu.VMEM((128, 128), jnp.float32)] + [pltpu.VMEM((nb, c, 128), jnp.float32)] * 3,
        compiler_params=pltpu.CompilerParams(dimension_semantics=("parallel", "arbitrary")),
    )(z, z, z, z, lb, norm_g)


def _hgrn_bwd_call(z, lb, norm_g, o_raw, dog, ck, name):
    t, w4 = z.shape
    w = w4 // 4
    heads = w // HGRN_EXPAND
    rows, c = _tile(t, HGRN_ROWS), HGRN_SUB
    nb, ntb = rows // c, t // rows

    def body(zq_ref, zf_ref, zi_ref, zg_ref, lb_ref, ng_ref, o_ref, dog_ref, ck_ref,
             dzq_ref, dzf_ref, dzi_ref, dzg_ref, dlb_ref, dng_ref,
             dst_ref, s_store, b_scr, k_scr, v_scr, dqt_scr, dkh_scr, dv_scr, ddec_scr):
        tb = pl.program_id(1)

        @pl.when(tb == 0)
        def _():
            dst_ref[...] = jnp.zeros_like(dst_ref)
            dlb_ref[...] = jnp.zeros_like(dlb_ref)
            dng_ref[...] = jnp.zeros_like(dng_ref)

        p = _hgrn_prep(zq_ref, zf_ref, zi_ref, lb_ref, b_scr, rows, c)
        q3, k3, v3, b3 = p["q3"], p["k3"], p["v3"], p["b3"]

        o = o_ref[...]
        rstd = lax.rsqrt(jnp.mean(o * o, axis=1, keepdims=True) + RMS_EPS)
        xh = o * rstd
        zg = zg_ref[...]
        sg = _sigmoid(zg)
        ng = ng_ref[...]
        dogv = dog_ref[...]
        don = dogv * (zg * sg)
        dzg_ref[...] = (dogv * (xh * ng) * (sg * (1.0 + zg * (1.0 - sg)))).astype(dzg_ref.dtype)
        dng_ref[0] += jnp.sum(don * xh, axis=0, keepdims=True)
        dyg = don * ng
        do = rstd * (dyg - xh * jnp.mean(dyg * xh, axis=1, keepdims=True))
        do3 = do.reshape(nb, c, 128)

        st = ck_ref[0, 0]
        for i in range(nb):
            s_store[i] = st
            st = st * p["dec"][i] + _dot(v3[i], p["kh"][i], _TN)

        dst = dst_ref[...]
        for i in reversed(range(nb)):
            sl = pl.ds(i * c, c)
            st_i = s_store[i]
            dqt_scr[sl, :] = _dot(do3[i], st_i, _NN)
            dv_scr[sl, :] = _dot(p["kh"][i], dst, _NT)
            dkh_scr[sl, :] = _dot(v3[i], dst, _NN)
            ddec_scr[i] = jnp.broadcast_to(jnp.sum(st_i * dst, axis=0, keepdims=True), (8, 128))
            dst = dst * p["dec"][i] + _dot(do3[i], p["qt"][i], _TN)
        dst_ref[...] = dst

        k_scr[...] = k3
        v_scr[...] = v3
        tpos = lax.broadcasted_iota(jnp.int32, (nb, c, 128), 1)
        zero = jnp.zeros((nb, c, 128), jnp.float32)
        dqd, dkd, dvd, dbd = zero, zero, zero, zero
        for s in range(c):
            bs, ks, vs = b_scr[:, s:s + 1, :], k_scr[:, s:s + 1, :], v_scr[:, s:s + 1, :]
            e = jnp.where(tpos >= s, jnp.exp(jnp.minimum(b3 - bs, 0.0)), 0.0)
            qe = q3 * e
            a = jnp.sum(qe * ks, axis=2, keepdims=True)
            da = jnp.sum(do3 * vs, axis=2, keepdims=True)
            y = da * (ks * e)
            dqd = dqd + y
            dbd = dbd + y * q3
            dks = jnp.sum(da * qe, axis=1, keepdims=True)
            dvs = jnp.sum(a * do3, axis=1, keepdims=True)
            at_s = tpos == s
            dkd = dkd + jnp.where(at_s, dks, 0.0)
            dvd = dvd + jnp.where(at_s, dvs, 0.0)
            dbd = dbd - jnp.where(at_s, dks * ks, 0.0)

        dqt3 = dqt_scr[...].reshape(nb, c, 128)
        dkh3 = dkh_scr[...].reshape(nb, c, 128)
        ddec = ddec_scr[:, 0:1, :]
        dq_act = dqt3 * p["eb"] + dqd
        dk = dkh3 * p["ek"] + dkd
        khd = p["kh"] * dkh3
        db = p["qt"] * dqt3 - khd + dbd
        dglast = jnp.sum(khd, axis=1, keepdims=True) + ddec * p["dec"]
        db = db + jnp.where(tpos == c - 1, dglast, 0.0)
        dlogf = _group_rcumsum(db.reshape(rows, 128), c, p["pos"])

        zq, sq, sf, f = p["zq"], p["sq"], p["sf"], p["f"]
        lbv = lb_ref[...]
        dzq_ref[...] = (dq_act.reshape(rows, 128) * (sq * (1.0 + zq * (1.0 - sq)))).astype(dzq_ref.dtype)
        df = dlogf / f - dk.reshape(rows, 128)
        dzf_ref[...] = (df * (1.0 - lbv) * (sf * (1.0 - sf))).astype(dzf_ref.dtype)
        dlb_ref[...] += jnp.sum(df * (1.0 - sf), axis=0, keepdims=True)
        dzi_ref[...] = (dv_scr[...] + dvd.reshape(rows, 128)).astype(dzi_ref.dtype)

    last = ntb - 1

    def col(off):
        return pl.BlockSpec((rows, 128), lambda h, tb: (last - tb, h + off))

    out = pl.BlockSpec((rows, 128), lambda h, tb: (last - tb, h))
    lbs = pl.BlockSpec((1, 128), lambda h, tb: (0, h))
    big = pltpu.VMEM((rows, 128), jnp.float32)
    return pl.pallas_call(
        body, name=name, grid=(heads, ntb),
        out_shape=[jax.ShapeDtypeStruct((t, w), jnp.bfloat16)] * 4
        + [jax.ShapeDtypeStruct((1, w), jnp.float32), jax.ShapeDtypeStruct((heads, 1, 128), jnp.float32)],
        in_specs=[col(0), col(heads), col(2 * heads), col(3 * heads), lbs, pl.BlockSpec((1, 128), lambda h, tb: (0, 0)),
                  out, out, pl.BlockSpec((1, 1, 128, 128), lambda h, tb: (h, last - tb, 0, 0))],
        out_specs=[out, out, out, out, lbs, pl.BlockSpec((1, 1, 128), lambda h, tb: (h, 0, 0))],
        scratch_shapes=[pltpu.VMEM((128, 128), jnp.float32), pltpu.VMEM((nb, 128, 128), jnp.float32)]
        + [pltpu.VMEM((nb, c, 128), jnp.float32)] * 3 + [big] * 3 + [pltpu.VMEM((nb, 8, 128), jnp.float32)],
        compiler_params=pltpu.CompilerParams(dimension_semantics=("parallel", "arbitrary")),
    )(z, z, z, z, lb, norm_g, o_raw, dog, ck)


@functools.partial(jax.custom_vjp, nondiff_argnums=(5,))
def _hgrn_block(h, w_in, w_out, norm_g, lb, name):
    return _hgrn_block_fwd(h, w_in, w_out, norm_g, lb, name)[0]


def _hgrn_block_fwd(h, w_in, w_out, norm_g, lb, name):
    hb = h.astype(jnp.bfloat16)
    z = _matmul(hb, w_in, "nn", name + "_in")
    og, o_raw, ck = _hgrn_fwd_call(z, lb.reshape(1, -1), norm_g.reshape(1, -1), name + "_scan")
    return _matmul(og, w_out, "nn", name + "_out"), (hb, z, og, o_raw, ck, w_in, w_out, norm_g, lb)


def _hgrn_block_bwd(name, saved, dy):
    hb, z, og, o_raw, ck, w_in, w_out, norm_g, lb = saved
    dyb = dy.astype(jnp.bfloat16)
    dog = _matmul(dyb, w_out, "nt", name + "_out_dx")
    dw_out = _matmul(og, dyb, "tn", name + "_out_dw", jnp.bfloat16)
    dzq, dzf, dzi, dzg, dlb, dng = _hgrn_bwd_call(z, lb.reshape(1, -1), norm_g.reshape(1, -1), o_raw, dog, ck,
                                                  name + "_scan_bwd")
    dz = jnp.concatenate([dzq, dzf, dzi, dzg], axis=1)
    dh = _matmul(dz, w_in, "nt", name + "_in_dx")
    dw_in = _matmul(hb, dz, "tn", name + "_in_dw", jnp.bfloat16)
    return dh, dw_in, dw_out, jnp.sum(dng, axis=0).reshape(norm_g.shape), dlb.reshape(lb.shape)


_hgrn_block.defvjp(_hgrn_block_fwd, _hgrn_block_bwd)


def _layer_plan(sharded):
    mixers = [
        [("hgrn_w_in", "col"), ("hgrn_w_out", "row")],
        [("swa_w_in", "colT"), ("swa_w_out", "row")],
        [("sc_w_in", "col"), ("sc_conv_w", "exact"), ("sc_w_out", "row")],
        [("fox_w_in", "colT"), ("fox_w_out", "row")],
    ]
    depth = sharded["ffn_w_up"].shape[0]
    plan = []
    for i in range(depth):
        m, j = i % len(mixers), i // len(mixers)
        entries = [(n, j, kind) for n, kind in mixers[m]]
        entries += [("ffn_w_up", i, "col"), ("ffn_conv_w", i, "exact"), ("ffn_w_down", i, "row")]
        plan.append(entries)
    return plan


def _local_loss(sharded, repl, x, positions, loss_target):
    x = x[0]
    loss_target = loss_target[0]
    d = x.shape[-1]
    plan = _layer_plan(sharded)
    lb_table = jnp.cumsum(jax.nn.softmax(repl["hgrn_lb_param"], axis=0), axis=0)
    for i, entries in enumerate(plan):
        shards = tuple(sharded[n][j] for n, j, _ in entries)
        kinds = tuple(kind for _, _, kind in entries)
        full = _make_layer_gather([s.shape for s in shards], kinds, "layer%d" % i)(shards)
        wts = {n: f for (n, _, _), f in zip(entries, full)}
        m, j = i % 4, i // 4
        tag = "l%d" % i
        hn = _norm(x, repl["mix_pre_g"][i], tag + "_mix_pre")
        if m == 0:
            y = _hgrn_block(hn, wts["hgrn_w_in"], wts["hgrn_w_out"], repl["hgrn_norm_g"][j], lb_table[i], tag + "_hgrn")
        elif m == 1:
            y = _swa_block(hn, positions, wts["swa_w_in"], wts["swa_w_out"], repl["swa_sinks"][j], tag + "_swa")
        elif m == 2:
            y = _sconv_block(hn, wts["sc_w_in"], wts["sc_conv_w"], wts["sc_w_out"], tag + "_sc")
        else:
            w_in = wts["fox_w_in"]
            heads = d // FOX_HEAD_DIM
            w_qkvg = jnp.concatenate([w_in[:3 * d], w_in[3 * d + heads:]], axis=0)
            w_f = jnp.pad(w_in[3 * d:3 * d + heads], ((0, 128 - heads), (0, 0)))
            y = _fox_block(hn, w_qkvg, w_f, repl["fox_b_f"][j], wts["fox_w_out"], tag + "_fox")
        x = _norm_add(y, repl["mix_post_g"][i], x, tag + "_mix_post")
        hn = _norm(x, repl["ffn_pre_g"][i], tag + "_ffn_pre")
        y = _ffn_block(hn, wts["ffn_w_up"], wts["ffn_conv_w"], repl["ffn_conv_b"][i], wts["ffn_w_down"], tag + "_ffn")
        x = _norm_add(y, repl["ffn_post_g"][i], x, tag + "_ffn_post")
    err = jnp.square(x - loss_target)
    return 0.5 * jnp.sum(jnp.mean(err, axis=-1))


SHARDED = ["hgrn_w_in", "hgrn_w_out", "swa_w_in", "swa_w_out", "sc_w_in", "sc_conv_w", "sc_w_out",
           "fox_w_in", "fox_w_out", "ffn_w_up", "ffn_conv_w", "ffn_w_down"]
REPLICATED = ["mix_pre_g", "mix_post_g", "ffn_pre_g", "ffn_post_g", "hgrn_norm_g", "hgrn_lb_param",
              "swa_sinks", "fox_b_f", "ffn_conv_b"]
WEIGHTS = ["mix_pre_g", "mix_post_g", "ffn_pre_g", "ffn_post_g", "hgrn_w_in", "hgrn_w_out", "hgrn_norm_g",
           "hgrn_lb_param", "swa_w_in", "swa_w_out", "swa_sinks", "sc_w_in", "sc_conv_w", "sc_w_out",
           "fox_w_in", "fox_b_f", "fox_w_out", "ffn_w_up", "ffn_conv_w", "ffn_conv_b", "ffn_w_down"]


def _sum_replicated(loss, grads):
    parts = [loss.reshape(1)] + [grads[n].reshape(-1) for n in REPLICATED]
    flat = jnp.concatenate(parts)
    n = flat.shape[0]
    cols = 1024
    rows = -(-n // cols)
    rows += (-rows) % 8
    flat = jnp.pad(flat, (0, rows * cols - n)).reshape(rows, cols)
    total = _sum_blocks(_all_gather(flat, "small_gather"), "small_sum").reshape(-1)
    out, off = {}, 1
    for name in REPLICATED:
        size = grads[name].size
        out[name] = total[off:off + size].reshape(grads[name].shape)
        off += size
    return total[0], out


def kernel(x, positions, mix_pre_g, mix_post_g, ffn_pre_g, ffn_post_g, hgrn_w_in, hgrn_w_out, hgrn_norm_g, hgrn_lb_param, swa_w_in, swa_w_out, swa_sinks, sc_w_in, sc_conv_w, sc_w_out, fox_w_in, fox_b_f, fox_w_out, ffn_w_up, ffn_conv_w, ffn_conv_b, ffn_w_down, loss_target, m_mix_pre_g, m_mix_post_g, m_ffn_pre_g, m_ffn_post_g, m_hgrn_w_in, m_hgrn_w_out, m_hgrn_norm_g, m_hgrn_lb_param, m_swa_w_in, m_swa_w_out, m_swa_sinks, m_sc_w_in, m_sc_conv_w, m_sc_w_out, m_fox_w_in, m_fox_b_f, m_fox_w_out, m_ffn_w_up, m_ffn_conv_w, m_ffn_conv_b, m_ffn_w_down, v_mix_pre_g, v_mix_post_g, v_ffn_pre_g, v_ffn_post_g, v_hgrn_w_in, v_hgrn_w_out, v_hgrn_norm_g, v_hgrn_lb_param, v_swa_w_in, v_swa_w_out, v_swa_sinks, v_sc_w_in, v_sc_conv_w, v_sc_w_out, v_fox_w_in, v_fox_b_f, v_fox_w_out, v_ffn_w_up, v_ffn_conv_w, v_ffn_conv_b, v_ffn_w_down):
    given = dict(locals())
    sharded = {n: given[n] for n in SHARDED}
    repl = {n: given[n] for n in REPLICATED}
    loss, (g_sharded, g_repl, grad_x) = jax.value_and_grad(_local_loss, argnums=(0, 1, 2))(
        sharded, repl, x, positions, loss_target)
    loss, g_repl = _sum_replicated(loss, g_repl)
    grads = {**g_sharded, **g_repl}
    delta, new_m, new_v = {}, {}, {}
    for n in WEIGHTS:
        delta[n], new_m[n], new_v[n] = _adamw(given[n], grads[n], given["m_" + n], given["v_" + n], "adamw_" + n)
    return (loss, grad_x, *[grads[n] for n in WEIGHTS], *[delta[n] for n in WEIGHTS],
            *[new_m[n] for n in WEIGHTS], *[new_v[n] for n in WEIGHTS])
```

```python
import functools
import math

import jax
import jax.numpy as jnp
from jax import lax
from jax.experimental import pallas as pl
from jax.experimental.pallas import tpu as pltpu

N_DEV = 8
MESH = pl.DeviceIdType.MESH
HBM_SPEC = pl.BlockSpec(memory_space=pltpu.HBM)
PACK_COLS = 512
PACK_ROW_BLOCK = 512
PACK_TILE_ROWS = 16

MM_TILE = 1024
MM_K_SINGLE = 2048
MM_K_TILE = 1536
MM_VMEM_BYTES = 48 * 1024 * 1024
ROW_BLOCK = 256
COL_BLOCK = 512
HALO = 8
ATT_BLOCK = 256
FOX_HEADS_PER_STEP = 2
HGRN_SUB = 16
HGRN_ROWS = 256
NEG = -1e30

RMS_EPS = 1e-6
HGRN_EXPAND = 128
SWA_HEAD_DIM = 64
SWA_GROUP = 8
SWA_WINDOW = 128
FOX_HEAD_DIM = 64
ROPE_THETA = 500000.0
ROT_DIM = SWA_HEAD_DIM // 4

ADAM_LR = 0.001
ADAM_B1 = 0.9
ADAM_B2 = 0.999
ADAM_EPS = 1e-08
ADAM_WD = 0.01
ADAM_STEP = 10

_NT = (((1,), (1,)), ((), ()))
_NN = (((1,), (0,)), ((), ()))
_TN = (((0,), (0,)), ((), ()))


def _tile(dim, pref):
    if dim <= pref:
        return dim
    t = pref - pref % 128
    while t >= 128:
        if dim % t == 0:
            return t
        t -= 128
    return dim


def _dot(a, b, dims):
    return lax.dot_general(a.astype(jnp.bfloat16), b.astype(jnp.bfloat16), dims, preferred_element_type=jnp.float32)


def _sigmoid(x):
    return 1.0 / (1.0 + jnp.exp(-x))


def _col_to_row(c):
    n = c.shape[0]
    eye = lax.broadcasted_iota(jnp.int32, (n, n), 0) == lax.broadcasted_iota(jnp.int32, (n, n), 1)
    return jnp.sum(jnp.where(eye, c, 0.0), axis=0, keepdims=True)


def _row_to_col(r):
    n = r.shape[1]
    eye = lax.broadcasted_iota(jnp.int32, (n, n), 0) == lax.broadcasted_iota(jnp.int32, (n, n), 1)
    return jnp.sum(jnp.where(eye, r, 0.0), axis=1, keepdims=True)


def _all_gather(x, name):
    r, c = x.shape

    def body(x_ref, out_ref, send_sems, recv_sems, local_sem):
        mx, my, mc = lax.axis_index("x"), lax.axis_index("y"), lax.axis_index("c")
        me, sibling = (mx, my, mc), (mx, my, 1 - mc)
        chips = [(1 - mx, my), (mx, 1 - my), (1 - mx, 1 - my)]

        def rows(px, py, pc):
            return out_ref.at[4 * px + 2 * py + pc]

        def copy(k, block, to, src=None):
            return pltpu.make_async_remote_copy(
                src_ref=rows(*block) if src is None else src, dst_ref=rows(*block),
                send_sem=send_sems.at[k], recv_sem=recv_sems.at[k], device_id=to, device_id_type=MESH)

        mine = pltpu.make_async_copy(x_ref, rows(*me), local_sem)
        mine.start()
        first = [copy(0, me, sibling, src=x_ref)]
        first += [copy(1 + j, me, (*chip, mc), src=x_ref) for j, chip in enumerate(chips)]
        for cp in first:
            cp.start()
        passed = [copy(4 + j, (*chip, mc), sibling) for j, chip in enumerate(chips)]
        for j, chip in enumerate(chips):
            copy(1 + j, (*chip, mc), me).wait_recv()
            passed[j].start()
        copy(0, sibling, me).wait_recv()
        for j, chip in enumerate(chips):
            copy(4 + j, (*chip, 1 - mc), me).wait_recv()
        for cp in first + passed:
            cp.wait_send()
        mine.wait()

    return pl.pallas_call(
        body, name=name, out_shape=jax.ShapeDtypeStruct((N_DEV, r, c), x.dtype),
        in_specs=[HBM_SPEC], out_specs=HBM_SPEC,
        scratch_shapes=[pltpu.SemaphoreType.DMA((7,)), pltpu.SemaphoreType.DMA((7,)), pltpu.SemaphoreType.DMA],
    )(x)


def _pair_exchange(x, name):
    n = x.shape[0] // 2

    def body(x_ref, out_ref, send_sems, recv_sems):
        mc = lax.axis_index("c")
        sibling = (lax.axis_index("x"), lax.axis_index("y"), 1 - mc)
        copies = [pltpu.make_async_remote_copy(
            src_ref=x_ref.at[2 * j + 1 - mc], dst_ref=out_ref.at[j], send_sem=send_sems.at[j],
            recv_sem=recv_sems.at[j], device_id=sibling, device_id_type=MESH) for j in range(n)]
        for cp in copies:
            cp.start()
        for cp in copies:
            cp.wait()

    return pl.pallas_call(
        body, name=name, out_shape=jax.ShapeDtypeStruct((n,) + x.shape[1:], x.dtype),
        in_specs=[HBM_SPEC], out_specs=HBM_SPEC,
        scratch_shapes=[pltpu.SemaphoreType.DMA((n,)), pltpu.SemaphoreType.DMA((n,))],
    )(x)


def _chip_exchange(x, name):
    flips = [(1, 0), (0, 1), (1, 1)]

    def body(x_ref, out_ref, send_sems, recv_sems, local_sem):
        mx, my, mc = lax.axis_index("x"), lax.axis_index("y"), lax.axis_index("c")
        me = 2 * mx + my

        def chip(flip):
            return ((1 - mx) if flip[0] else mx, (1 - my) if flip[1] else my)

        def copy(k, dst_block):
            px, py = chip(flips[k])
            return pltpu.make_async_remote_copy(
                src_ref=x_ref.at[2 * px + py], dst_ref=out_ref.at[dst_block],
                send_sem=send_sems.at[k], recv_sem=recv_sems.at[k], device_id=(px, py, mc), device_id_type=MESH)

        mine = pltpu.make_async_copy(x_ref.at[me], out_ref.at[me], local_sem)
        mine.start()
        sends = [copy(k, me) for k in range(len(flips))]
        for cp in sends:
            cp.start()
        for k in range(len(flips)):
            px, py = chip(flips[k])
            copy(k, 2 * px + py).wait_recv()
        for cp in sends:
            cp.wait_send()
        mine.wait()

    return pl.pallas_call(
        body, name=name, out_shape=jax.ShapeDtypeStruct(x.shape, x.dtype),
        in_specs=[HBM_SPEC], out_specs=HBM_SPEC,
        scratch_shapes=[pltpu.SemaphoreType.DMA((3,)), pltpu.SemaphoreType.DMA((3,)), pltpu.SemaphoreType.DMA],
    )(x)


def _sum_blocks(x, name, out_dtype=jnp.float32):
    n, r, c = x.shape
    tr = PACK_ROW_BLOCK if r % PACK_ROW_BLOCK == 0 else r

    def body(x_ref, o_ref):
        acc = x_ref[0].astype(jnp.float32)
        for j in range(1, n):
            acc = acc + x_ref[j].astype(jnp.float32)
        o_ref[...] = acc.astype(o_ref.dtype)

    return pl.pallas_call(
        body, name=name, grid=(r // tr,), out_shape=jax.ShapeDtypeStruct((r, c), out_dtype),
        in_specs=[pl.BlockSpec((n, tr, c), lambda i: (0, i, 0))],
        out_specs=pl.BlockSpec((tr, c), lambda i: (i, 0)),
    )(x)


def _pair_sum(x, got, name):
    n, _, r, c = x.shape
    tr = PACK_ROW_BLOCK if r % PACK_ROW_BLOCK == 0 else r

    def body(x_ref, g_ref, o_ref):
        mine = jnp.where(lax.axis_index("c") == 0, x_ref[0, 0], x_ref[0, 1])
        o_ref[0] = (mine.astype(jnp.float32) + g_ref[0].astype(jnp.float32)).astype(o_ref.dtype)

    spec = pl.BlockSpec((1, tr, c), lambda j, i: (j, i, 0))
    return pl.pallas_call(
        body, name=name, grid=(n, r // tr), out_shape=jax.ShapeDtypeStruct(got.shape, got.dtype),
        in_specs=[pl.BlockSpec((1, 2, tr, c), lambda j, i: (j, 0, i, 0)), spec], out_specs=spec,
    )(x, got)


def _reduce_scatter(x, name):
    _, r, c = x.shape
    paired = _pair_sum(x.reshape(4, 2, r, c), _pair_exchange(x, name + "_pair"), name + "_pair_sum")
    return _sum_blocks(_chip_exchange(paired, name + "_chips"), name + "_sum")


def _matmul(a, b, mode, name, out_dtype=jnp.float32):
    if mode == "nn":
        (m, k), (_, n) = a.shape, b.shape
    elif mode == "nt":
        (m, k), (n, _) = a.shape, b.shape
    else:
        (k, m), (_, n) = a.shape, b.shape
    tm, tn = _tile(m, MM_TILE), _tile(n, MM_TILE)
    tk = k if k <= MM_K_SINGLE else _tile(k, MM_K_TILE)
    nk = k // tk
    dims = {"nn": _NN, "nt": _NT, "tn": _TN}[mode]
    rows_inner = nk == 1 and (n // tn) * a.size < (m // tm) * b.size

    def body(a_ref, b_ref, o_ref, *scratch):
        part = lax.dot_general(a_ref[...], b_ref[...], dims, preferred_element_type=jnp.float32)
        if nk == 1:
            o_ref[...] = part.astype(o_ref.dtype)
            return
        acc_ref, = scratch
        kk = pl.program_id(2)

        @pl.when(kk == 0)
        def _():
            acc_ref[...] = part

        @pl.when(kk > 0)
        def _():
            acc_ref[...] += part

        @pl.when(kk == nk - 1)
        def _():
            o_ref[...] = acc_ref[...].astype(o_ref.dtype)

    def ij(g0, g1):
        return (g1, g0) if rows_inner else (g0, g1)

    def a_map(g0, g1, kk):
        i, _ = ij(g0, g1)
        return (kk, i) if mode == "tn" else (i, kk)

    def b_map(g0, g1, kk):
        _, j = ij(g0, g1)
        return (j, kk) if mode == "nt" else (kk, j)

    def o_map(g0, g1, kk):
        return ij(g0, g1)

    grid = (n // tn, m // tm, nk) if rows_inner else (m // tm, n // tn, nk)
    return pl.pallas_call(
        body, name=name, grid=grid, out_shape=jax.ShapeDtypeStruct((m, n), out_dtype),
        in_specs=[pl.BlockSpec((tk, tm) if mode == "tn" else (tm, tk), a_map),
                  pl.BlockSpec((tn, tk) if mode == "nt" else (tk, tn), b_map)],
        out_specs=pl.BlockSpec((tm, tn), o_map),
        scratch_shapes=[pltpu.VMEM((tm, tn), jnp.float32)] if nk > 1 else [],
        compiler_params=pltpu.CompilerParams(dimension_semantics=("parallel", "parallel", "arbitrary"),
                                             vmem_limit_bytes=MM_VMEM_BYTES),
    )(a, b)


def _piece_rows(shape, kind, gathering):
    size = math.prod(shape) * (2 if kind == "exact" and gathering else 1)
    rows = -(-size // PACK_COLS)
    return rows + (-rows) % PACK_TILE_ROWS


def _as_rows(x, rows, lead):
    n = x.shape[-1]
    if n != rows * PACK_COLS:
        x = jnp.pad(x, [(0, 0)] * len(lead) + [(0, rows * PACK_COLS - n)])
    return x.reshape(tuple(lead) + (rows, PACK_COLS))


def _stack_rows(parts, axis):
    total = sum(p.shape[axis] for p in parts)
    fill = (-total) % PACK_ROW_BLOCK
    if fill:
        shape = list(parts[0].shape)
        shape[axis] = fill
        parts = parts + [jnp.zeros(shape, parts[0].dtype)]
    return jnp.concatenate(parts, axis=axis)


def _gather_pack(shards, kinds):
    parts = []
    for s, kind in zip(shards, kinds):
        if kind == "exact":
            flat = lax.bitcast_convert_type(s, jnp.bfloat16).reshape(-1)
        elif kind == "colT":
            flat = s.T.astype(jnp.bfloat16).reshape(-1)
        else:
            flat = s.astype(jnp.bfloat16).reshape(-1)
        parts.append(_as_rows(flat, _piece_rows(s.shape, kind, True), ()))
    return _stack_rows(parts, 0)


def _gather_unpack(gathered, shapes, kinds):
    out, off = [], 0
    for shape, kind in zip(shapes, kinds):
        size = math.prod(shape) * (2 if kind == "exact" else 1)
        rows = _piece_rows(shape, kind, True)
        piece = gathered[:, off:off + rows]
        off += rows
        if size != rows * PACK_COLS:
            piece = piece.reshape(N_DEV, rows * PACK_COLS)[:, :size]
        if kind == "colT":
            out.append(piece.reshape(N_DEV * shape[1], shape[0]))
            continue
        if kind == "exact":
            piece = lax.bitcast_convert_type(piece.reshape((N_DEV,) + tuple(shape) + (2,)), jnp.float32)
            kind = "col"
        else:
            piece = piece.reshape((N_DEV,) + tuple(shape))
        if kind == "col":
            piece = jnp.moveaxis(piece, 0, -2)
            piece = piece.reshape(piece.shape[:-2] + (N_DEV * shape[-1],))
        else:
            piece = piece.reshape((N_DEV * shape[0],) + tuple(shape[1:]))
        out.append(piece)
    return tuple(out)


def _scatter_pack(cts, shapes, kinds):
    parts = []
    for ct, shape, kind in zip(cts, shapes, kinds):
        ct = ct.astype(jnp.bfloat16)
        if kind not in ("row", "colT"):
            ct = jnp.moveaxis(ct.reshape(tuple(shape[:-1]) + (N_DEV, shape[-1])), -2, 0)
        rows = _piece_rows(shape, kind, False)
        if math.prod(shape) == rows * PACK_COLS:
            parts.append(ct.reshape(N_DEV, rows, PACK_COLS))
        else:
            parts.append(_as_rows(ct.reshape(N_DEV, -1), rows, (N_DEV,)))
    return _stack_rows(parts, 1)


def _scatter_unpack(summed, shapes, kinds):
    out, off = [], 0
    for shape, kind in zip(shapes, kinds):
        size = math.prod(shape)
        rows = _piece_rows(shape, kind, False)
        piece = summed[off:off + rows]
        off += rows
        if size != rows * PACK_COLS:
            piece = piece.reshape(-1)[:size]
        out.append(piece.reshape(shape[1], shape[0]).T if kind == "colT" else piece.reshape(shape))
    return tuple(out)


def _make_layer_gather(shapes, kinds, tag):
    shapes = tuple(tuple(s) for s in shapes)

    def impl(shards):
        return _gather_unpack(_all_gather(_gather_pack(shards, kinds), tag + "_gather"), shapes, kinds)

    gather = jax.custom_vjp(impl)

    def fwd(shards):
        return impl(shards), None

    def bwd(_, cts):
        packed = _scatter_pack(cts, shapes, kinds)
        return (_scatter_unpack(_reduce_scatter(packed, tag + "_scatter"), shapes, kinds),)

    gather.defvjp(fwd, bwd)
    return gather


def _adamw(w, g, m, v, name):
    shape = w.shape
    cols = shape[-1]
    rows = math.prod(shape[:-1])
    tr = rows
    if rows > 512:
        for cand in (256, 128, 64, 32, 16, 8):
            if rows % cand == 0:
                tr = cand
                break
    c1 = 1.0 - ADAM_B1 ** ADAM_STEP
    c2 = 1.0 - ADAM_B2 ** ADAM_STEP

    def body(w_ref, g_ref, m_ref, v_ref, d_ref, nm_ref, nv_ref):
        gg = g_ref[...]
        nm = ADAM_B1 * m_ref[...] + (1.0 - ADAM_B1) * gg
        nv = ADAM_B2 * v_ref[...] + (1.0 - ADAM_B2) * (gg * gg)
        m_hat = nm / c1
        v_hat = nv / c2
        d_ref[...] = -ADAM_LR * (m_hat / (jnp.sqrt(v_hat) + ADAM_EPS) + ADAM_WD * w_ref[...])
        nm_ref[...] = nm
        nv_ref[...] = nv

    spec = pl.BlockSpec((tr, cols), lambda i: (i, 0))
    outs = pl.pallas_call(
        body, name=name, grid=(rows // tr,), out_shape=[jax.ShapeDtypeStruct((rows, cols), jnp.float32)] * 3,
        in_specs=[spec] * 4, out_specs=[spec] * 3,
    )(*[t.reshape(rows, cols) for t in (w, g, m, v)])
    return tuple(o.reshape(shape) for o in outs)


def _rms_fwd_call(x, g, res, name):
    t, d = x.shape
    tr = _tile(t, ROW_BLOCK)
    with_res = res is not None

    def body(*refs):
        if with_res:
            x_ref, g_ref, r_ref, y_ref = refs
        else:
            x_ref, g_ref, y_ref = refs
        xv = x_ref[...]
        rstd = lax.rsqrt(jnp.mean(xv * xv, axis=-1, keepdims=True) + RMS_EPS)
        y = xv * rstd * g_ref[...]
        if with_res:
            y = y + r_ref[...]
        y_ref[...] = y

    row = pl.BlockSpec((tr, d), lambda i: (i, 0))
    gspec = pl.BlockSpec((1, d), lambda i: (0, 0))
    ins = [x, g] + ([res] if with_res else [])
    return pl.pallas_call(
        body, name=name, grid=(t // tr,), out_shape=jax.ShapeDtypeStruct((t, d), jnp.float32),
        in_specs=[row, gspec] + ([row] if with_res else []), out_specs=row,
    )(*ins)


def _rms_bwd_call(x, g, dy, name):
    t, d = x.shape
    tr = _tile(t, ROW_BLOCK)

    def body(x_ref, g_ref, dy_ref, dx_ref, dg_ref):
        i = pl.program_id(0)
        xv = x_ref[...]
        rstd = lax.rsqrt(jnp.mean(xv * xv, axis=-1, keepdims=True) + RMS_EPS)
        xh = xv * rstd
        dyv = dy_ref[...]
        dyg = dyv * g_ref[...]
        dx_ref[...] = rstd * (dyg - xh * jnp.mean(dyg * xh, axis=-1, keepdims=True))

        @pl.when(i == 0)
        def _():
            dg_ref[...] = jnp.zeros_like(dg_ref)

        dg_ref[...] += jnp.sum(dyv * xh, axis=0, keepdims=True)

    row = pl.BlockSpec((tr, d), lambda i: (i, 0))
    gspec = pl.BlockSpec((1, d), lambda i: (0, 0))
    return pl.pallas_call(
        body, name=name, grid=(t // tr,),
        out_shape=[jax.ShapeDtypeStruct((t, d), jnp.float32), jax.ShapeDtypeStruct((1, d), jnp.float32)],
        in_specs=[row, gspec, row], out_specs=[row, gspec],
        compiler_params=pltpu.CompilerParams(dimension_semantics=("arbitrary",)),
    )(x, g, dy)


@functools.partial(jax.custom_vjp, nondiff_argnums=(3,))
def _norm_add(y, g, res, name):
    return _rms_fwd_call(y, g.reshape(1, -1), res, name + "_fwd")


def _norm_add_fwd(y, g, res, name):
    return _rms_fwd_call(y, g.reshape(1, -1), res, name + "_fwd"), (y, g)


def _norm_add_bwd(name, saved, dout):
    y, g = saved
    dy, dg = _rms_bwd_call(y, g.reshape(1, -1), dout, name + "_bwd")
    return dy, dg.reshape(g.shape), dout


_norm_add.defvjp(_norm_add_fwd, _norm_add_bwd)


@functools.partial(jax.custom_vjp, nondiff_argnums=(2,))
def _norm(x, g, name):
    return _rms_fwd_call(x, g.reshape(1, -1), None, name + "_fwd")


def _norm_fwd(x, g, name):
    return _rms_fwd_call(x, g.reshape(1, -1), None, name + "_fwd"), (x, g)


def _norm_bwd(name, saved, dout):
    x, g = saved
    dx, dg = _rms_bwd_call(x, g.reshape(1, -1), dout, name + "_bwd")
    return dx, dg.reshape(g.shape)


_norm.defvjp(_norm_fwd, _norm_bwd)


def _shift_down(x, before1, before2, row):
    xm1 = jnp.where(row == 0, before1, pltpu.roll(x, 1, 0))
    xm2 = jnp.where(row == 0, before2, jnp.where(row == 1, before1, pltpu.roll(x, 2, 0)))
    return xm1, xm2


def _shift_up(x, after1, after2, row):
    n = x.shape[0]
    xp1 = jnp.where(row == n - 1, after1, pltpu.roll(x, n - 1, 0))
    xp2 = jnp.where(row == n - 1, after2, jnp.where(row == n - 2, after1, pltpu.roll(x, n - 2, 0)))
    return xp1, xp2


def _ffn_act_fwd_call(u, conv_w, conv_b, name):
    t, f2 = u.shape
    f = f2 // 2
    tr, tc = _tile(t, ROW_BLOCK), _tile(f, COL_BLOCK)
    nj, hb = f // tc, tr // HALO

    def body(ug_ref, uu_ref, hg_ref, hu_ref, wg_ref, wu_ref, bg_ref, bu_ref, cg_ref, cu_ref, a_ref):
        i = pl.program_id(0)
        row = lax.broadcasted_iota(jnp.int32, (tr, tc), 0)

        def conv(x_ref, h_ref, w_ref, b_ref):
            xv = x_ref[...]
            h1 = jnp.where(i > 0, h_ref[7:8, :], 0.0)
            h2 = jnp.where(i > 0, h_ref[6:7, :], 0.0)
            xm1, xm2 = _shift_down(xv, h1, h2, row)
            return w_ref[0:1, :] * xm2 + w_ref[1:2, :] * xm1 + w_ref[2:3, :] * xv + b_ref[...]

        cg = conv(ug_ref, hg_ref, wg_ref, bg_ref)
        cu = conv(uu_ref, hu_ref, wu_ref, bu_ref)
        cg_ref[...] = cg
        cu_ref[...] = cu
        a_ref[...] = (cg * _sigmoid(cg) * cu).astype(jnp.bfloat16)

    def blk(off):
        return pl.BlockSpec((tr, tc), lambda i, j: (i, j + off))

    def halo(off):
        return pl.BlockSpec((HALO, tc), lambda i, j: (jnp.maximum(i * hb - 1, 0), j + off))

    def wspec(rows, off):
        return pl.BlockSpec((rows, tc), lambda i, j: (0, j + off))

    out_blk = pl.BlockSpec((tr, tc), lambda i, j: (i, j))
    return pl.pallas_call(
        body, name=name, grid=(t // tr, nj),
        out_shape=[jax.ShapeDtypeStruct((t, f), jnp.float32), jax.ShapeDtypeStruct((t, f), jnp.float32),
                   jax.ShapeDtypeStruct((t, f), jnp.bfloat16)],
        in_specs=[blk(0), blk(nj), halo(0), halo(nj), wspec(3, 0), wspec(3, nj), wspec(1, 0), wspec(1, nj)],
        out_specs=[out_blk, out_blk, out_blk],
    )(u, u, u, u, conv_w, conv_w, conv_b, conv_b)


def _ffn_act_bwd_call(da, cg, cu, u, conv_w, name):
    t, f = da.shape
    tr, tc = _tile(t, ROW_BLOCK), _tile(f, COL_BLOCK)
    nj, hb, ni = f // tc, tr // HALO, t // tr

    def body(da_ref, cg_ref, cu_ref, nda_ref, ncg_ref, ncu_ref, u_ref, hu_ref, w_ref, du_ref, dw_ref, db_ref):
        j, i = pl.program_id(0), pl.program_id(1)
        is_gate = j < nj
        row = lax.broadcasted_iota(jnp.int32, (tr, tc), 0)

        def duc_of(dav, cgv, cuv):
            sg = _sigmoid(cgv)
            d_gate = dav * cuv * (sg * (1.0 + cgv * (1.0 - sg)))
            d_up = dav * (cgv * sg)
            return jnp.where(is_gate, d_gate, d_up)

        duc = duc_of(da_ref[...], cg_ref[...], cu_ref[...])
        n1 = jnp.where(i < ni - 1, duc_of(nda_ref[0:1, :], ncg_ref[0:1, :], ncu_ref[0:1, :]), 0.0)
        n2 = jnp.where(i < ni - 1, duc_of(nda_ref[1:2, :], ncg_ref[1:2, :], ncu_ref[1:2, :]), 0.0)
        dp1, dp2 = _shift_up(duc, n1, n2, row)
        du_ref[...] = (w_ref[2:3, :] * duc + w_ref[1:2, :] * dp1 + w_ref[0:1, :] * dp2).astype(jnp.bfloat16)

        uv = u_ref[...]
        h1 = jnp.where(i > 0, hu_ref[7:8, :], 0.0)
        h2 = jnp.where(i > 0, hu_ref[6:7, :], 0.0)
        um1, um2 = _shift_down(uv, h1, h2, row)

        @pl.when(i == 0)
        def _():
            dw_ref[...] = jnp.zeros_like(dw_ref)
            db_ref[...] = jnp.zeros_like(db_ref)

        rows8 = lax.broadcasted_iota(jnp.int32, (HALO, tc), 0)
        s0 = jnp.sum(duc * um2, axis=0, keepdims=True)
        s1 = jnp.sum(duc * um1, axis=0, keepdims=True)
        s2 = jnp.sum(duc * uv, axis=0, keepdims=True)
        dw_ref[...] += jnp.where(rows8 == 0, s0, jnp.where(rows8 == 1, s1, jnp.where(rows8 == 2, s2, 0.0)))
        db_ref[...] += jnp.sum(duc, axis=0, keepdims=True)

    half = pl.BlockSpec((tr, tc), lambda j, i: (i, j % nj))
    nhalf = pl.BlockSpec((HALO, tc), lambda j, i: (jnp.minimum((i + 1) * hb, t // HALO - 1), j % nj))
    full = pl.BlockSpec((tr, tc), lambda j, i: (i, j))
    hfull = pl.BlockSpec((HALO, tc), lambda j, i: (jnp.maximum(i * hb - 1, 0), j))
    return pl.pallas_call(
        body, name=name, grid=(2 * nj, ni),
        out_shape=[jax.ShapeDtypeStruct((t, 2 * f), jnp.bfloat16), jax.ShapeDtypeStruct((HALO, 2 * f), jnp.float32),
                   jax.ShapeDtypeStruct((1, 2 * f), jnp.float32)],
        in_specs=[half, half, half, nhalf, nhalf, nhalf, full, hfull, pl.BlockSpec((3, tc), lambda j, i: (0, j))],
        out_specs=[full, pl.BlockSpec((HALO, tc), lambda j, i: (0, j)), pl.BlockSpec((1, tc), lambda j, i: (0, j))],
        compiler_params=pltpu.CompilerParams(dimension_semantics=("parallel", "arbitrary")),
    )(da, cg, cu, da, cg, cu, u, u, conv_w)


@functools.partial(jax.custom_vjp, nondiff_argnums=(5,))
def _ffn_block(h, w_up, conv_w, conv_b, w_down, name):
    return _ffn_block_fwd(h, w_up, conv_w, conv_b, w_down, name)[0]


def _ffn_block_fwd(h, w_up, conv_w, conv_b, w_down, name):
    hb = h.astype(jnp.bfloat16)
    u = _matmul(hb, w_up, "nn", name + "_up")
    cg, cu, a = _ffn_act_fwd_call(u, conv_w, conv_b.reshape(1, -1), name + "_act")
    return _matmul(a, w_down, "nn", name + "_down"), (hb, u, cg, cu, a, w_up, conv_w, conv_b, w_down)


def _ffn_block_bwd(name, saved, dy):
    hb, u, cg, cu, a, w_up, conv_w, conv_b, w_down = saved
    dyb = dy.astype(jnp.bfloat16)
    da = _matmul(dyb, w_down, "nt", name + "_down_dx")
    dw_down = _matmul(a, dyb, "tn", name + "_down_dw", jnp.bfloat16)
    du, dcw, dcb = _ffn_act_bwd_call(da, cg, cu, u, conv_w, name + "_act_bwd")
    dh = _matmul(du, w_up, "nt", name + "_up_dx")
    dw_up = _matmul(hb, du, "tn", name + "_up_dw", jnp.bfloat16)
    return dh, dw_up, dcw[:3], dcb.reshape(conv_b.shape), dw_down


_ffn_block.defvjp(_ffn_block_fwd, _ffn_block_bwd)


def _sconv_fwd_call(z, conv_w, name):
    t, d3 = z.shape
    d = d3 // 3
    tr, tc = _tile(t, ROW_BLOCK), _tile(d, COL_BLOCK)
    nj, hb = d // tc, tr // HALO

    def body(b_ref, c_ref, x_ref, hc_ref, hx_ref, w_ref, a_ref):
        i = pl.program_id(0)
        row = lax.broadcasted_iota(jnp.int32, (tr, tc), 0)
        p = c_ref[...] * x_ref[...]
        h1 = jnp.where(i > 0, hc_ref[7:8, :] * hx_ref[7:8, :], 0.0)
        h2 = jnp.where(i > 0, hc_ref[6:7, :] * hx_ref[6:7, :], 0.0)
        pm1, pm2 = _shift_down(p, h1, h2, row)
        pc = w_ref[0:1, :] * pm2 + w_ref[1:2, :] * pm1 + w_ref[2:3, :] * p
        a_ref[...] = (b_ref[...] * pc).astype(jnp.bfloat16)

    def blk(off):
        return pl.BlockSpec((tr, tc), lambda i, j: (i, j + off))

    def halo(off):
        return pl.BlockSpec((HALO, tc), lambda i, j: (jnp.maximum(i * hb - 1, 0), j + off))

    return pl.pallas_call(
        body, name=name, grid=(t // tr, nj), out_shape=jax.ShapeDtypeStruct((t, d), jnp.bfloat16),
        in_specs=[blk(0), blk(nj), blk(2 * nj), halo(nj), halo(2 * nj), pl.BlockSpec((3, tc), lambda i, j: (0, j))],
        out_specs=pl.BlockSpec((tr, tc), lambda i, j: (i, j)),
    )(z, z, z, z, z, conv_w)


def _sconv_bwd_call(da, z, conv_w, name):
    t, d = da.shape
    tr, tc = _tile(t, ROW_BLOCK), _tile(d, COL_BLOCK)
    nj, hb, ni = d // tc, tr // HALO, t // tr

    def body(da_ref, b_ref, c_ref, x_ref, hc_ref, hx_ref, nda_ref, nb_ref, w_ref, db_ref, dc_ref, dx_ref, dw_ref):
        i = pl.program_id(1)
        row = lax.broadcasted_iota(jnp.int32, (tr, tc), 0)
        cv, xv, dav = c_ref[...], x_ref[...], da_ref[...]
        p = cv * xv
        h1 = jnp.where(i > 0, hc_ref[7:8, :] * hx_ref[7:8, :], 0.0)
        h2 = jnp.where(i > 0, hc_ref[6:7, :] * hx_ref[6:7, :], 0.0)
        pm1, pm2 = _shift_down(p, h1, h2, row)
        pc = w_ref[0:1, :] * pm2 + w_ref[1:2, :] * pm1 + w_ref[2:3, :] * p
        db_ref[...] = (dav * pc).astype(jnp.bfloat16)
        dpc = dav * b_ref[...]
        n1 = jnp.where(i < ni - 1, nda_ref[0:1, :] * nb_ref[0:1, :], 0.0)
        n2 = jnp.where(i < ni - 1, nda_ref[1:2, :] * nb_ref[1:2, :], 0.0)
        dp1, dp2 = _shift_up(dpc, n1, n2, row)
        dp = w_ref[2:3, :] * dpc + w_ref[1:2, :] * dp1 + w_ref[0:1, :] * dp2
        dc_ref[...] = (dp * xv).astype(jnp.bfloat16)
        dx_ref[...] = (dp * cv).astype(jnp.bfloat16)

        @pl.when(i == 0)
        def _():
            dw_ref[...] = jnp.zeros_like(dw_ref)

        rows8 = lax.broadcasted_iota(jnp.int32, (HALO, tc), 0)
        s0 = jnp.sum(dpc * pm2, axis=0, keepdims=True)
        s1 = jnp.sum(dpc * pm1, axis=0, keepdims=True)
        s2 = jnp.sum(dpc * p, axis=0, keepdims=True)
        dw_ref[...] += jnp.where(rows8 == 0, s0, jnp.where(rows8 == 1, s1, jnp.where(rows8 == 2, s2, 0.0)))

    def blk(off):
        return pl.BlockSpec((tr, tc), lambda j, i: (i, j + off))

    def halo(off):
        return pl.BlockSpec((HALO, tc), lambda j, i: (jnp.maximum(i * hb - 1, 0), j + off))

    def nhalo(off):
        return pl.BlockSpec((HALO, tc), lambda j, i: (jnp.minimum((i + 1) * hb, t // HALO - 1), j + off))

    out = pl.BlockSpec((tr, tc), lambda j, i: (i, j))
    return pl.pallas_call(
        body, name=name, grid=(nj, ni),
        out_shape=[jax.ShapeDtypeStruct((t, d), jnp.bfloat16)] * 3 + [jax.ShapeDtypeStruct((HALO, d), jnp.float32)],
        in_specs=[blk(0), blk(0), blk(nj), blk(2 * nj), halo(nj), halo(2 * nj), nhalo(0), nhalo(0),
                  pl.BlockSpec((3, tc), lambda j, i: (0, j))],
        out_specs=[out, out, out, pl.BlockSpec((HALO, tc), lambda j, i: (0, j))],
        compiler_params=pltpu.CompilerParams(dimension_semantics=("parallel", "arbitrary")),
    )(da, z, z, z, z, z, da, z, conv_w)


@functools.partial(jax.custom_vjp, nondiff_argnums=(4,))
def _sconv_block(h, w_in, conv_w, w_out, name):
    return _sconv_block_fwd(h, w_in, conv_w, w_out, name)[0]


def _sconv_block_fwd(h, w_in, conv_w, w_out, name):
    hb = h.astype(jnp.bfloat16)
    z = _matmul(hb, w_in, "nn", name + "_in")
    a = _sconv_fwd_call(z, conv_w, name + "_gate")
    return _matmul(a, w_out, "nn", name + "_out"), (hb, z, a, w_in, conv_w, w_out)


def _sconv_block_bwd(name, saved, dy):
    hb, z, a, w_in, conv_w, w_out = saved
    dyb = dy.astype(jnp.bfloat16)
    da = _matmul(dyb, w_out, "nt", name + "_out_dx")
    dw_out = _matmul(a, dyb, "tn", name + "_out_dw", jnp.bfloat16)
    db, dc, dx, dcw = _sconv_bwd_call(da, z, conv_w, name + "_gate_bwd")
    dz = jnp.concatenate([db, dc, dx], axis=1)
    dh = _matmul(dz, w_in, "nt", name + "_in_dx")
    dw_in = _matmul(hb, dz, "tn", name + "_in_dw", jnp.bfloat16)
    return dh, dw_in, dcw[:3], dw_out


_sconv_block.defvjp(_sconv_block_fwd, _sconv_block_bwd)


def _to_heads(x, heads):
    t = x.shape[0]
    return x.reshape(t, heads, -1).transpose(1, 0, 2)


def _from_heads(x):
    h, t, d = x.shape
    return x.transpose(1, 0, 2).reshape(t, h * d)


def _partial_rope(x, positions):
    half = ROT_DIM // 2
    inv_freq = ROPE_THETA ** (-jnp.arange(half, dtype=jnp.float32) / half)
    ang = positions.astype(jnp.float32)[:, None] * inv_freq[None, :]
    cos = jnp.cos(ang)[:, None, :]
    sin = jnp.sin(ang)[:, None, :]
    x1, x2 = x[..., :half], x[..., half:ROT_DIM]
    return jnp.concatenate([x1 * cos - x2 * sin, x2 * cos + x1 * sin, x[..., ROT_DIM:]], axis=-1)


def _swa_probs(q_ref, kp_ref, kc_ref, sink_ref, n, w, scale):
    grp, d = q_ref.shape[0], q_ref.shape[2]
    q2 = q_ref[...].reshape(grp * w, d)
    kcat = jnp.concatenate([kp_ref[0], kc_ref[0]], axis=0)
    s3 = (_dot(q2, kcat, _NT) * scale).reshape(grp, w, 2 * w)
    qi = lax.broadcasted_iota(jnp.int32, s3.shape, 1)
    kj = lax.broadcasted_iota(jnp.int32, s3.shape, 2)
    diff = qi + w - kj
    allowed = (diff >= 0) & (diff < w) & ((kj >= w) | (n > 0))
    s3 = jnp.where(allowed, s3, NEG)
    sink = sink_ref[...]
    m = jnp.maximum(jnp.max(s3, axis=2, keepdims=True), sink)
    e = jnp.exp(s3 - m)
    es = jnp.exp(sink - m)
    z = jnp.sum(e, axis=2, keepdims=True) + es
    return q2, kcat, e / z, es / z


def _swa_fwd_call(q, k, v, sinks, name):
    qh, t, d = q.shape
    kv = k.shape[0]
    grp, w = qh // kv, SWA_WINDOW
    nblk = t // w
    scale = d ** -0.5

    def body(q_ref, kp_ref, kc_ref, vp_ref, vc_ref, sink_ref, o_ref):
        n = pl.program_id(1)
        _, _, p3, _ = _swa_probs(q_ref, kp_ref, kc_ref, sink_ref, n, w, scale)
        vcat = jnp.concatenate([vp_ref[0], vc_ref[0]], axis=0)
        o = _dot(p3.reshape(grp * w, 2 * w), vcat, _NN)
        o_ref[...] = o.reshape(grp, w, d).astype(o_ref.dtype)

    qspec = pl.BlockSpec((grp, w, d), lambda h, n: (h, n, 0))
    prev = pl.BlockSpec((1, w, d), lambda h, n: (h, jnp.maximum(n - 1, 0), 0))
    cur = pl.BlockSpec((1, w, d), lambda h, n: (h, n, 0))
    return pl.pallas_call(
        body, name=name, grid=(kv, nblk), out_shape=jax.ShapeDtypeStruct((qh, t, d), jnp.bfloat16),
        in_specs=[qspec, prev, cur, prev, cur, pl.BlockSpec((grp, 1, 1), lambda h, n: (h, 0, 0))],
        out_specs=qspec,
    )(q, k, k, v, v, sinks)


def _swa_bwd_call(q, k, v, sinks, do, name):
    qh, t, d = q.shape
    kv = k.shape[0]
    grp, w = qh // kv, SWA_WINDOW
    nblk = t // w
    scale = d ** -0.5

    def body(q_ref, kp_ref, kc_ref, vp_ref, vc_ref, sink_ref, do_ref, dq_ref, dk_ref, dv_ref, ds_ref,
             carry_k, carry_v, part_k, part_v):
        n = pl.program_id(1)

        @pl.when(n == 0)
        def _():
            carry_k[...] = jnp.zeros_like(carry_k)
            carry_v[...] = jnp.zeros_like(carry_v)
            ds_ref[...] = jnp.zeros_like(ds_ref)

        @pl.when(n < nblk)
        def _():
            q2, kcat, p3, ps = _swa_probs(q_ref, kp_ref, kc_ref, sink_ref, n, w, scale)
            vcat = jnp.concatenate([vp_ref[0], vc_ref[0]], axis=0)
            do2 = do_ref[...].reshape(grp * w, d)
            dp3 = _dot(do2, vcat, _NT).reshape(grp, w, 2 * w)
            rs = jnp.sum(p3 * dp3, axis=2, keepdims=True)
            ds2 = (p3 * (dp3 - rs)).reshape(grp * w, 2 * w)
            dsink = -jnp.sum(ps * rs, axis=1, keepdims=True)
            ds_ref[...] += jnp.broadcast_to(dsink, ds_ref.shape)
            dq_ref[...] = (_dot(ds2, kcat, _NN) * scale).reshape(grp, w, d)
            part_k[...] = _dot(ds2, q2, _TN) * scale
            part_v[...] = _dot(p3.reshape(grp * w, 2 * w), do2, _TN)

        @pl.when(n == nblk)
        def _():
            part_k[...] = jnp.zeros_like(part_k)
            part_v[...] = jnp.zeros_like(part_v)

        dk_ref[0] = carry_k[...] + part_k[0:w, :]
        dv_ref[0] = carry_v[...] + part_v[0:w, :]
        carry_k[...] = part_k[w:2 * w, :]
        carry_v[...] = part_v[w:2 * w, :]

    last = nblk - 1
    qspec = pl.BlockSpec((grp, w, d), lambda h, n: (h, jnp.minimum(n, last), 0))
    prev = pl.BlockSpec((1, w, d), lambda h, n: (h, jnp.maximum(jnp.minimum(n, last) - 1, 0), 0))
    cur = pl.BlockSpec((1, w, d), lambda h, n: (h, jnp.minimum(n, last), 0))
    kout = pl.BlockSpec((1, w, d), lambda h, n: (h, jnp.maximum(n - 1, 0), 0))
    return pl.pallas_call(
        body, name=name, grid=(kv, nblk + 1),
        out_shape=[jax.ShapeDtypeStruct((qh, t, d), jnp.float32), jax.ShapeDtypeStruct((kv, t, d), jnp.float32),
                   jax.ShapeDtypeStruct((kv, t, d), jnp.float32), jax.ShapeDtypeStruct((qh, 1, 128), jnp.float32)],
        in_specs=[qspec, prev, cur, prev, cur, pl.BlockSpec((grp, 1, 1), lambda h, n: (h, 0, 0)), qspec],
        out_specs=[qspec, kout, kout, pl.BlockSpec((grp, 1, 128), lambda h, n: (h, 0, 0))],
        scratch_shapes=[pltpu.VMEM((w, d), jnp.float32), pltpu.VMEM((w, d), jnp.float32),
                        pltpu.VMEM((2 * w, d), jnp.float32), pltpu.VMEM((2 * w, d), jnp.float32)],
        compiler_params=pltpu.CompilerParams(dimension_semantics=("parallel", "arbitrary")),
    )(q, k, k, v, v, sinks, do)


def _swa_split(z, positions, qh, kv, d):
    q, k, v = jnp.split(z, [qh * d, qh * d + kv * d], axis=-1)
    t = z.shape[0]
    q = _partial_rope(q.reshape(t, qh, d), positions).transpose(1, 0, 2)
    k = _partial_rope(k.reshape(t, kv, d), positions).transpose(1, 0, 2)
    v = v.reshape(t, kv, d).transpose(1, 0, 2)
    return q, k, v


@functools.partial(jax.custom_vjp, nondiff_argnums=(5,))
def _swa_block(h, positions, w_in, w_out, sinks, name):
    return _swa_block_fwd(h, positions, w_in, w_out, sinks, name)[0]


def _swa_block_fwd(h, positions, w_in, w_out, sinks, name):
    d = SWA_HEAD_DIM
    qh = h.shape[1] // d
    kv = qh // SWA_GROUP
    hb = h.astype(jnp.bfloat16)
    z = _matmul(hb, w_in, "nt", name + "_in")
    q, k, v = [a.astype(jnp.bfloat16) for a in _swa_split(z, positions, qh, kv, d)]
    o = _swa_fwd_call(q, k, v, sinks.reshape(qh, 1, 1), name + "_attn")
    a = _from_heads(o)
    return _matmul(a, w_out, "nn", name + "_out"), (hb, positions, z, q, k, v, a, w_in, w_out, sinks)


def _swa_block_bwd(name, saved, dy):
    hb, positions, z, q, k, v, a, w_in, w_out, sinks = saved
    d = SWA_HEAD_DIM
    qh = q.shape[0]
    kv = k.shape[0]
    dyb = dy.astype(jnp.bfloat16)
    da = _matmul(dyb, w_out, "nt", name + "_out_dx")
    dw_out = _matmul(a, dyb, "tn", name + "_out_dw", jnp.bfloat16)
    do = _to_heads(da, qh)
    dq, dk, dv, dsinks = _swa_bwd_call(q, k, v, sinks.reshape(qh, 1, 1), do, name + "_attn_bwd")
    _, split_vjp = jax.vjp(lambda zz: _swa_split(zz, positions, qh, kv, d), z)
    dz = split_vjp((dq, dk, dv))[0].astype(jnp.bfloat16)
    dh = _matmul(dz, w_in, "nn", name + "_in_dx")
    dw_in = _matmul(dz, hb, "tn", name + "_in_dw", jnp.bfloat16)
    return dh, None, dw_in, dw_out, dsinks[:, 0, 0].reshape(sinks.shape)


_swa_block.defvjp(_swa_block_fwd, _swa_block_bwd)


def _fox_specs(hp, nb, tb, d):
    blk = pl.BlockSpec((hp, 1, tb, d), lambda h, i: (h, i, 0, 0))
    whole = pl.BlockSpec((hp, nb, tb, d), lambda h, i: (h, 0, 0, 0))
    rowblk = pl.BlockSpec((hp, 1, 1, tb), lambda h, i: (h, i, 0, 0))
    rowwhole = pl.BlockSpec((hp, nb, 1, tb), lambda h, i: (h, 0, 0, 0))
    return blk, whole, rowblk, rowwhole


def _fox_fwd_call(q, k, v, g, bias, name):
    hh, nb, tb, d = q.shape
    scale = d ** -0.5

    hp = FOX_HEADS_PER_STEP

    def body(q_ref, k_ref, v_ref, g_ref, b_ref, og_ref, o_ref, lse_ref):
        i = pl.program_id(1)
        qs = [q_ref[a, 0] for a in range(hp)]
        rows = lax.broadcasted_iota(jnp.int32, (tb, tb), 0)
        cols = lax.broadcasted_iota(jnp.int32, (tb, tb), 1)

        def step(kb, carry, masked):
            out = []
            for a in range(hp):
                m, l, acc = carry[a]
                s = _dot(qs[a], k_ref[a, kb], _NT) * scale + b_ref[a, kb]
                if masked:
                    s = jnp.where(cols <= rows, s, NEG)
                m_new = jnp.maximum(m, jnp.max(s, axis=1, keepdims=True))
                alpha = jnp.exp(m - m_new)
                p = jnp.exp(s - m_new)
                out.append((m_new, alpha * l + jnp.sum(p, axis=1, keepdims=True),
                            alpha * acc + _dot(p, v_ref[a, kb], _NN)))
            return tuple(out)

        init = (jnp.full((tb, 1), NEG, jnp.float32), jnp.zeros((tb, 1), jnp.float32), jnp.zeros((tb, d), jnp.float32))
        carry = lax.fori_loop(0, i, lambda kb, c: step(kb, c, False), (init,) * hp)
        for a, (m, l, acc) in enumerate(step(i, carry, True)):
            o = acc / l
            o_ref[a, 0] = o
            og_ref[a, 0] = (o * _sigmoid(g_ref[a, 0])).astype(og_ref.dtype)
            lse_ref[a, 0] = _col_to_row(m + jnp.log(l))

    blk, whole, rowblk, rowwhole = _fox_specs(hp, nb, tb, d)
    return pl.pallas_call(
        body, name=name, grid=(hh // hp, nb),
        out_shape=[jax.ShapeDtypeStruct(q.shape, jnp.bfloat16), jax.ShapeDtypeStruct(q.shape, jnp.float32),
                   jax.ShapeDtypeStruct((hh, nb, 1, tb), jnp.float32)],
        in_specs=[blk, whole, whole, blk, rowwhole],
        out_specs=[blk, blk, rowblk],
    )(q, k, v, g, bias)


def _fox_dq_call(q, k, v, g, o, dog, bias, lse, name):
    hh, nb, tb, d = q.shape
    scale = d ** -0.5

    hp = FOX_HEADS_PER_STEP

    def body(q_ref, k_ref, v_ref, g_ref, o_ref, dog_ref, b_ref, lse_ref, dq_ref, dg_ref, do_ref, delta_ref):
        i = pl.program_id(1)
        rows = lax.broadcasted_iota(jnp.int32, (tb, tb), 0)
        cols = lax.broadcasted_iota(jnp.int32, (tb, tb), 1)
        qs, dos, deltas, lses = [], [], [], []
        for a in range(hp):
            sig = _sigmoid(g_ref[a, 0])
            ov, dogv = o_ref[a, 0], dog_ref[a, 0]
            do = dogv * sig
            dg_ref[a, 0] = dogv * ov * sig * (1.0 - sig)
            do_ref[a, 0] = do.astype(do_ref.dtype)
            qs.append(q_ref[a, 0])
            dos.append(do.astype(jnp.bfloat16))
            deltas.append(jnp.sum(do * ov, axis=1, keepdims=True))
            lses.append(_row_to_col(lse_ref[a, 0]))

        def step(kb, carry, masked):
            out = []
            for a in range(hp):
                dq, spdp, sp = carry[a]
                s = _dot(qs[a], k_ref[a, kb], _NT) * scale + b_ref[a, kb]
                p = jnp.exp(s - lses[a])
                if masked:
                    p = jnp.where(cols <= rows, p, 0.0)
                dp = _dot(dos[a], v_ref[a, kb], _NT)
                out.append((dq + _dot(p * (dp - deltas[a]), k_ref[a, kb], _NN),
                            spdp + jnp.sum(p * dp, axis=1, keepdims=True), sp + jnp.sum(p, axis=1, keepdims=True)))
            return tuple(out)

        zcol = jnp.zeros((tb, 1), jnp.float32)
        init = (jnp.zeros((tb, d), jnp.float32), zcol, zcol)
        carry = lax.fori_loop(0, i, lambda kb, c: step(kb, c, False), (init,) * hp)
        for a, (dq, spdp, sp) in enumerate(step(i, carry, True)):
            dq_ref[a, 0] = dq * scale
            delta_ref[a, 0] = _col_to_row(spdp / sp)

    blk, whole, rowblk, rowwhole = _fox_specs(hp, nb, tb, d)
    return pl.pallas_call(
        body, name=name, grid=(hh // hp, nb),
        out_shape=[jax.ShapeDtypeStruct(q.shape, jnp.float32), jax.ShapeDtypeStruct(q.shape, jnp.float32),
                   jax.ShapeDtypeStruct(q.shape, jnp.bfloat16), jax.ShapeDtypeStruct((hh, nb, 1, tb), jnp.float32)],
        in_specs=[blk, whole, whole, blk, blk, blk, rowwhole, rowblk],
        out_specs=[blk, blk, blk, rowblk],
    )(q, k, v, g, o, dog, bias, lse)


def _fox_dkv_call(q, k, v, do, bias, lse, delta, name):
    hh, nb, tb, d = q.shape
    scale = d ** -0.5

    hp = FOX_HEADS_PER_STEP

    def body(q_ref, k_ref, v_ref, do_ref, b_ref, lse_ref, delta_ref, dk_ref, dv_ref, db_ref):
        j = pl.program_id(1)
        ks = [k_ref[a, 0] for a in range(hp)]
        vs = [v_ref[a, 0] for a in range(hp)]
        bcols = [_row_to_col(b_ref[a, 0]) for a in range(hp)]
        rows = lax.broadcasted_iota(jnp.int32, (tb, tb), 0)
        cols = lax.broadcasted_iota(jnp.int32, (tb, tb), 1)

        def step(qb, carry, masked):
            out = []
            for a in range(hp):
                dk, dv, db = carry[a]
                st = _dot(ks[a], q_ref[a, qb], _NT) * scale + bcols[a] - lse_ref[a, qb]
                pt = jnp.exp(st)
                if masked:
                    pt = jnp.where(rows <= cols, pt, 0.0)
                dpt = _dot(vs[a], do_ref[a, qb], _NT)
                dst = pt * (dpt - delta_ref[a, qb])
                out.append((dk + _dot(dst, q_ref[a, qb], _NN), dv + _dot(pt, do_ref[a, qb], _NN),
                            db + jnp.sum(dst, axis=1, keepdims=True)))
            return tuple(out)

        init = (jnp.zeros((tb, d), jnp.float32), jnp.zeros((tb, d), jnp.float32), jnp.zeros((tb, 1), jnp.float32))
        carry = step(j, (init,) * hp, True)
        for a, (dk, dv, db) in enumerate(lax.fori_loop(j + 1, nb, lambda qb, c: step(qb, c, False), carry)):
            dk_ref[a, 0] = dk * scale
            dv_ref[a, 0] = dv
            db_ref[a, 0] = _col_to_row(db)

    blk, whole, rowblk, rowwhole = _fox_specs(hp, nb, tb, d)
    return pl.pallas_call(
        body, name=name, grid=(hh // hp, nb),
        out_shape=[jax.ShapeDtypeStruct(q.shape, jnp.float32), jax.ShapeDtypeStruct(q.shape, jnp.float32),
                   jax.ShapeDtypeStruct((hh, nb, 1, tb), jnp.float32)],
        in_specs=[whole, blk, blk, whole, rowblk, rowwhole, rowwhole],
        out_specs=[blk, blk, rowblk],
    )(q, k, v, do, bias, lse, delta)


def _fox_bias_fwd_call(fl, b_f, name):
    t, lanes = fl.shape
    tr = _tile(t, ROW_BLOCK)

    def body(fl_ref, b_ref, o_ref, carry):
        i = pl.program_id(0)

        @pl.when(i == 0)
        def _():
            carry[...] = jnp.zeros_like(carry)

        xv = fl_ref[...] + b_ref[...]
        ls = jnp.minimum(xv, 0.0) - jnp.log(1.0 + jnp.exp(-jnp.abs(xv)))
        pos = lax.broadcasted_iota(jnp.int32, (tr, lanes), 0)
        o_ref[...] = -(_group_cumsum(ls, tr, pos) + carry[...])
        carry[...] += jnp.sum(ls, axis=0, keepdims=True)

    blk = pl.BlockSpec((tr, lanes), lambda i: (i, 0))
    return pl.pallas_call(
        body, name=name, grid=(t // tr,), out_shape=jax.ShapeDtypeStruct((t, lanes), jnp.float32),
        in_specs=[blk, pl.BlockSpec((1, lanes), lambda i: (0, 0))], out_specs=blk,
        scratch_shapes=[pltpu.VMEM((1, lanes), jnp.float32)],
        compiler_params=pltpu.CompilerParams(dimension_semantics=("arbitrary",)),
    )(fl, b_f)


def _fox_bias_bwd_call(dbias, fl, b_f, name):
    t, lanes = fl.shape
    tr = _tile(t, ROW_BLOCK)
    last = t // tr - 1

    def body(db_ref, fl_ref, b_ref, dfl_ref, dbf_ref, carry):
        i = pl.program_id(0)

        @pl.when(i == 0)
        def _():
            carry[...] = jnp.zeros_like(carry)
            dbf_ref[...] = jnp.zeros_like(dbf_ref)

        dbv = db_ref[...]
        pos = lax.broadcasted_iota(jnp.int32, (tr, lanes), 0)
        dls = -(_group_rcumsum(dbv, tr, pos) + carry[...])
        carry[...] += jnp.sum(dbv, axis=0, keepdims=True)
        dfl = dls * _sigmoid(-(fl_ref[...] + b_ref[...]))
        dfl_ref[...] = dfl
        dbf_ref[...] += jnp.sum(dfl, axis=0, keepdims=True)

    blk = pl.BlockSpec((tr, lanes), lambda i: (last - i, 0))
    one = pl.BlockSpec((1, lanes), lambda i: (0, 0))
    return pl.pallas_call(
        body, name=name, grid=(t // tr,),
        out_shape=[jax.ShapeDtypeStruct((t, lanes), jnp.float32), jax.ShapeDtypeStruct((1, lanes), jnp.float32)],
        in_specs=[blk, blk, one], out_specs=[blk, one],
        scratch_shapes=[pltpu.VMEM((1, lanes), jnp.float32)],
        compiler_params=pltpu.CompilerParams(dimension_semantics=("arbitrary",)),
    )(dbias, fl, b_f)


@functools.partial(jax.custom_vjp, nondiff_argnums=(5,))
def _fox_block(h, w_qkvg, w_f, b_f, w_out, name):
    return _fox_block_fwd(h, w_qkvg, w_f, b_f, w_out, name)[0]


def _fox_block_fwd(h, w_qkvg, w_f, b_f, w_out, name):
    t, dm = h.shape
    d = FOX_HEAD_DIM
    heads = dm // d
    lanes = w_f.shape[0]
    tb = _tile(t, ATT_BLOCK)
    nb = t // tb
    hb = h.astype(jnp.bfloat16)
    z = _matmul(hb, w_qkvg, "nt", name + "_in")
    fl = _matmul(hb, w_f, "nt", name + "_inf")
    z4 = z.reshape(t, 4, heads, d).transpose(1, 2, 0, 3).reshape(4, heads, nb, tb, d)
    q, k, v = [z4[i].astype(jnp.bfloat16) for i in range(3)]
    g = z4[3]
    b_fp = jnp.pad(b_f, (0, lanes - heads)).reshape(1, lanes)
    bias = _fox_bias_fwd_call(fl, b_fp, name + "_bias")[:, :heads].T.reshape(heads, nb, 1, tb)
    og, o, lse = _fox_fwd_call(q, k, v, g, bias, name + "_attn")
    a = _from_heads(og.reshape(heads, t, d))
    return _matmul(a, w_out, "nn", name + "_out"), (hb, q, k, v, g, bias, o, lse, a, fl, b_fp, w_qkvg, w_f, w_out)


def _fox_block_bwd(name, saved, dy):
    hb, q, k, v, g, bias, o, lse, a, fl, b_fp, w_qkvg, w_f, w_out = saved
    heads, nb, tb, d = q.shape
    t = nb * tb
    lanes = w_f.shape[0]
    dyb = dy.astype(jnp.bfloat16)
    da = _matmul(dyb, w_out, "nt", name + "_out_dx")
    dw_out = _matmul(a, dyb, "tn", name + "_out_dw", jnp.bfloat16)
    dog = _to_heads(da, heads).reshape(heads, nb, tb, d)
    dq, dg, do, delta = _fox_dq_call(q, k, v, g, o, dog, bias, lse, name + "_attn_dq")
    dk, dv, dbias = _fox_dkv_call(q, k, v, do, bias, lse, delta, name + "_attn_dkv")
    dz = jnp.stack([dq, dk, dv, dg]).reshape(4, heads, t, d).transpose(2, 0, 1, 3).reshape(t, 4 * heads * d)
    dz = dz.astype(jnp.bfloat16)
    dbias_rows = jnp.pad(dbias.reshape(heads, t).T, ((0, 0), (0, lanes - heads)))
    dfl, db_fp = _fox_bias_bwd_call(dbias_rows, fl, b_fp, name + "_bias_bwd")
    dflb = dfl.astype(jnp.bfloat16)
    dh = _matmul(dz, w_qkvg, "nn", name + "_in_dx") + _matmul(dflb, w_f, "nn", name + "_inf_dx")
    dw_qkvg = _matmul(dz, hb, "tn", name + "_in_dw", jnp.bfloat16)
    dw_f = _matmul(dflb, hb, "tn", name + "_inf_dw", jnp.bfloat16)
    return dh, dw_qkvg, dw_f, db_fp[0, :heads], dw_out


_fox_block.defvjp(_fox_block_fwd, _fox_block_bwd)


def _group_cumsum(x, c, pos):
    sh = 1
    while sh < c:
        x = x + jnp.where(pos >= sh, pltpu.roll(x, sh, 0), 0.0)
        sh *= 2
    return x


def _group_rcumsum(x, c, pos):
    n = x.shape[0]
    sh = 1
    while sh < c:
        x = x + jnp.where(pos < c - sh, pltpu.roll(x, n - sh, 0), 0.0)
        sh *= 2
    return x


def _hgrn_prep(zq_ref, zf_ref, zi_ref, lb_ref, b_scr, rows, c):
    nb = rows // c
    zq, zf = zq_ref[...], zf_ref[...]
    lbv = lb_ref[...]
    sq = _sigmoid(zq)
    sf = _sigmoid(zf)
    f = lbv + (1.0 - lbv) * sf
    pos = lax.broadcasted_iota(jnp.int32, (rows, 128), 0) % c
    b = _group_cumsum(jnp.log(f), c, pos)
    sh = (nb, c, 128)
    q3, k3, v3, b3 = (zq * sq).reshape(sh), (1.0 - f).reshape(sh), zi_ref[...].reshape(sh), b.reshape(sh)
    b_scr[...] = b3
    glast = b_scr[:, c - 1:c, :]
    eb = jnp.exp(b3)
    ek = jnp.exp(glast - b3)
    return dict(zq=zq, sq=sq, sf=sf, f=f, pos=pos, q3=q3, k3=k3, v3=v3, b3=b3, eb=eb, ek=ek,
                qt=q3 * eb, kh=k3 * ek, dec=jnp.exp(glast))


def _hgrn_fwd_call(z, lb, norm_g, name):
    t, w4 = z.shape
    w = w4 // 4
    heads = w // HGRN_EXPAND
    rows, c = _tile(t, HGRN_ROWS), HGRN_SUB
    nb, ntb = rows // c, t // rows

    def body(zq_ref, zf_ref, zi_ref, zg_ref, lb_ref, ng_ref, og_ref, o_ref, ck_ref, st_ref, b_scr, k_scr, v_scr):
        tb = pl.program_id(1)

        @pl.when(tb == 0)
        def _():
            st_ref[...] = jnp.zeros_like(st_ref)

        ck_ref[0, 0] = st_ref[...]
        p = _hgrn_prep(zq_ref, zf_ref, zi_ref, lb_ref, b_scr, rows, c)
        k_scr[...] = p["k3"]
        v_scr[...] = p["v3"]
        tpos = lax.broadcasted_iota(jnp.int32, (nb, c, 128), 1)
        od = jnp.zeros((nb, c, 128), jnp.float32)
        for s in range(c):
            bs, ks, vs = b_scr[:, s:s + 1, :], k_scr[:, s:s + 1, :], v_scr[:, s:s + 1, :]
            e = jnp.exp(jnp.minimum(p["b3"] - bs, 0.0))
            a = jnp.sum(jnp.where(tpos >= s, p["q3"] * ks * e, 0.0), axis=2, keepdims=True)
            od = od + a * vs
        st = st_ref[...]
        for i in range(nb):
            o_ref[pl.ds(i * c, c), :] = _dot(p["qt"][i], st, _NT) + od[i]
            st = st * p["dec"][i] + _dot(p["v3"][i], p["kh"][i], _TN)
        st_ref[...] = st
        o = o_ref[...]
        rstd = lax.rsqrt(jnp.mean(o * o, axis=1, keepdims=True) + RMS_EPS)
        zg = zg_ref[...]
        og_ref[...] = (o * rstd * ng_ref[...] * (zg * _sigmoid(zg))).astype(og_ref.dtype)

    def col(off):
        return pl.BlockSpec((rows, 128), lambda h, tb: (tb, h + off))

    out = pl.BlockSpec((rows, 128), lambda h, tb: (tb, h))
    return pl.pallas_call(
        body, name=name, grid=(heads, ntb),
        out_shape=[jax.ShapeDtypeStruct((t, w), jnp.bfloat16), jax.ShapeDtypeStruct((t, w), jnp.float32),
                   jax.ShapeDtypeStruct((heads, ntb, 128, 128), jnp.float32)],
        in_specs=[col(0), col(heads), col(2 * heads), col(3 * heads), pl.BlockSpec((1, 128), lambda h, tb: (0, h)),
                  pl.BlockSpec((1, 128), lambda h, tb: (0, 0))],
        out_specs=[out, out, pl.BlockSpec((1, 1, 128, 128), lambda h, tb: (h, tb, 0, 0))],
        scratch_shapes=[pltpu.VMEM((128, 128), jnp.float32)] + [pltpu.VMEM((nb, c, 128), jnp.float32)] * 3,
        compiler_params=pltpu.CompilerParams(dimension_semantics=("parallel", "arbitrary")),
    )(z, z, z, z, lb, norm_g)


def _hgrn_bwd_call(z, lb, norm_g, o_raw, dog, ck, name):
    t, w4 = z.shape
    w = w4 // 4
    heads = w // HGRN_EXPAND
    rows, c = _tile(t, HGRN_ROWS), HGRN_SUB
    nb, ntb = rows // c, t // rows

    def body(zq_ref, zf_ref, zi_ref, zg_ref, lb_ref, ng_ref, o_ref, dog_ref, ck_ref,
             dzq_ref, dzf_ref, dzi_ref, dzg_ref, dlb_ref, dng_ref,
             dst_ref, s_store, b_scr, k_scr, v_scr, dqt_scr, dkh_scr, dv_scr, ddec_scr):
        tb = pl.program_id(1)

        @pl.when(tb == 0)
        def _():
            dst_ref[...] = jnp.zeros_like(dst_ref)
            dlb_ref[...] = jnp.zeros_like(dlb_ref)
            dng_ref[...] = jnp.zeros_like(dng_ref)

        p = _hgrn_prep(zq_ref, zf_ref, zi_ref, lb_ref, b_scr, rows, c)
        q3, k3, v3, b3 = p["q3"], p["k3"], p["v3"], p["b3"]

        o = o_ref[...]
        rstd = lax.rsqrt(jnp.mean(o * o, axis=1, keepdims=True) + RMS_EPS)
        xh = o * rstd
        zg = zg_ref[...]
        sg = _sigmoid(zg)
        ng = ng_ref[...]
        dogv = dog_ref[...]
        don = dogv * (zg * sg)
        dzg_ref[...] = (dogv * (xh * ng) * (sg * (1.0 + zg * (1.0 - sg)))).astype(dzg_ref.dtype)
        dng_ref[0] += jnp.sum(don * xh, axis=0, keepdims=True)
        dyg = don * ng
        do = rstd * (dyg - xh * jnp.mean(dyg * xh, axis=1, keepdims=True))
        do3 = do.reshape(nb, c, 128)

        st = ck_ref[0, 0]
        for i in range(nb):
            s_store[i] = st
            st = st * p["dec"][i] + _dot(v3[i], p["kh"][i], _TN)

        dst = dst_ref[...]
        for i in reversed(range(nb)):
            sl = pl.ds(i * c, c)
            st_i = s_store[i]
            dqt_scr[sl, :] = _dot(do3[i], st_i, _NN)
            dv_scr[sl, :] = _dot(p["kh"][i], dst, _NT)
            dkh_scr[sl, :] = _dot(v3[i], dst, _NN)
            ddec_scr[i] = jnp.broadcast_to(jnp.sum(st_i * dst, axis=0, keepdims=True), (8, 128))
            dst = dst * p["dec"][i] + _dot(do3[i], p["qt"][i], _TN)
        dst_ref[...] = dst

        k_scr[...] = k3
        v_scr[...] = v3
        tpos = lax.broadcasted_iota(jnp.int32, (nb, c, 128), 1)
        zero = jnp.zeros((nb, c, 128), jnp.float32)
        dqd, dkd, dvd, dbd = zero, zero, zero, zero
        for s in range(c):
            bs, ks, vs = b_scr[:, s:s + 1, :], k_scr[:, s:s + 1, :], v_scr[:, s:s + 1, :]
            e = jnp.where(tpos >= s, jnp.exp(jnp.minimum(b3 - bs, 0.0)), 0.0)
            qe = q3 * e
            a = jnp.sum(qe * ks, axis=2, keepdims=True)
            da = jnp.sum(do3 * vs, axis=2, keepdims=True)
            y = da * (ks * e)
            dqd = dqd + y
            dbd = dbd + y * q3
            dks = jnp.sum(da * qe, axis=1, keepdims=True)
            dvs = jnp.sum(a * do3, axis=1, keepdims=True)
            at_s = tpos == s
            dkd = dkd + jnp.where(at_s, dks, 0.0)
            dvd = dvd + jnp.where(at_s, dvs, 0.0)
            dbd = dbd - jnp.where(at_s, dks * ks, 0.0)

        dqt3 = dqt_scr[...].reshape(nb, c, 128)
        dkh3 = dkh_scr[...].reshape(nb, c, 128)
        ddec = ddec_scr[:, 0:1, :]
        dq_act = dqt3 * p["eb"] + dqd
        dk = dkh3 * p["ek"] + dkd
        khd = p["kh"] * dkh3
        db = p["qt"] * dqt3 - khd + dbd
        dglast = jnp.sum(khd, axis=1, keepdims=True) + ddec * p["dec"]
        db = db + jnp.where(tpos == c - 1, dglast, 0.0)
        dlogf = _group_rcumsum(db.reshape(rows, 128), c, p["pos"])

        zq, sq, sf, f = p["zq"], p["sq"], p["sf"], p["f"]
        lbv = lb_ref[...]
        dzq_ref[...] = (dq_act.reshape(rows, 128) * (sq * (1.0 + zq * (1.0 - sq)))).astype(dzq_ref.dtype)
        df = dlogf / f - dk.reshape(rows, 128)
        dzf_ref[...] = (df * (1.0 - lbv) * (sf * (1.0 - sf))).astype(dzf_ref.dtype)
        dlb_ref[...] += jnp.sum(df * (1.0 - sf), axis=0, keepdims=True)
        dzi_ref[...] = (dv_scr[...] + dvd.reshape(rows, 128)).astype(dzi_ref.dtype)

    last = ntb - 1

    def col(off):
        return pl.BlockSpec((rows, 128), lambda h, tb: (last - tb, h + off))

    out = pl.BlockSpec((rows, 128), lambda h, tb: (last - tb, h))
    lbs = pl.BlockSpec((1, 128), lambda h, tb: (0, h))
    big = pltpu.VMEM((rows, 128), jnp.float32)
    return pl.pallas_call(
        body, name=name, grid=(heads, ntb),
        out_shape=[jax.ShapeDtypeStruct((t, w), jnp.bfloat16)] * 4
        + [jax.ShapeDtypeStruct((1, w), jnp.float32), jax.ShapeDtypeStruct((heads, 1, 128), jnp.float32)],
        in_specs=[col(0), col(heads), col(2 * heads), col(3 * heads), lbs, pl.BlockSpec((1, 128), lambda h, tb: (0, 0)),
                  out, out, pl.BlockSpec((1, 1, 128, 128), lambda h, tb: (h, last - tb, 0, 0))],
        out_specs=[out, out, out, out, lbs, pl.BlockSpec((1, 1, 128), lambda h, tb: (h, 0, 0))],
        scratch_shapes=[pltpu.VMEM((128, 128), jnp.float32), pltpu.VMEM((nb, 128, 128), jnp.float32)]
        + [pltpu.VMEM((nb, c, 128), jnp.float32)] * 3 + [big] * 3 + [pltpu.VMEM((nb, 8, 128), jnp.float32)],
        compiler_params=pltpu.CompilerParams(dimension_semantics=("parallel", "arbitrary")),
    )(z, z, z, z, lb, norm_g, o_raw, dog, ck)


@functools.partial(jax.custom_vjp, nondiff_argnums=(5,))
def _hgrn_block(h, w_in, w_out, norm_g, lb, name):
    return _hgrn_block_fwd(h, w_in, w_out, norm_g, lb, name)[0]


def _hgrn_block_fwd(h, w_in, w_out, norm_g, lb, name):
    hb = h.astype(jnp.bfloat16)
    z = _matmul(hb, w_in, "nn", name + "_in")
    og, o_raw, ck = _hgrn_fwd_call(z, lb.reshape(1, -1), norm_g.reshape(1, -1), name + "_scan")
    return _matmul(og, w_out, "nn", name + "_out"), (hb, z, og, o_raw, ck, w_in, w_out, norm_g, lb)


def _hgrn_block_bwd(name, saved, dy):
    hb, z, og, o_raw, ck, w_in, w_out, norm_g, lb = saved
    dyb = dy.astype(jnp.bfloat16)
    dog = _matmul(dyb, w_out, "nt", name + "_out_dx")
    dw_out = _matmul(og, dyb, "tn", name + "_out_dw", jnp.bfloat16)
    dzq, dzf, dzi, dzg, dlb, dng = _hgrn_bwd_call(z, lb.reshape(1, -1), norm_g.reshape(1, -1), o_raw, dog, ck,
                                                  name + "_scan_bwd")
    dz = jnp.concatenate([dzq, dzf, dzi, dzg], axis=1)
    dh = _matmul(dz, w_in, "nt", name + "_in_dx")
    dw_in = _matmul(hb, dz, "tn", name + "_in_dw", jnp.bfloat16)
    return dh, dw_in, dw_out, jnp.sum(dng, axis=0).reshape(norm_g.shape), dlb.reshape(lb.shape)


_hgrn_block.defvjp(_hgrn_block_fwd, _hgrn_block_bwd)


def _layer_plan(sharded):
    mixers = [
        [("hgrn_w_in", "col"), ("hgrn_w_out", "row")],
        [("swa_w_in", "colT"), ("swa_w_out", "row")],
        [("sc_w_in", "col"), ("sc_conv_w", "exact"), ("sc_w_out", "row")],
        [("fox_w_in", "colT"), ("fox_w_out", "row")],
    ]
    depth = sharded["ffn_w_up"].shape[0]
    plan = []
    for i in range(depth):
        m, j = i % len(mixers), i // len(mixers)
        entries = [(n, j, kind) for n, kind in mixers[m]]
        entries += [("ffn_w_up", i, "col"), ("ffn_conv_w", i, "exact"), ("ffn_w_down", i, "row")]
        plan.append(entries)
    return plan


def _local_loss(sharded, repl, x, positions, loss_target):
    x = x[0]
    loss_target = loss_target[0]
    d = x.shape[-1]
    plan = _layer_plan(sharded)
    lb_table = jnp.cumsum(jax.nn.softmax(repl["hgrn_lb_param"], axis=0), axis=0)
    for i, entries in enumerate(plan):
        shards = tuple(sharded[n][j] for n, j, _ in entries)
        kinds = tuple(kind for _, _, kind in entries)
        full = _make_layer_gather([s.shape for s in shards], kinds, "layer%d" % i)(shards)
        wts = {n: f for (n, _, _), f in zip(entries, full)}
        m, j = i % 4, i // 4
        tag = "l%d" % i
        hn = _norm(x, repl["mix_pre_g"][i], tag + "_mix_pre")
        if m == 0:
            y = _hgrn_block(hn, wts["hgrn_w_in"], wts["hgrn_w_out"], repl["hgrn_norm_g"][j], lb_table[i], tag + "_hgrn")
        elif m == 1:
            y = _swa_block(hn, positions, wts["swa_w_in"], wts["swa_w_out"], repl["swa_sinks"][j], tag + "_swa")
        elif m == 2:
            y = _sconv_block(hn, wts["sc_w_in"], wts["sc_conv_w"], wts["sc_w_out"], tag + "_sc")
        else:
            w_in = wts["fox_w_in"]
            heads = d // FOX_HEAD_DIM
            w_qkvg = jnp.concatenate([w_in[:3 * d], w_in[3 * d + heads:]], axis=0)
            w_f = jnp.pad(w_in[3 * d:3 * d + heads], ((0, 128 - heads), (0, 0)))
            y = _fox_block(hn, w_qkvg, w_f, repl["fox_b_f"][j], wts["fox_w_out"], tag + "_fox")
        x = _norm_add(y, repl["mix_post_g"][i], x, tag + "_mix_post")
        hn = _norm(x, repl["ffn_pre_g"][i], tag + "_ffn_pre")
        y = _ffn_block(hn, wts["ffn_w_up"], wts["ffn_conv_w"], repl["ffn_conv_b"][i], wts["ffn_w_down"], tag + "_ffn")
        x = _norm_add(y, repl["ffn_post_g"][i], x, tag + "_ffn_post")
    err = jnp.square(x - loss_target)
    return 0.5 * jnp.sum(jnp.mean(err, axis=-1))


SHARDED = ["hgrn_w_in", "hgrn_w_out", "swa_w_in", "swa_w_out", "sc_w_in", "sc_conv_w", "sc_w_out",
           "fox_w_in", "fox_w_out", "ffn_w_up", "ffn_conv_w", "ffn_w_down"]
REPLICATED = ["mix_pre_g", "mix_post_g", "ffn_pre_g", "ffn_post_g", "hgrn_norm_g", "hgrn_lb_param",
              "swa_sinks", "fox_b_f", "ffn_conv_b"]
WEIGHTS = ["mix_pre_g", "mix_post_g", "ffn_pre_g", "ffn_post_g", "hgrn_w_in", "hgrn_w_out", "hgrn_norm_g",
           "hgrn_lb_param", "swa_w_in", "swa_w_out", "swa_sinks", "sc_w_in", "sc_conv_w", "sc_w_out",
           "fox_w_in", "fox_b_f", "fox_w_out", "ffn_w_up", "ffn_conv_w", "ffn_conv_b", "ffn_w_down"]


def _sum_replicated(loss, grads):
    parts = [loss.reshape(1)] + [grads[n].reshape(-1) for n in REPLICATED]
    flat = jnp.concatenate(parts)
    n = flat.shape[0]
    cols = 1024
    rows = -(-n // cols)
    rows += (-rows) % 8
    flat = jnp.pad(flat, (0, rows * cols - n)).reshape(rows, cols)
    total = _sum_blocks(_all_gather(flat, "small_gather"), "small_sum").reshape(-1)
    out, off = {}, 1
    for name in REPLICATED:
        size = grads[name].size
        out[name] = total[off:off + size].reshape(grads[name].shape)
        off += size
    return total[0], out


def kernel(x, positions, mix_pre_g, mix_post_g, ffn_pre_g, ffn_post_g, hgrn_w_in, hgrn_w_out, hgrn_norm_g, hgrn_lb_param, swa_w_in, swa_w_out, swa_sinks, sc_w_in, sc_conv_w, sc_w_out, fox_w_in, fox_b_f, fox_w_out, ffn_w_up, ffn_conv_w, ffn_conv_b, ffn_w_down, loss_target, m_mix_pre_g, m_mix_post_g, m_ffn_pre_g, m_ffn_post_g, m_hgrn_w_in, m_hgrn_w_out, m_hgrn_norm_g, m_hgrn_lb_param, m_swa_w_in, m_swa_w_out, m_swa_sinks, m_sc_w_in, m_sc_conv_w, m_sc_w_out, m_fox_w_in, m_fox_b_f, m_fox_w_out, m_ffn_w_up, m_ffn_conv_w, m_ffn_conv_b, m_ffn_w_down, v_mix_pre_g, v_mix_post_g, v_ffn_pre_g, v_ffn_post_g, v_hgrn_w_in, v_hgrn_w_out, v_hgrn_norm_g, v_hgrn_lb_param, v_swa_w_in, v_swa_w_out, v_swa_sinks, v_sc_w_in, v_sc_conv_w, v_sc_w_out, v_fox_w_in, v_fox_b_f, v_fox_w_out, v_ffn_w_up, v_ffn_conv_w, v_ffn_conv_b, v_ffn_w_down):
    given = dict(locals())
    sharded = {n: given[n] for n in SHARDED}
    repl = {n: given[n] for n in REPLICATED}
    loss, (g_sharded, g_repl, grad_x) = jax.value_and_grad(_local_loss, argnums=(0, 1, 2))(
        sharded, repl, x, positions, loss_target)
    loss, g_repl = _sum_replicated(loss, g_repl)
    grads = {**g_sharded, **g_repl}
    delta, new_m, new_v = {}, {}, {}
    for n in WEIGHTS:
        delta[n], new_m[n], new_v[n] = _adamw(given[n], grads[n], given["m_" + n], given["v_" + n], "adamw_" + n)
    return (loss, grad_x, *[grads[n] for n in WEIGHTS], *[delta[n] for n in WEIGHTS],
            *[new_m[n] for n in WEIGHTS], *[new_v[n] for n in WEIGHTS])
```

```python
import functools
import math

import jax
import jax.numpy as jnp
from jax import lax
from jax.experimental import pallas as pl
from jax.experimental.pallas import tpu as pltpu

N_DEV = 8
MESH = pl.DeviceIdType.MESH
HBM_SPEC = pl.BlockSpec(memory_space=pltpu.HBM)
PACK_COLS = 512
PACK_ROW_BLOCK = 512
PACK_TILE_ROWS = 16

MM_TILE = 1024
MM_K_SINGLE = 2048
MM_K_TILE = 1536
MM_VMEM_BYTES = 48 * 1024 * 1024
ROW_BLOCK = 256
COL_BLOCK = 512
HALO = 8
ATT_BLOCK = 512
FOX_HEADS_PER_STEP = 2
HGRN_SUB = 16
HGRN_ROWS = 256
NEG = -1e30

RMS_EPS = 1e-6
HGRN_EXPAND = 128
SWA_HEAD_DIM = 64
SWA_GROUP = 8
SWA_WINDOW = 128
FOX_HEAD_DIM = 64
ROPE_THETA = 500000.0
ROT_DIM = SWA_HEAD_DIM // 4

ADAM_LR = 0.001
ADAM_B1 = 0.9
ADAM_B2 = 0.999
ADAM_EPS = 1e-08
ADAM_WD = 0.01
ADAM_STEP = 10

_NT = (((1,), (1,)), ((), ()))
_NN = (((1,), (0,)), ((), ()))
_TN = (((0,), (0,)), ((), ()))


def _tile(dim, pref):
    if dim <= pref:
        return dim
    t = pref - pref % 128
    while t >= 128:
        if dim % t == 0:
            return t
        t -= 128
    return dim


def _dot(a, b, dims):
    return lax.dot_general(a.astype(jnp.bfloat16), b.astype(jnp.bfloat16), dims, preferred_element_type=jnp.float32)


def _sigmoid(x):
    return 1.0 / (1.0 + jnp.exp(-x))


def _col_to_row(c):
    n = c.shape[0]
    eye = lax.broadcasted_iota(jnp.int32, (n, n), 0) == lax.broadcasted_iota(jnp.int32, (n, n), 1)
    return jnp.sum(jnp.where(eye, c, 0.0), axis=0, keepdims=True)


def _row_to_col(r):
    n = r.shape[1]
    eye = lax.broadcasted_iota(jnp.int32, (n, n), 0) == lax.broadcasted_iota(jnp.int32, (n, n), 1)
    return jnp.sum(jnp.where(eye, r, 0.0), axis=1, keepdims=True)


def _all_gather(x, name):
    r, c = x.shape

    def body(x_ref, out_ref, send_sems, recv_sems, local_sem):
        mx, my, mc = lax.axis_index("x"), lax.axis_index("y"), lax.axis_index("c")
        me, sibling = (mx, my, mc), (mx, my, 1 - mc)
        chips = [(1 - mx, my), (mx, 1 - my), (1 - mx, 1 - my)]

        def rows(px, py, pc):
            return out_ref.at[4 * px + 2 * py + pc]

        def copy(k, block, to, src=None):
            return pltpu.make_async_remote_copy(
                src_ref=rows(*block) if src is None else src, dst_ref=rows(*block),
                send_sem=send_sems.at[k], recv_sem=recv_sems.at[k], device_id=to, device_id_type=MESH)

        mine = pltpu.make_async_copy(x_ref, rows(*me), local_sem)
        mine.start()
        first = [copy(0, me, sibling, src=x_ref)]
        first += [copy(1 + j, me, (*chip, mc), src=x_ref) for j, chip in enumerate(chips)]
        for cp in first:
            cp.start()
        passed = [copy(4 + j, (*chip, mc), sibling) for j, chip in enumerate(chips)]
        for j, chip in enumerate(chips):
            copy(1 + j, (*chip, mc), me).wait_recv()
            passed[j].start()
        copy(0, sibling, me).wait_recv()
        for j, chip in enumerate(chips):
            copy(4 + j, (*chip, 1 - mc), me).wait_recv()
        for cp in first + passed:
            cp.wait_send()
        mine.wait()

    return pl.pallas_call(
        body, name=name, out_shape=jax.ShapeDtypeStruct((N_DEV, r, c), x.dtype),
        in_specs=[HBM_SPEC], out_specs=HBM_SPEC,
        scratch_shapes=[pltpu.SemaphoreType.DMA((7,)), pltpu.SemaphoreType.DMA((7,)), pltpu.SemaphoreType.DMA],
    )(x)


def _pair_exchange(x, name):
    n = x.shape[0] // 2

    def body(x_ref, out_ref, send_sems, recv_sems):
        mc = lax.axis_index("c")
        sibling = (lax.axis_index("x"), lax.axis_index("y"), 1 - mc)
        copies = [pltpu.make_async_remote_copy(
            src_ref=x_ref.at[2 * j + 1 - mc], dst_ref=out_ref.at[j], send_sem=send_sems.at[j],
            recv_sem=recv_sems.at[j], device_id=sibling, device_id_type=MESH) for j in range(n)]
        for cp in copies:
            cp.start()
        for cp in copies:
            cp.wait()

    return pl.pallas_call(
        body, name=name, out_shape=jax.ShapeDtypeStruct((n,) + x.shape[1:], x.dtype),
        in_specs=[HBM_SPEC], out_specs=HBM_SPEC,
        scratch_shapes=[pltpu.SemaphoreType.DMA((n,)), pltpu.SemaphoreType.DMA((n,))],
    )(x)


def _chip_exchange(x, name):
    flips = [(1, 0), (0, 1), (1, 1)]

    def body(x_ref, out_ref, send_sems, recv_sems, local_sem):
        mx, my, mc = lax.axis_index("x"), lax.axis_index("y"), lax.axis_index("c")
        me = 2 * mx + my

        def chip(flip):
            return ((1 - mx) if flip[0] else mx, (1 - my) if flip[1] else my)

        def copy(k, dst_block):
            px, py = chip(flips[k])
            return pltpu.make_async_remote_copy(
                src_ref=x_ref.at[2 * px + py], dst_ref=out_ref.at[dst_block],
                send_sem=send_sems.at[k], recv_sem=recv_sems.at[k], device_id=(px, py, mc), device_id_type=MESH)

        mine = pltpu.make_async_copy(x_ref.at[me], out_ref.at[me], local_sem)
        mine.start()
        sends = [copy(k, me) for k in range(len(flips))]
        for cp in sends:
            cp.start()
        for k in range(len(flips)):
            px, py = chip(flips[k])
            copy(k, 2 * px + py).wait_recv()
        for cp in sends:
            cp.wait_send()
        mine.wait()

    return pl.pallas_call(
        body, name=name, out_shape=jax.ShapeDtypeStruct(x.shape, x.dtype),
        in_specs=[HBM_SPEC], out_specs=HBM_SPEC,
        scratch_shapes=[pltpu.SemaphoreType.DMA((3,)), pltpu.SemaphoreType.DMA((3,)), pltpu.SemaphoreType.DMA],
    )(x)


def _sum_blocks(x, name, out_dtype=jnp.float32):
    n, r, c = x.shape
    tr = PACK_ROW_BLOCK if r % PACK_ROW_BLOCK == 0 else r

    def body(x_ref, o_ref):
        acc = x_ref[0].astype(jnp.float32)
        for j in range(1, n):
            acc = acc + x_ref[j].astype(jnp.float32)
        o_ref[...] = acc.astype(o_ref.dtype)

    return pl.pallas_call(
        body, name=name, grid=(r // tr,), out_shape=jax.ShapeDtypeStruct((r, c), out_dtype),
        in_specs=[pl.BlockSpec((n, tr, c), lambda i: (0, i, 0))],
        out_specs=pl.BlockSpec((tr, c), lambda i: (i, 0)),
    )(x)


def _pair_sum(x, got, name):
    n, _, r, c = x.shape
    tr = PACK_ROW_BLOCK if r % PACK_ROW_BLOCK == 0 else r

    def body(x_ref, g_ref, o_ref):
        mine = jnp.where(lax.axis_index("c") == 0, x_ref[0, 0], x_ref[0, 1])
        o_ref[0] = (mine.astype(jnp.float32) + g_ref[0].astype(jnp.float32)).astype(o_ref.dtype)

    spec = pl.BlockSpec((1, tr, c), lambda j, i: (j, i, 0))
    return pl.pallas_call(
        body, name=name, grid=(n, r // tr), out_shape=jax.ShapeDtypeStruct(got.shape, got.dtype),
        in_specs=[pl.BlockSpec((1, 2, tr, c), lambda j, i: (j, 0, i, 0)), spec], out_specs=spec,
    )(x, got)


def _reduce_scatter(x, name):
    _, r, c = x.shape
    paired = _pair_sum(x.reshape(4, 2, r, c), _pair_exchange(x, name + "_pair"), name + "_pair_sum")
    return _sum_blocks(_chip_exchange(paired, name + "_chips"), name + "_sum")


def _matmul(a, b, mode, name, out_dtype=jnp.float32):
    if mode == "nn":
        (m, k), (_, n) = a.shape, b.shape
    elif mode == "nt":
        (m, k), (n, _) = a.shape, b.shape
    else:
        (k, m), (_, n) = a.shape, b.shape
    tm, tn = _tile(m, MM_TILE), _tile(n, MM_TILE)
    tk = k if k <= MM_K_SINGLE else _tile(k, MM_K_TILE)
    nk = k // tk
    dims = {"nn": _NN, "nt": _NT, "tn": _TN}[mode]
    rows_inner = nk == 1 and (n // tn) * a.size < (m // tm) * b.size

    def body(a_ref, b_ref, o_ref, *scratch):
        part = lax.dot_general(a_ref[...], b_ref[...], dims, preferred_element_type=jnp.float32)
        if nk == 1:
            o_ref[...] = part.astype(o_ref.dtype)
            return
        acc_ref, = scratch
        kk = pl.program_id(2)

        @pl.when(kk == 0)
        def _():
            acc_ref[...] = part

        @pl.when(kk > 0)
        def _():
            acc_ref[...] += part

        @pl.when(kk == nk - 1)
        def _():
            o_ref[...] = acc_ref[...].astype(o_ref.dtype)

    def ij(g0, g1):
        return (g1, g0) if rows_inner else (g0, g1)

    def a_map(g0, g1, kk):
        i, _ = ij(g0, g1)
        return (kk, i) if mode == "tn" else (i, kk)

    def b_map(g0, g1, kk):
        _, j = ij(g0, g1)
        return (j, kk) if mode == "nt" else (kk, j)

    def o_map(g0, g1, kk):
        return ij(g0, g1)

    grid = (n // tn, m // tm, nk) if rows_inner else (m // tm, n // tn, nk)
    return pl.pallas_call(
        body, name=name, grid=grid, out_shape=jax.ShapeDtypeStruct((m, n), out_dtype),
        in_specs=[pl.BlockSpec((tk, tm) if mode == "tn" else (tm, tk), a_map),
                  pl.BlockSpec((tn, tk) if mode == "nt" else (tk, tn), b_map)],
        out_specs=pl.BlockSpec((tm, tn), o_map),
        scratch_shapes=[pltpu.VMEM((tm, tn), jnp.float32)] if nk > 1 else [],
        compiler_params=pltpu.CompilerParams(dimension_semantics=("parallel", "parallel", "arbitrary"),
                                             vmem_limit_bytes=MM_VMEM_BYTES),
    )(a, b)


def _piece_rows(shape, kind, gathering):
    size = math.prod(shape) * (2 if kind == "exact" and gathering else 1)
    rows = -(-size // PACK_COLS)
    return rows + (-rows) % PACK_TILE_ROWS


def _as_rows(x, rows, lead):
    n = x.shape[-1]
    if n != rows * PACK_COLS:
        x = jnp.pad(x, [(0, 0)] * len(lead) + [(0, rows * PACK_COLS - n)])
    return x.reshape(tuple(lead) + (rows, PACK_COLS))


def _stack_rows(parts, axis):
    total = sum(p.shape[axis] for p in parts)
    fill = (-total) % PACK_ROW_BLOCK
    if fill:
        shape = list(parts[0].shape)
        shape[axis] = fill
        parts = parts + [jnp.zeros(shape, parts[0].dtype)]
    return jnp.concatenate(parts, axis=axis)


def _gather_pack(shards, kinds):
    parts = []
    for s, kind in zip(shards, kinds):
        if kind == "exact":
            flat = lax.bitcast_convert_type(s, jnp.bfloat16).reshape(-1)
        elif kind == "colT":
            flat = s.T.astype(jnp.bfloat16).reshape(-1)
        else:
            flat = s.astype(jnp.bfloat16).reshape(-1)
        parts.append(_as_rows(flat, _piece_rows(s.shape, kind, True), ()))
    return _stack_rows(parts, 0)


def _gather_unpack(gathered, shapes, kinds):
    out, off = [], 0
    for shape, kind in zip(shapes, kinds):
        size = math.prod(shape) * (2 if kind == "exact" else 1)
        rows = _piece_rows(shape, kind, True)
        piece = gathered[:, off:off + rows]
        off += rows
        if size != rows * PACK_COLS:
            piece = piece.reshape(N_DEV, rows * PACK_COLS)[:, :size]
        if kind == "colT":
            out.append(piece.reshape(N_DEV * shape[1], shape[0]))
            continue
        if kind == "exact":
            piece = lax.bitcast_convert_type(piece.reshape((N_DEV,) + tuple(shape) + (2,)), jnp.float32)
            kind = "col"
        else:
            piece = piece.reshape((N_DEV,) + tuple(shape))
        if kind == "col":
            piece = jnp.moveaxis(piece, 0, -2)
            piece = piece.reshape(piece.shape[:-2] + (N_DEV * shape[-1],))
        else:
            piece = piece.reshape((N_DEV * shape[0],) + tuple(shape[1:]))
        out.append(piece)
    return tuple(out)


def _scatter_pack(cts, shapes, kinds):
    parts = []
    for ct, shape, kind in zip(cts, shapes, kinds):
        ct = ct.astype(jnp.bfloat16)
        if kind not in ("row", "colT"):
            ct = jnp.moveaxis(ct.reshape(tuple(shape[:-1]) + (N_DEV, shape[-1])), -2, 0)
        rows = _piece_rows(shape, kind, False)
        if math.prod(shape) == rows * PACK_COLS:
            parts.append(ct.reshape(N_DEV, rows, PACK_COLS))
        else:
            parts.append(_as_rows(ct.reshape(N_DEV, -1), rows, (N_DEV,)))
    return _stack_rows(parts, 1)


def _scatter_unpack(summed, shapes, kinds):
    out, off = [], 0
    for shape, kind in zip(shapes, kinds):
        size = math.prod(shape)
        rows = _piece_rows(shape, kind, False)
        piece = summed[off:off + rows]
        off += rows
        if size != rows * PACK_COLS:
            piece = piece.reshape(-1)[:size]
        out.append(piece.reshape(shape[1], shape[0]).T if kind == "colT" else piece.reshape(shape))
    return tuple(out)


def _make_layer_gather(shapes, kinds, tag):
    shapes = tuple(tuple(s) for s in shapes)

    def impl(shards):
        return _gather_unpack(_all_gather(_gather_pack(shards, kinds), tag + "_gather"), shapes, kinds)

    gather = jax.custom_vjp(impl)

    def fwd(shards):
        return impl(shards), None

    def bwd(_, cts):
        packed = _scatter_pack(cts, shapes, kinds)
        return (_scatter_unpack(_reduce_scatter(packed, tag + "_scatter"), shapes, kinds),)

    gather.defvjp(fwd, bwd)
    return gather


def _adamw(w, g, m, v, name):
    shape = w.shape
    cols = shape[-1]
    rows = math.prod(shape[:-1])
    tr = rows
    if rows > 512:
        for cand in (256, 128, 64, 32, 16, 8):
            if rows % cand == 0:
                tr = cand
                break
    c1 = 1.0 - ADAM_B1 ** ADAM_STEP
    c2 = 1.0 - ADAM_B2 ** ADAM_STEP

    def body(w_ref, g_ref, m_ref, v_ref, d_ref, nm_ref, nv_ref):
        gg = g_ref[...]
        nm = ADAM_B1 * m_ref[...] + (1.0 - ADAM_B1) * gg
        nv = ADAM_B2 * v_ref[...] + (1.0 - ADAM_B2) * (gg * gg)
        m_hat = nm / c1
        v_hat = nv / c2
        d_ref[...] = -ADAM_LR * (m_hat / (jnp.sqrt(v_hat) + ADAM_EPS) + ADAM_WD * w_ref[...])
        nm_ref[...] = nm
        nv_ref[...] = nv

    spec = pl.BlockSpec((tr, cols), lambda i: (i, 0))
    outs = pl.pallas_call(
        body, name=name, grid=(rows // tr,), out_shape=[jax.ShapeDtypeStruct((rows, cols), jnp.float32)] * 3,
        in_specs=[spec] * 4, out_specs=[spec] * 3,
    )(*[t.reshape(rows, cols) for t in (w, g, m, v)])
    return tuple(o.reshape(shape) for o in outs)


def _rms_fwd_call(x, g, res, name):
    t, d = x.shape
    tr = _tile(t, ROW_BLOCK)
    with_res = res is not None

    def body(*refs):
        if with_res:
            x_ref, g_ref, r_ref, y_ref = refs
        else:
            x_ref, g_ref, y_ref = refs
        xv = x_ref[...]
        rstd = lax.rsqrt(jnp.mean(xv * xv, axis=-1, keepdims=True) + RMS_EPS)
        y = xv * rstd * g_ref[...]
        if with_res:
            y = y + r_ref[...]
        y_ref[...] = y

    row = pl.BlockSpec((tr, d), lambda i: (i, 0))
    gspec = pl.BlockSpec((1, d), lambda i: (0, 0))
    ins = [x, g] + ([res] if with_res else [])
    return pl.pallas_call(
        body, name=name, grid=(t // tr,), out_shape=jax.ShapeDtypeStruct((t, d), jnp.float32),
        in_specs=[row, gspec] + ([row] if with_res else []), out_specs=row,
    )(*ins)


def _rms_bwd_call(x, g, dy, name):
    t, d = x.shape
    tr = _tile(t, ROW_BLOCK)

    def body(x_ref, g_ref, dy_ref, dx_ref, dg_ref):
        i = pl.program_id(0)
        xv = x_ref[...]
        rstd = lax.rsqrt(jnp.mean(xv * xv, axis=-1, keepdims=True) + RMS_EPS)
        xh = xv * rstd
        dyv = dy_ref[...]
        dyg = dyv * g_ref[...]
        dx_ref[...] = rstd * (dyg - xh * jnp.mean(dyg * xh, axis=-1, keepdims=True))

        @pl.when(i == 0)
        def _():
            dg_ref[...] = jnp.zeros_like(dg_ref)

        dg_ref[...] += jnp.sum(dyv * xh, axis=0, keepdims=True)

    row = pl.BlockSpec((tr, d), lambda i: (i, 0))
    gspec = pl.BlockSpec((1, d), lambda i: (0, 0))
    return pl.pallas_call(
        body, name=name, grid=(t // tr,),
        out_shape=[jax.ShapeDtypeStruct((t, d), jnp.float32), jax.ShapeDtypeStruct((1, d), jnp.float32)],
        in_specs=[row, gspec, row], out_specs=[row, gspec],
        compiler_params=pltpu.CompilerParams(dimension_semantics=("arbitrary",)),
    )(x, g, dy)


@functools.partial(jax.custom_vjp, nondiff_argnums=(3,))
def _norm_add(y, g, res, name):
    return _rms_fwd_call(y, g.reshape(1, -1), res, name + "_fwd")


def _norm_add_fwd(y, g, res, name):
    return _rms_fwd_call(y, g.reshape(1, -1), res, name + "_fwd"), (y, g)


def _norm_add_bwd(name, saved, dout):
    y, g = saved
    dy, dg = _rms_bwd_call(y, g.reshape(1, -1), dout, name + "_bwd")
    return dy, dg.reshape(g.shape), dout


_norm_add.defvjp(_norm_add_fwd, _norm_add_bwd)


@functools.partial(jax.custom_vjp, nondiff_argnums=(2,))
def _norm(x, g, name):
    return _rms_fwd_call(x, g.reshape(1, -1), None, name + "_fwd")


def _norm_fwd(x, g, name):
    return _rms_fwd_call(x, g.reshape(1, -1), None, name + "_fwd"), (x, g)


def _norm_bwd(name, saved, dout):
    x, g = saved
    dx, dg = _rms_bwd_call(x, g.reshape(1, -1), dout, name + "_bwd")
    return dx, dg.reshape(g.shape)


_norm.defvjp(_norm_fwd, _norm_bwd)


def _shift_down(x, before1, before2, row):
    xm1 = jnp.where(row == 0, before1, pltpu.roll(x, 1, 0))
    xm2 = jnp.where(row == 0, before2, jnp.where(row == 1, before1, pltpu.roll(x, 2, 0)))
    return xm1, xm2


def _shift_up(x, after1, after2, row):
    n = x.shape[0]
    xp1 = jnp.where(row == n - 1, after1, pltpu.roll(x, n - 1, 0))
    xp2 = jnp.where(row == n - 1, after2, jnp.where(row == n - 2, after1, pltpu.roll(x, n - 2, 0)))
    return xp1, xp2


def _ffn_act_fwd_call(u, conv_w, conv_b, name):
    t, f2 = u.shape
    f = f2 // 2
    tr, tc = _tile(t, ROW_BLOCK), _tile(f, COL_BLOCK)
    nj, hb = f // tc, tr // HALO

    def body(ug_ref, uu_ref, hg_ref, hu_ref, wg_ref, wu_ref, bg_ref, bu_ref, cg_ref, cu_ref, a_ref):
        i = pl.program_id(0)
        row = lax.broadcasted_iota(jnp.int32, (tr, tc), 0)

        def conv(x_ref, h_ref, w_ref, b_ref):
            xv = x_ref[...]
            h1 = jnp.where(i > 0, h_ref[7:8, :], 0.0)
            h2 = jnp.where(i > 0, h_ref[6:7, :], 0.0)
            xm1, xm2 = _shift_down(xv, h1, h2, row)
            return w_ref[0:1, :] * xm2 + w_ref[1:2, :] * xm1 + w_ref[2:3, :] * xv + b_ref[...]

        cg = conv(ug_ref, hg_ref, wg_ref, bg_ref)
        cu = conv(uu_ref, hu_ref, wu_ref, bu_ref)
        cg_ref[...] = cg
        cu_ref[...] = cu
        a_ref[...] = (cg * _sigmoid(cg) * cu).astype(jnp.bfloat16)

    def blk(off):
        return pl.BlockSpec((tr, tc), lambda i, j: (i, j + off))

    def halo(off):
        return pl.BlockSpec((HALO, tc), lambda i, j: (jnp.maximum(i * hb - 1, 0), j + off))

    def wspec(rows, off):
        return pl.BlockSpec((rows, tc), lambda i, j: (0, j + off))

    out_blk = pl.BlockSpec((tr, tc), lambda i, j: (i, j))
    return pl.pallas_call(
        body, name=name, grid=(t // tr, nj),
        out_shape=[jax.ShapeDtypeStruct((t, f), jnp.float32), jax.ShapeDtypeStruct((t, f), jnp.float32),
                   jax.ShapeDtypeStruct((t, f), jnp.bfloat16)],
        in_specs=[blk(0), blk(nj), halo(0), halo(nj), wspec(3, 0), wspec(3, nj), wspec(1, 0), wspec(1, nj)],
        out_specs=[out_blk, out_blk, out_blk],
    )(u, u, u, u, conv_w, conv_w, conv_b, conv_b)


def _ffn_act_bwd_call(da, cg, cu, u, conv_w, name):
    t, f = da.shape
    tr, tc = _tile(t, ROW_BLOCK), _tile(f, COL_BLOCK)
    nj, hb, ni = f // tc, tr // HALO, t // tr

    def body(da_ref, cg_ref, cu_ref, nda_ref, ncg_ref, ncu_ref, u_ref, hu_ref, w_ref, du_ref, dw_ref, db_ref):
        j, i = pl.program_id(0), pl.program_id(1)
        is_gate = j < nj
        row = lax.broadcasted_iota(jnp.int32, (tr, tc), 0)

        def duc_of(dav, cgv, cuv):
            sg = _sigmoid(cgv)
            d_gate = dav * cuv * (sg * (1.0 + cgv * (1.0 - sg)))
            d_up = dav * (cgv * sg)
            return jnp.where(is_gate, d_gate, d_up)

        duc = duc_of(da_ref[...], cg_ref[...], cu_ref[...])
        n1 = jnp.where(i < ni - 1, duc_of(nda_ref[0:1, :], ncg_ref[0:1, :], ncu_ref[0:1, :]), 0.0)
        n2 = jnp.where(i < ni - 1, duc_of(nda_ref[1:2, :], ncg_ref[1:2, :], ncu_ref[1:2, :]), 0.0)
        dp1, dp2 = _shift_up(duc, n1, n2, row)
        du_ref[...] = (w_ref[2:3, :] * duc + w_ref[1:2, :] * dp1 + w_ref[0:1, :] * dp2).astype(jnp.bfloat16)

        uv = u_ref[...]
        h1 = jnp.where(i > 0, hu_ref[7:8, :], 0.0)
        h2 = jnp.where(i > 0, hu_ref[6:7, :], 0.0)
        um1, um2 = _shift_down(uv, h1, h2, row)

        @pl.when(i == 0)
        def _():
            dw_ref[...] = jnp.zeros_like(dw_ref)
            db_ref[...] = jnp.zeros_like(db_ref)

        rows8 = lax.broadcasted_iota(jnp.int32, (HALO, tc), 0)
        s0 = jnp.sum(duc * um2, axis=0, keepdims=True)
        s1 = jnp.sum(duc * um1, axis=0, keepdims=True)
        s2 = jnp.sum(duc * uv, axis=0, keepdims=True)
        dw_ref[...] += jnp.where(rows8 == 0, s0, jnp.where(rows8 == 1, s1, jnp.where(rows8 == 2, s2, 0.0)))
        db_ref[...] += jnp.sum(duc, axis=0, keepdims=True)

    half = pl.BlockSpec((tr, tc), lambda j, i: (i, j % nj))
    nhalf = pl.BlockSpec((HALO, tc), lambda j, i: (jnp.minimum((i + 1) * hb, t // HALO - 1), j % nj))
    full = pl.BlockSpec((tr, tc), lambda j, i: (i, j))
    hfull = pl.BlockSpec((HALO, tc), lambda j, i: (jnp.maximum(i * hb - 1, 0), j))
    return pl.pallas_call(
        body, name=name, grid=(2 * nj, ni),
        out_shape=[jax.ShapeDtypeStruct((t, 2 * f), jnp.bfloat16), jax.ShapeDtypeStruct((HALO, 2 * f), jnp.float32),
                   jax.ShapeDtypeStruct((1, 2 * f), jnp.float32)],
        in_specs=[half, half, half, nhalf, nhalf, nhalf, full, hfull, pl.BlockSpec((3, tc), lambda j, i: (0, j))],
        out_specs=[full, pl.BlockSpec((HALO, tc), lambda j, i: (0, j)), pl.BlockSpec((1, tc), lambda j, i: (0, j))],
        compiler_params=pltpu.CompilerParams(dimension_semantics=("parallel", "arbitrary")),
    )(da, cg, cu, da, cg, cu, u, u, conv_w)


@functools.partial(jax.custom_vjp, nondiff_argnums=(5,))
def _ffn_block(h, w_up, conv_w, conv_b, w_down, name):
    return _ffn_block_fwd(h, w_up, conv_w, conv_b, w_down, name)[0]


def _ffn_block_fwd(h, w_up, conv_w, conv_b, w_down, name):
    hb = h.astype(jnp.bfloat16)
    u = _matmul(hb, w_up, "nn", name + "_up")
    cg, cu, a = _ffn_act_fwd_call(u, conv_w, conv_b.reshape(1, -1), name + "_act")
    return _matmul(a, w_down, "nn", name + "_down"), (hb, u, cg, cu, a, w_up, conv_w, conv_b, w_down)


def _ffn_block_bwd(name, saved, dy):
    hb, u, cg, cu, a, w_up, conv_w, conv_b, w_down = saved
    dyb = dy.astype(jnp.bfloat16)
    da = _matmul(dyb, w_down, "nt", name + "_down_dx")
    dw_down = _matmul(a, dyb, "tn", name + "_down_dw", jnp.bfloat16)
    du, dcw, dcb = _ffn_act_bwd_call(da, cg, cu, u, conv_w, name + "_act_bwd")
    dh = _matmul(du, w_up, "nt", name + "_up_dx")
    dw_up = _matmul(hb, du, "tn", name + "_up_dw", jnp.bfloat16)
    return dh, dw_up, dcw[:3], dcb.reshape(conv_b.shape), dw_down


_ffn_block.defvjp(_ffn_block_fwd, _ffn_block_bwd)


def _sconv_fwd_call(z, conv_w, name):
    t, d3 = z.shape
    d = d3 // 3
    tr, tc = _tile(t, ROW_BLOCK), _tile(d, COL_BLOCK)
    nj, hb = d // tc, tr // HALO

    def body(b_ref, c_ref, x_ref, hc_ref, hx_ref, w_ref, a_ref):
        i = pl.program_id(0)
        row = lax.broadcasted_iota(jnp.int32, (tr, tc), 0)
        p = c_ref[...] * x_ref[...]
        h1 = jnp.where(i > 0, hc_ref[7:8, :] * hx_ref[7:8, :], 0.0)
        h2 = jnp.where(i > 0, hc_ref[6:7, :] * hx_ref[6:7, :], 0.0)
        pm1, pm2 = _shift_down(p, h1, h2, row)
        pc = w_ref[0:1, :] * pm2 + w_ref[1:2, :] * pm1 + w_ref[2:3, :] * p
        a_ref[...] = (b_ref[...] * pc).astype(jnp.bfloat16)

    def blk(off):
        return pl.BlockSpec((tr, tc), lambda i, j: (i, j + off))

    def halo(off):
        return pl.BlockSpec((HALO, tc), lambda i, j: (jnp.maximum(i * hb - 1, 0), j + off))

    return pl.pallas_call(
        body, name=name, grid=(t // tr, nj), out_shape=jax.ShapeDtypeStruct((t, d), jnp.bfloat16),
        in_specs=[blk(0), blk(nj), blk(2 * nj), halo(nj), halo(2 * nj), pl.BlockSpec((3, tc), lambda i, j: (0, j))],
        out_specs=pl.BlockSpec((tr, tc), lambda i, j: (i, j)),
    )(z, z, z, z, z, conv_w)


def _sconv_bwd_call(da, z, conv_w, name):
    t, d = da.shape
    tr, tc = _tile(t, ROW_BLOCK), _tile(d, COL_BLOCK)
    nj, hb, ni = d // tc, tr // HALO, t // tr

    def body(da_ref, b_ref, c_ref, x_ref, hc_ref, hx_ref, nda_ref, nb_ref, w_ref, db_ref, dc_ref, dx_ref, dw_ref):
        i = pl.program_id(1)
        row = lax.broadcasted_iota(jnp.int32, (tr, tc), 0)
        cv, xv, dav = c_ref[...], x_ref[...], da_ref[...]
        p = cv * xv
        h1 = jnp.where(i > 0, hc_ref[7:8, :] * hx_ref[7:8, :], 0.0)
        h2 = jnp.where(i > 0, hc_ref[6:7, :] * hx_ref[6:7, :], 0.0)
        pm1, pm2 = _shift_down(p, h1, h2, row)
        pc = w_ref[0:1, :] * pm2 + w_ref[1:2, :] * pm1 + w_ref[2:3, :] * p
        db_ref[...] = (dav * pc).astype(jnp.bfloat16)
        dpc = dav * b_ref[...]
        n1 = jnp.where(i < ni - 1, nda_ref[0:1, :] * nb_ref[0:1, :], 0.0)
        n2 = jnp.where(i < ni - 1, nda_ref[1:2, :] * nb_ref[1:2, :], 0.0)
        dp1, dp2 = _shift_up(dpc, n1, n2, row)
        dp = w_ref[2:3, :] * dpc + w_ref[1:2, :] * dp1 + w_ref[0:1, :] * dp2
        dc_ref[...] = (dp * xv).astype(jnp.bfloat16)
        dx_ref[...] = (dp * cv).astype(jnp.bfloat16)

        @pl.when(i == 0)
        def _():
            dw_ref[...] = jnp.zeros_like(dw_ref)

        rows8 = lax.broadcasted_iota(jnp.int32, (HALO, tc), 0)
        s0 = jnp.sum(dpc * pm2, axis=0, keepdims=True)
        s1 = jnp.sum(dpc * pm1, axis=0, keepdims=True)
        s2 = jnp.sum(dpc * p, axis=0, keepdims=True)
        dw_ref[...] += jnp.where(rows8 == 0, s0, jnp.where(rows8 == 1, s1, jnp.where(rows8 == 2, s2, 0.0)))

    def blk(off):
        return pl.BlockSpec((tr, tc), lambda j, i: (i, j + off))

    def halo(off):
        return pl.BlockSpec((HALO, tc), lambda j, i: (jnp.maximum(i * hb - 1, 0), j + off))

    def nhalo(off):
        return pl.BlockSpec((HALO, tc), lambda j, i: (jnp.minimum((i + 1) * hb, t // HALO - 1), j + off))

    out = pl.BlockSpec((tr, tc), lambda j, i: (i, j))
    return pl.pallas_call(
        body, name=name, grid=(nj, ni),
        out_shape=[jax.ShapeDtypeStruct((t, d), jnp.bfloat16)] * 3 + [jax.ShapeDtypeStruct((HALO, d), jnp.float32)],
        in_specs=[blk(0), blk(0), blk(nj), blk(2 * nj), halo(nj), halo(2 * nj), nhalo(0), nhalo(0),
                  pl.BlockSpec((3, tc), lambda j, i: (0, j))],
        out_specs=[out, out, out, pl.BlockSpec((HALO, tc), lambda j, i: (0, j))],
        compiler_params=pltpu.CompilerParams(dimension_semantics=("parallel", "arbitrary")),
    )(da, z, z, z, z, z, da, z, conv_w)


@functools.partial(jax.custom_vjp, nondiff_argnums=(4,))
def _sconv_block(h, w_in, conv_w, w_out, name):
    return _sconv_block_fwd(h, w_in, conv_w, w_out, name)[0]


def _sconv_block_fwd(h, w_in, conv_w, w_out, name):
    hb = h.astype(jnp.bfloat16)
    z = _matmul(hb, w_in, "nn", name + "_in")
    a = _sconv_fwd_call(z, conv_w, name + "_gate")
    return _matmul(a, w_out, "nn", name + "_out"), (hb, z, a, w_in, conv_w, w_out)


def _sconv_block_bwd(name, saved, dy):
    hb, z, a, w_in, conv_w, w_out = saved
    dyb = dy.astype(jnp.bfloat16)
    da = _matmul(dyb, w_out, "nt", name + "_out_dx")
    dw_out = _matmul(a, dyb, "tn", name + "_out_dw", jnp.bfloat16)
    db, dc, dx, dcw = _sconv_bwd_call(da, z, conv_w, name + "_gate_bwd")
    dz = jnp.concatenate([db, dc, dx], axis=1)
    dh = _matmul(dz, w_in, "nt", name + "_in_dx")
    dw_in = _matmul(hb, dz, "tn", name + "_in_dw", jnp.bfloat16)
    return dh, dw_in, dcw[:3], dw_out


_sconv_block.defvjp(_sconv_block_fwd, _sconv_block_bwd)


def _to_heads(x, heads):
    t = x.shape[0]
    return x.reshape(t, heads, -1).transpose(1, 0, 2)


def _from_heads(x):
    h, t, d = x.shape
    return x.transpose(1, 0, 2).reshape(t, h * d)


def _partial_rope(x, positions):
    half = ROT_DIM // 2
    inv_freq = ROPE_THETA ** (-jnp.arange(half, dtype=jnp.float32) / half)
    ang = positions.astype(jnp.float32)[:, None] * inv_freq[None, :]
    cos = jnp.cos(ang)[:, None, :]
    sin = jnp.sin(ang)[:, None, :]
    x1, x2 = x[..., :half], x[..., half:ROT_DIM]
    return jnp.concatenate([x1 * cos - x2 * sin, x2 * cos + x1 * sin, x[..., ROT_DIM:]], axis=-1)


def _swa_probs(q_ref, kp_ref, kc_ref, sink_ref, n, w, scale):
    grp, d = q_ref.shape[0], q_ref.shape[2]
    q2 = q_ref[...].reshape(grp * w, d)
    kcat = jnp.concatenate([kp_ref[0], kc_ref[0]], axis=0)
    s3 = (_dot(q2, kcat, _NT) * scale).reshape(grp, w, 2 * w)
    qi = lax.broadcasted_iota(jnp.int32, s3.shape, 1)
    kj = lax.broadcasted_iota(jnp.int32, s3.shape, 2)
    diff = qi + w - kj
    allowed = (diff >= 0) & (diff < w) & ((kj >= w) | (n > 0))
    s3 = jnp.where(allowed, s3, NEG)
    sink = sink_ref[...]
    m = jnp.maximum(jnp.max(s3, axis=2, keepdims=True), sink)
    e = jnp.exp(s3 - m)
    es = jnp.exp(sink - m)
    z = jnp.sum(e, axis=2, keepdims=True) + es
    return q2, kcat, e / z, es / z


def _swa_fwd_call(q, k, v, sinks, name):
    qh, t, d = q.shape
    kv = k.shape[0]
    grp, w = qh // kv, SWA_WINDOW
    nblk = t // w
    scale = d ** -0.5

    def body(q_ref, kp_ref, kc_ref, vp_ref, vc_ref, sink_ref, o_ref):
        n = pl.program_id(1)
        _, _, p3, _ = _swa_probs(q_ref, kp_ref, kc_ref, sink_ref, n, w, scale)
        vcat = jnp.concatenate([vp_ref[0], vc_ref[0]], axis=0)
        o = _dot(p3.reshape(grp * w, 2 * w), vcat, _NN)
        o_ref[...] = o.reshape(grp, w, d).astype(o_ref.dtype)

    qspec = pl.BlockSpec((grp, w, d), lambda h, n: (h, n, 0))
    prev = pl.BlockSpec((1, w, d), lambda h, n: (h, jnp.maximum(n - 1, 0), 0))
    cur = pl.BlockSpec((1, w, d), lambda h, n: (h, n, 0))
    return pl.pallas_call(
        body, name=name, grid=(kv, nblk), out_shape=jax.ShapeDtypeStruct((qh, t, d), jnp.bfloat16),
        in_specs=[qspec, prev, cur, prev, cur, pl.BlockSpec((grp, 1, 1), lambda h, n: (h, 0, 0))],
        out_specs=qspec,
    )(q, k, k, v, v, sinks)


def _swa_bwd_call(q, k, v, sinks, do, name):
    qh, t, d = q.shape
    kv = k.shape[0]
    grp, w = qh // kv, SWA_WINDOW
    nblk = t // w
    scale = d ** -0.5

    def body(q_ref, kp_ref, kc_ref, vp_ref, vc_ref, sink_ref, do_ref, dq_ref, dk_ref, dv_ref, ds_ref,
             carry_k, carry_v, part_k, part_v):
        n = pl.program_id(1)

        @pl.when(n == 0)
        def _():
            carry_k[...] = jnp.zeros_like(carry_k)
            carry_v[...] = jnp.zeros_like(carry_v)
            ds_ref[...] = jnp.zeros_like(ds_ref)

        @pl.when(n < nblk)
        def _():
            q2, kcat, p3, ps = _swa_probs(q_ref, kp_ref, kc_ref, sink_ref, n, w, scale)
            vcat = jnp.concatenate([vp_ref[0], vc_ref[0]], axis=0)
            do2 = do_ref[...].reshape(grp * w, d)
            dp3 = _dot(do2, vcat, _NT).reshape(grp, w, 2 * w)
            rs = jnp.sum(p3 * dp3, axis=2, keepdims=True)
            ds2 = (p3 * (dp3 - rs)).reshape(grp * w, 2 * w)
            dsink = -jnp.sum(ps * rs, axis=1, keepdims=True)
            ds_ref[...] += jnp.broadcast_to(dsink, ds_ref.shape)
            dq_ref[...] = (_dot(ds2, kcat, _NN) * scale).reshape(grp, w, d)
            part_k[...] = _dot(ds2, q2, _TN) * scale
            part_v[...] = _dot(p3.reshape(grp * w, 2 * w), do2, _TN)

        @pl.when(n == nblk)
        def _():
            part_k[...] = jnp.zeros_like(part_k)
            part_v[...] = jnp.zeros_like(part_v)

        dk_ref[0] = carry_k[...] + part_k[0:w, :]
        dv_ref[0] = carry_v[...] + part_v[0:w, :]
        carry_k[...] = part_k[w:2 * w, :]
        carry_v[...] = part_v[w:2 * w, :]

    last = nblk - 1
    qspec = pl.BlockSpec((grp, w, d), lambda h, n: (h, jnp.minimum(n, last), 0))
    prev = pl.BlockSpec((1, w, d), lambda h, n: (h, jnp.maximum(jnp.minimum(n, last) - 1, 0), 0))
    cur = pl.BlockSpec((1, w, d), lambda h, n: (h, jnp.minimum(n, last), 0))
    kout = pl.BlockSpec((1, w, d), lambda h, n: (h, jnp.maximum(n - 1, 0), 0))
    return pl.pallas_call(
        body, name=name, grid=(kv, nblk + 1),
        out_shape=[jax.ShapeDtypeStruct((qh, t, d), jnp.float32), jax.ShapeDtypeStruct((kv, t, d), jnp.float32),
                   jax.ShapeDtypeStruct((kv, t, d), jnp.float32), jax.ShapeDtypeStruct((qh, 1, 128), jnp.float32)],
        in_specs=[qspec, prev, cur, prev, cur, pl.BlockSpec((grp, 1, 1), lambda h, n: (h, 0, 0)), qspec],
        out_specs=[qspec, kout, kout, pl.BlockSpec((grp, 1, 128), lambda h, n: (h, 0, 0))],
        scratch_shapes=[pltpu.VMEM((w, d), jnp.float32), pltpu.VMEM((w, d), jnp.float32),
                        pltpu.VMEM((2 * w, d), jnp.float32), pltpu.VMEM((2 * w, d), jnp.float32)],
        compiler_params=pltpu.CompilerParams(dimension_semantics=("parallel", "arbitrary")),
    )(q, k, k, v, v, sinks, do)


def _swa_split(z, positions, qh, kv, d):
    q, k, v = jnp.split(z, [qh * d, qh * d + kv * d], axis=-1)
    t = z.shape[0]
    q = _partial_rope(q.reshape(t, qh, d), positions).transpose(1, 0, 2)
    k = _partial_rope(k.reshape(t, kv, d), positions).transpose(1, 0, 2)
    v = v.reshape(t, kv, d).transpose(1, 0, 2)
    return q, k, v


@functools.partial(jax.custom_vjp, nondiff_argnums=(5,))
def _swa_block(h, positions, w_in, w_out, sinks, name):
    return _swa_block_fwd(h, positions, w_in, w_out, sinks, name)[0]


def _swa_block_fwd(h, positions, w_in, w_out, sinks, name):
    d = SWA_HEAD_DIM
    qh = h.shape[1] // d
    kv = qh // SWA_GROUP
    hb = h.astype(jnp.bfloat16)
    z = _matmul(hb, w_in, "nt", name + "_in")
    q, k, v = [a.astype(jnp.bfloat16) for a in _swa_split(z, positions, qh, kv, d)]
    o = _swa_fwd_call(q, k, v, sinks.reshape(qh, 1, 1), name + "_attn")
    a = _from_heads(o)
    return _matmul(a, w_out, "nn", name + "_out"), (hb, positions, z, q, k, v, a, w_in, w_out, sinks)


def _swa_block_bwd(name, saved, dy):
    hb, positions, z, q, k, v, a, w_in, w_out, sinks = saved
    d = SWA_HEAD_DIM
    qh = q.shape[0]
    kv = k.shape[0]
    dyb = dy.astype(jnp.bfloat16)
    da = _matmul(dyb, w_out, "nt", name + "_out_dx")
    dw_out = _matmul(a, dyb, "tn", name + "_out_dw", jnp.bfloat16)
    do = _to_heads(da, qh)
    dq, dk, dv, dsinks = _swa_bwd_call(q, k, v, sinks.reshape(qh, 1, 1), do, name + "_attn_bwd")
    _, split_vjp = jax.vjp(lambda zz: _swa_split(zz, positions, qh, kv, d), z)
    dz = split_vjp((dq, dk, dv))[0].astype(jnp.bfloat16)
    dh = _matmul(dz, w_in, "nn", name + "_in_dx")
    dw_in = _matmul(dz, hb, "tn", name + "_in_dw", jnp.bfloat16)
    return dh, None, dw_in, dw_out, dsinks[:, 0, 0].reshape(sinks.shape)


_swa_block.defvjp(_swa_block_fwd, _swa_block_bwd)


def _fox_specs(hp, nb, tb, d):
    blk = pl.BlockSpec((hp, 1, tb, d), lambda h, i: (h, i, 0, 0))
    whole = pl.BlockSpec((hp, nb, tb, d), lambda h, i: (h, 0, 0, 0))
    rowblk = pl.BlockSpec((hp, 1, 1, tb), lambda h, i: (h, i, 0, 0))
    rowwhole = pl.BlockSpec((hp, nb, 1, tb), lambda h, i: (h, 0, 0, 0))
    return blk, whole, rowblk, rowwhole


def _fox_fwd_call(q, k, v, g, bias, name):
    hh, nb, tb, d = q.shape
    scale = d ** -0.5

    hp = FOX_HEADS_PER_STEP

    def body(q_ref, k_ref, v_ref, g_ref, b_ref, og_ref, o_ref, lse_ref):
        i = pl.program_id(1)
        qs = [q_ref[a, 0] for a in range(hp)]
        rows = lax.broadcasted_iota(jnp.int32, (tb, tb), 0)
        cols = lax.broadcasted_iota(jnp.int32, (tb, tb), 1)

        def step(kb, carry, masked):
            out = []
            for a in range(hp):
                m, l, acc = carry[a]
                s = _dot(qs[a], k_ref[a, kb], _NT) + b_ref[a, kb]
                if masked:
                    s = jnp.where(cols <= rows, s, NEG)
                m_new = jnp.maximum(m, jnp.max(s, axis=1, keepdims=True))
                alpha = jnp.exp(m - m_new)
                p = jnp.exp(s - m_new)
                out.append((m_new, alpha * l + jnp.sum(p, axis=1, keepdims=True),
                            alpha * acc + _dot(p, v_ref[a, kb], _NN)))
            return tuple(out)

        init = (jnp.full((tb, 1), NEG, jnp.float32), jnp.zeros((tb, 1), jnp.float32), jnp.zeros((tb, d), jnp.float32))
        carry = lax.fori_loop(0, i, lambda kb, c: step(kb, c, False), (init,) * hp)
        for a, (m, l, acc) in enumerate(step(i, carry, True)):
            o = acc / l
            o_ref[a, 0] = o
            og_ref[a, 0] = (o * _sigmoid(g_ref[a, 0])).astype(og_ref.dtype)
            lse_ref[a, 0] = _col_to_row(m + jnp.log(l))

    blk, whole, rowblk, rowwhole = _fox_specs(hp, nb, tb, d)
    return pl.pallas_call(
        body, name=name, grid=(hh // hp, nb),
        out_shape=[jax.ShapeDtypeStruct(q.shape, jnp.bfloat16), jax.ShapeDtypeStruct(q.shape, jnp.float32),
                   jax.ShapeDtypeStruct((hh, nb, 1, tb), jnp.float32)],
        in_specs=[blk, whole, whole, blk, rowwhole],
        out_specs=[blk, blk, rowblk],
    )(q, k, v, g, bias)


def _fox_dq_call(q, k, v, g, o, dog, bias, lse, name):
    hh, nb, tb, d = q.shape
    scale = d ** -0.5

    hp = FOX_HEADS_PER_STEP

    def body(q_ref, k_ref, v_ref, g_ref, o_ref, dog_ref, b_ref, lse_ref, dq_ref, dg_ref, do_ref, delta_ref):
        i = pl.program_id(1)
        rows = lax.broadcasted_iota(jnp.int32, (tb, tb), 0)
        cols = lax.broadcasted_iota(jnp.int32, (tb, tb), 1)
        qs, dos, deltas, lses = [], [], [], []
        for a in range(hp):
            sig = _sigmoid(g_ref[a, 0])
            ov, dogv = o_ref[a, 0], dog_ref[a, 0]
            do = dogv * sig
            dg_ref[a, 0] = dogv * ov * sig * (1.0 - sig)
            do_ref[a, 0] = do.astype(do_ref.dtype)
            qs.append(q_ref[a, 0])
            dos.append(do.astype(jnp.bfloat16))
            deltas.append(jnp.sum(do * ov, axis=1, keepdims=True))
            lses.append(_row_to_col(lse_ref[a, 0]))

        def step(kb, carry, masked):
            out = []
            for a in range(hp):
                dq, spdp, sp = carry[a]
                s = _dot(qs[a], k_ref[a, kb], _NT) + b_ref[a, kb]
                p = jnp.exp(s - lses[a])
                if masked:
                    p = jnp.where(cols <= rows, p, 0.0)
                dp = _dot(dos[a], v_ref[a, kb], _NT)
                out.append((dq + _dot(p * (dp - deltas[a]), k_ref[a, kb], _NN),
                            spdp + jnp.sum(p * dp, axis=1, keepdims=True), sp + jnp.sum(p, axis=1, keepdims=True)))
            return tuple(out)

        zcol = jnp.zeros((tb, 1), jnp.float32)
        init = (jnp.zeros((tb, d), jnp.float32), zcol, zcol)
        carry = lax.fori_loop(0, i, lambda kb, c: step(kb, c, False), (init,) * hp)
        for a, (dq, spdp, sp) in enumerate(step(i, carry, True)):
            dq_ref[a, 0] = dq * scale
            delta_ref[a, 0] = _col_to_row(spdp / sp)

    blk, whole, rowblk, rowwhole = _fox_specs(hp, nb, tb, d)
    return pl.pallas_call(
        body, name=name, grid=(hh // hp, nb),
        out_shape=[jax.ShapeDtypeStruct(q.shape, jnp.float32), jax.ShapeDtypeStruct(q.shape, jnp.float32),
                   jax.ShapeDtypeStruct(q.shape, jnp.bfloat16), jax.ShapeDtypeStruct((hh, nb, 1, tb), jnp.float32)],
        in_specs=[blk, whole, whole, blk, blk, blk, rowwhole, rowblk],
        out_specs=[blk, blk, blk, rowblk],
    )(q, k, v, g, o, dog, bias, lse)


def _fox_dkv_call(q, k, v, do, bias, lse, delta, name):
    hh, nb, tb, d = q.shape
    scale = d ** -0.5

    hp = FOX_HEADS_PER_STEP

    def body(q_ref, k_ref, v_ref, do_ref, b_ref, lse_ref, delta_ref, dk_ref, dv_ref, db_ref):
        j = pl.program_id(1)
        ks = [k_ref[a, 0] for a in range(hp)]
        vs = [v_ref[a, 0] for a in range(hp)]
        bcols = [_row_to_col(b_ref[a, 0]) for a in range(hp)]
        rows = lax.broadcasted_iota(jnp.int32, (tb, tb), 0)
        cols = lax.broadcasted_iota(jnp.int32, (tb, tb), 1)

        def step(qb, carry, masked):
            out = []
            for a in range(hp):
                dk, dv, db = carry[a]
                st = _dot(ks[a], q_ref[a, qb], _NT) + bcols[a] - lse_ref[a, qb]
                pt = jnp.exp(st)
                if masked:
                    pt = jnp.where(rows <= cols, pt, 0.0)
                dpt = _dot(vs[a], do_ref[a, qb], _NT)
                dst = pt * (dpt - delta_ref[a, qb])
                out.append((dk + _dot(dst, q_ref[a, qb], _NN), dv + _dot(pt, do_ref[a, qb], _NN),
                            db + jnp.sum(dst, axis=1, keepdims=True)))
            return tuple(out)

        init = (jnp.zeros((tb, d), jnp.float32), jnp.zeros((tb, d), jnp.float32), jnp.zeros((tb, 1), jnp.float32))
        carry = step(j, (init,) * hp, True)
        for a, (dk, dv, db) in enumerate(lax.fori_loop(j + 1, nb, lambda qb, c: step(qb, c, False), carry)):
            dk_ref[a, 0] = dk
            dv_ref[a, 0] = dv
            db_ref[a, 0] = _col_to_row(db)

    blk, whole, rowblk, rowwhole = _fox_specs(hp, nb, tb, d)
    return pl.pallas_call(
        body, name=name, grid=(hh // hp, nb),
        out_shape=[jax.ShapeDtypeStruct(q.shape, jnp.float32), jax.ShapeDtypeStruct(q.shape, jnp.float32),
                   jax.ShapeDtypeStruct((hh, nb, 1, tb), jnp.float32)],
        in_specs=[whole, blk, blk, whole, rowblk, rowwhole, rowwhole],
        out_specs=[blk, blk, rowblk],
    )(q, k, v, do, bias, lse, delta)


def _fox_bias_fwd_call(fl, b_f, name):
    t, lanes = fl.shape
    tr = _tile(t, ROW_BLOCK)

    def body(fl_ref, b_ref, o_ref, carry):
        i = pl.program_id(0)

        @pl.when(i == 0)
        def _():
            carry[...] = jnp.zeros_like(carry)

        xv = fl_ref[...] + b_ref[...]
        ls = jnp.minimum(xv, 0.0) - jnp.log(1.0 + jnp.exp(-jnp.abs(xv)))
        pos = lax.broadcasted_iota(jnp.int32, (tr, lanes), 0)
        o_ref[...] = -(_group_cumsum(ls, tr, pos) + carry[...])
        carry[...] += jnp.sum(ls, axis=0, keepdims=True)

    blk = pl.BlockSpec((tr, lanes), lambda i: (i, 0))
    return pl.pallas_call(
        body, name=name, grid=(t // tr,), out_shape=jax.ShapeDtypeStruct((t, lanes), jnp.float32),
        in_specs=[blk, pl.BlockSpec((1, lanes), lambda i: (0, 0))], out_specs=blk,
        scratch_shapes=[pltpu.VMEM((1, lanes), jnp.float32)],
        compiler_params=pltpu.CompilerParams(dimension_semantics=("arbitrary",)),
    )(fl, b_f)


def _fox_bias_bwd_call(dbias, fl, b_f, name):
    t, lanes = fl.shape
    tr = _tile(t, ROW_BLOCK)
    last = t // tr - 1

    def body(db_ref, fl_ref, b_ref, dfl_ref, dbf_ref, carry):
        i = pl.program_id(0)

        @pl.when(i == 0)
        def _():
            carry[...] = jnp.zeros_like(carry)
            dbf_ref[...] = jnp.zeros_like(dbf_ref)

        dbv = db_ref[...]
        pos = lax.broadcasted_iota(jnp.int32, (tr, lanes), 0)
        dls = -(_group_rcumsum(dbv, tr, pos) + carry[...])
        carry[...] += jnp.sum(dbv, axis=0, keepdims=True)
        dfl = dls * _sigmoid(-(fl_ref[...] + b_ref[...]))
        dfl_ref[...] = dfl
        dbf_ref[...] += jnp.sum(dfl, axis=0, keepdims=True)

    blk = pl.BlockSpec((tr, lanes), lambda i: (last - i, 0))
    one = pl.BlockSpec((1, lanes), lambda i: (0, 0))
    return pl.pallas_call(
        body, name=name, grid=(t // tr,),
        out_shape=[jax.ShapeDtypeStruct((t, lanes), jnp.float32), jax.ShapeDtypeStruct((1, lanes), jnp.float32)],
        in_specs=[blk, blk, one], out_specs=[blk, one],
        scratch_shapes=[pltpu.VMEM((1, lanes), jnp.float32)],
        compiler_params=pltpu.CompilerParams(dimension_semantics=("arbitrary",)),
    )(dbias, fl, b_f)


@functools.partial(jax.custom_vjp, nondiff_argnums=(5,))
def _fox_block(h, w_qkvg, w_f, b_f, w_out, name):
    return _fox_block_fwd(h, w_qkvg, w_f, b_f, w_out, name)[0]


def _fox_block_fwd(h, w_qkvg, w_f, b_f, w_out, name):
    t, dm = h.shape
    d = FOX_HEAD_DIM
    heads = dm // d
    lanes = w_f.shape[0]
    tb = _tile(t, ATT_BLOCK)
    nb = t // tb
    hb = h.astype(jnp.bfloat16)
    z = _matmul(hb, w_qkvg, "nt", name + "_in")
    fl = _matmul(hb, w_f, "nt", name + "_inf")
    z4 = z.reshape(t, 4, heads, d).transpose(1, 2, 0, 3).reshape(4, heads, nb, tb, d)
    q = (z4[0] * d ** -0.5).astype(jnp.bfloat16)
    k, v = z4[1].astype(jnp.bfloat16), z4[2].astype(jnp.bfloat16)
    g = z4[3]
    b_fp = jnp.pad(b_f, (0, lanes - heads)).reshape(1, lanes)
    bias = _fox_bias_fwd_call(fl, b_fp, name + "_bias")[:, :heads].T.reshape(heads, nb, 1, tb)
    og, o, lse = _fox_fwd_call(q, k, v, g, bias, name + "_attn")
    a = _from_heads(og.reshape(heads, t, d))
    return _matmul(a, w_out, "nn", name + "_out"), (hb, q, k, v, g, bias, o, lse, a, fl, b_fp, w_qkvg, w_f, w_out)


def _fox_block_bwd(name, saved, dy):
    hb, q, k, v, g, bias, o, lse, a, fl, b_fp, w_qkvg, w_f, w_out = saved
    heads, nb, tb, d = q.shape
    t = nb * tb
    lanes = w_f.shape[0]
    dyb = dy.astype(jnp.bfloat16)
    da = _matmul(dyb, w_out, "nt", name + "_out_dx")
    dw_out = _matmul(a, dyb, "tn", name + "_out_dw", jnp.bfloat16)
    dog = _to_heads(da, heads).reshape(heads, nb, tb, d)
    dq, dg, do, delta = _fox_dq_call(q, k, v, g, o, dog, bias, lse, name + "_attn_dq")
    dk, dv, dbias = _fox_dkv_call(q, k, v, do, bias, lse, delta, name + "_attn_dkv")
    dz = jnp.stack([dq, dk, dv, dg]).reshape(4, heads, t, d).transpose(2, 0, 1, 3).reshape(t, 4 * heads * d)
    dz = dz.astype(jnp.bfloat16)
    dbias_rows = jnp.pad(dbias.reshape(heads, t).T, ((0, 0), (0, lanes - heads)))
    dfl, db_fp = _fox_bias_bwd_call(dbias_rows, fl, b_fp, name + "_bias_bwd")
    dflb = dfl.astype(jnp.bfloat16)
    dh = _matmul(dz, w_qkvg, "nn", name + "_in_dx") + _matmul(dflb, w_f, "nn", name + "_inf_dx")
    dw_qkvg = _matmul(dz, hb, "tn", name + "_in_dw", jnp.bfloat16)
    dw_f = _matmul(dflb, hb, "tn", name + "_inf_dw", jnp.bfloat16)
    return dh, dw_qkvg, dw_f, db_fp[0, :heads], dw_out


_fox_block.defvjp(_fox_block_fwd, _fox_block_bwd)


def _group_cumsum(x, c, pos):
    sh = 1
    while sh < c:
        x = x + jnp.where(pos >= sh, pltpu.roll(x, sh, 0), 0.0)
        sh *= 2
    return x


def _group_rcumsum(x, c, pos):
    n = x.shape[0]
    sh = 1
    while sh < c:
        x = x + jnp.where(pos < c - sh, pltpu.roll(x, n - sh, 0), 0.0)
        sh *= 2
    return x


def _hgrn_prep(zq_ref, zf_ref, zi_ref, lb_ref, b_scr, rows, c):
    nb = rows // c
    zq, zf = zq_ref[...], zf_ref[...]
    lbv = lb_ref[...]
    sq = _sigmoid(zq)
    sf = _sigmoid(zf)
    f = lbv + (1.0 - lbv) * sf
    pos = lax.broadcasted_iota(jnp.int32, (rows, 128), 0) % c
    b = _group_cumsum(jnp.log(f), c, pos)
    sh = (nb, c, 128)
    q3, k3, v3, b3 = (zq * sq).reshape(sh), (1.0 - f).reshape(sh), zi_ref[...].reshape(sh), b.reshape(sh)
    b_scr[...] = b3
    glast = b_scr[:, c - 1:c, :]
    eb = jnp.exp(b3)
    ek = jnp.exp(glast - b3)
    return dict(zq=zq, sq=sq, sf=sf, f=f, pos=pos, q3=q3, k3=k3, v3=v3, b3=b3, eb=eb, ek=ek,
                qt=q3 * eb, kh=k3 * ek, dec=jnp.exp(glast))


def _hgrn_fwd_call(z, lb, norm_g, name):
    t, w4 = z.shape
    w = w4 // 4
    heads = w // HGRN_EXPAND
    rows, c = _tile(t, HGRN_ROWS), HGRN_SUB
    nb, ntb = rows // c, t // rows

    def body(zq_ref, zf_ref, zi_ref, zg_ref, lb_ref, ng_ref, og_ref, o_ref, ck_ref, st_ref, b_scr, k_scr, v_scr):
        tb = pl.program_id(1)

        @pl.when(tb == 0)
        def _():
            st_ref[...] = jnp.zeros_like(st_ref)

        ck_ref[0, 0] = st_ref[...]
        p = _hgrn_prep(zq_ref, zf_ref, zi_ref, lb_ref, b_scr, rows, c)
        k_scr[...] = p["k3"]
        v_scr[...] = p["v3"]
        tpos = lax.broadcasted_iota(jnp.int32, (nb, c, 128), 1)
        od = jnp.zeros((nb, c, 128), jnp.float32)
        for s in range(c):
            bs, ks, vs = b_scr[:, s:s + 1, :], k_scr[:, s:s + 1, :], v_scr[:, s:s + 1, :]
            e = jnp.exp(jnp.minimum(p["b3"] - bs, 0.0))
            a = jnp.sum(jnp.where(tpos >= s, p["q3"] * ks * e, 0.0), axis=2, keepdims=True)
            od = od + a * vs
        st = st_ref[...]
        for i in range(nb):
            o_ref[pl.ds(i * c, c), :] = _dot(p["qt"][i], st, _NT) + od[i]
            st = st * p["dec"][i] + _dot(p["v3"][i], p["kh"][i], _TN)
        st_ref[...] = st
        o = o_ref[...]
        rstd = lax.rsqrt(jnp.mean(o * o, axis=1, keepdims=True) + RMS_EPS)
        zg = zg_ref[...]
        og_ref[...] = (o * rstd * ng_ref[...] * (zg * _sigmoid(zg))).astype(og_ref.dtype)

    def col(off):
        return pl.BlockSpec((rows, 128), lambda h, tb: (tb, h + off))

    out = pl.BlockSpec((rows, 128), lambda h, tb: (tb, h))
    return pl.pallas_call(
        body, name=name, grid=(heads, ntb),
        out_shape=[jax.ShapeDtypeStruct((t, w), jnp.bfloat16), jax.ShapeDtypeStruct((t, w), jnp.float32),
                   jax.ShapeDtypeStruct((heads, ntb, 128, 128), jnp.float32)],
        in_specs=[col(0), col(heads), col(2 * heads), col(3 * heads), pl.BlockSpec((1, 128), lambda h, tb: (0, h)),
                  pl.BlockSpec((1, 128), lambda h, tb: (0, 0))],
        out_specs=[out, out, pl.BlockSpec((1, 1, 128, 128), lambda h, tb: (h, tb, 0, 0))],
        scratch_shapes=[pltpu.VMEM((128, 128), jnp.float32)] + [pltpu.VMEM((nb, c, 128), jnp.float32)] * 3,
        compiler_params=pltpu.CompilerParams(dimension_semantics=("parallel", "arbitrary")),
    )(z, z, z, z, lb, norm_g)


def _hgrn_bwd_call(z, lb, norm_g, o_raw, dog, ck, name):
    t, w4 = z.shape
    w = w4 // 4
    heads = w // HGRN_EXPAND
    rows, c = _tile(t, HGRN_ROWS), HGRN_SUB
    nb, ntb = rows // c, t // rows

    def body(zq_ref, zf_ref, zi_ref, zg_ref, lb_ref, ng_ref, o_ref, dog_ref, ck_ref,
             dzq_ref, dzf_ref, dzi_ref, dzg_ref, dlb_ref, dng_ref,
             dst_ref, s_store, b_scr, k_scr, v_scr, dqt_scr, dkh_scr, dv_scr, ddec_scr):
        tb = pl.program_id(1)

        @pl.when(tb == 0)
        def _():
            dst_ref[...] = jnp.zeros_like(dst_ref)
            dlb_ref[...] = jnp.zeros_like(dlb_ref)
            dng_ref[...] = jnp.zeros_like(dng_ref)

        p = _hgrn_prep(zq_ref, zf_ref, zi_ref, lb_ref, b_scr, rows, c)
        q3, k3, v3, b3 = p["q3"], p["k3"], p["v3"], p["b3"]

        o = o_ref[...]
        rstd = lax.rsqrt(jnp.mean(o * o, axis=1, keepdims=True) + RMS_EPS)
        xh = o * rstd
        zg = zg_ref[...]
        sg = _sigmoid(zg)
        ng = ng_ref[...]
        dogv = dog_ref[...]
        don = dogv * (zg * sg)
        dzg_ref[...] = (dogv * (xh * ng) * (sg * (1.0 + zg * (1.0 - sg)))).astype(dzg_ref.dtype)
        dng_ref[0] += jnp.sum(don * xh, axis=0, keepdims=True)
        dyg = don * ng
        do = rstd * (dyg - xh * jnp.mean(dyg * xh, axis=1, keepdims=True))
        do3 = do.reshape(nb, c, 128)

        st = ck_ref[0, 0]
        for i in range(nb):
            s_store[i] = st
            st = st * p["dec"][i] + _dot(v3[i], p["kh"][i], _TN)

        dst = dst_ref[...]
        for i in reversed(range(nb)):
            sl = pl.ds(i * c, c)
            st_i = s_store[i]
            dqt_scr[sl, :] = _dot(do3[i], st_i, _NN)
            dv_scr[sl, :] = _dot(p["kh"][i], dst, _NT)
            dkh_scr[sl, :] = _dot(v3[i], dst, _NN)
            ddec_scr[i] = jnp.broadcast_to(jnp.sum(st_i * dst, axis=0, keepdims=True), (8, 128))
            dst = dst * p["dec"][i] + _dot(do3[i], p["qt"][i], _TN)
        dst_ref[...] = dst

        k_scr[...] = k3
        v_scr[...] = v3
        tpos = lax.broadcasted_iota(jnp.int32, (nb, c, 128), 1)
        zero = jnp.zeros((nb, c, 128), jnp.float32)
        dqd, dkd, dvd, dbd = zero, zero, zero, zero
        for s in range(c):
            bs, ks, vs = b_scr[:, s:s + 1, :], k_scr[:, s:s + 1, :], v_scr[:, s:s + 1, :]
            e = jnp.where(tpos >= s, jnp.exp(jnp.minimum(b3 - bs, 0.0)), 0.0)
            qe = q3 * e
            a = jnp.sum(qe * ks, axis=2, keepdims=True)
            da = jnp.sum(do3 * vs, axis=2, keepdims=True)
            y = da * (ks * e)
            dqd = dqd + y
            dbd = dbd + y * q3
            dks = jnp.sum(da * qe, axis=1, keepdims=True)
            dvs = jnp.sum(a * do3, axis=1, keepdims=True)
            at_s = tpos == s
            dkd = dkd + jnp.where(at_s, dks, 0.0)
            dvd = dvd + jnp.where(at_s, dvs, 0.0)
            dbd = dbd - jnp.where(at_s, dks * ks, 0.0)

        dqt3 = dqt_scr[...].reshape(nb, c, 128)
        dkh3 = dkh_scr[...].reshape(nb, c, 128)
        ddec = ddec_scr[:, 0:1, :]
        dq_act = dqt3 * p["eb"] + dqd
        dk = dkh3 * p["ek"] + dkd
        khd = p["kh"] * dkh3
        db = p["qt"] * dqt3 - khd + dbd
        dglast = jnp.sum(khd, axis=1, keepdims=True) + ddec * p["dec"]
        db = db + jnp.where(tpos == c - 1, dglast, 0.0)
        dlogf = _group_rcumsum(db.reshape(rows, 128), c, p["pos"])

        zq, sq, sf, f = p["zq"], p["sq"], p["sf"], p["f"]
        lbv = lb_ref[...]
        dzq_ref[...] = (dq_act.reshape(rows, 128) * (sq * (1.0 + zq * (1.0 - sq)))).astype(dzq_ref.dtype)
        df = dlogf / f - dk.reshape(rows, 128)
        dzf_ref[...] = (df * (1.0 - lbv) * (sf * (1.0 - sf))).astype(dzf_ref.dtype)
        dlb_ref[...] += jnp.sum(df * (1.0 - sf), axis=0, keepdims=True)
        dzi_ref[...] = (dv_scr[...] + dvd.reshape(rows, 128)).astype(dzi_ref.dtype)

    last = ntb - 1

    def col(off):
        return pl.BlockSpec((rows, 128), lambda h, tb: (last - tb, h + off))

    out = pl.BlockSpec((rows, 128), lambda h, tb: (last - tb, h))
    lbs = pl.BlockSpec((1, 128), lambda h, tb: (0, h))
    big = pltpu.VMEM((rows, 128), jnp.float32)
    return pl.pallas_call(
        body, name=name, grid=(heads, ntb),
        out_shape=[jax.ShapeDtypeStruct((t, w), jnp.bfloat16)] * 4
        + [jax.ShapeDtypeStruct((1, w), jnp.float32), jax.ShapeDtypeStruct((heads, 1, 128), jnp.float32)],
        in_specs=[col(0), col(heads), col(2 * heads), col(3 * heads), lbs, pl.BlockSpec((1, 128), lambda h, tb: (0, 0)),
                  out, out, pl.BlockSpec((1, 1, 128, 128), lambda h, tb: (h, last - tb, 0, 0))],
        out_specs=[out, out, out, out, lbs, pl.BlockSpec((1, 1, 128), lambda h, tb: (h, 0, 0))],
        scratch_shapes=[pltpu.VMEM((128, 128), jnp.float32), pltpu.VMEM((nb, 128, 128), jnp.float32)]
        + [pltpu.VMEM((nb, c, 128), jnp.float32)] * 3 + [big] * 3 + [pltpu.VMEM((nb, 8, 128), jnp.float32)],
        compiler_params=pltpu.CompilerParams(dimension_semantics=("parallel", "arbitrary")),
    )(z, z, z, z, lb, norm_g, o_raw, dog, ck)


@functools.partial(jax.custom_vjp, nondiff_argnums=(5,))
def _hgrn_block(h, w_in, w_out, norm_g, lb, name):
    return _hgrn_block_fwd(h, w_in, w_out, norm_g, lb, name)[0]


def _hgrn_block_fwd(h, w_in, w_out, norm_g, lb, name):
    hb = h.astype(jnp.bfloat16)
    z = _matmul(hb, w_in, "nn", name + "_in")
    og, o_raw, ck = _hgrn_fwd_call(z, lb.reshape(1, -1), norm_g.reshape(1, -1), name + "_scan")
    return _matmul(og, w_out, "nn", name + "_out"), (hb, z, og, o_raw, ck, w_in, w_out, norm_g, lb)


def _hgrn_block_bwd(name, saved, dy):
    hb, z, og, o_raw, ck, w_in, w_out, norm_g, lb = saved
    dyb = dy.astype(jnp.bfloat16)
    dog = _matmul(dyb, w_out, "nt", name + "_out_dx")
    dw_out = _matmul(og, dyb, "tn", name + "_out_dw", jnp.bfloat16)
    dzq, dzf, dzi, dzg, dlb, dng = _hgrn_bwd_call(z, lb.reshape(1, -1), norm_g.reshape(1, -1), o_raw, dog, ck,
                                                  name + "_scan_bwd")
    dz = jnp.concatenate([dzq, dzf, dzi, dzg], axis=1)
    dh = _matmul(dz, w_in, "nt", name + "_in_dx")
    dw_in = _matmul(hb, dz, "tn", name + "_in_dw", jnp.bfloat16)
    return dh, dw_in, dw_out, jnp.sum(dng, axis=0).reshape(norm_g.shape), dlb.reshape(lb.shape)


_hgrn_block.defvjp(_hgrn_block_fwd, _hgrn_block_bwd)


def _layer_plan(sharded):
    mixers = [
        [("hgrn_w_in", "col"), ("hgrn_w_out", "row")],
        [("swa_w_in", "colT"), ("swa_w_out", "row")],
        [("sc_w_in", "col"), ("sc_conv_w", "exact"), ("sc_w_out", "row")],
        [("fox_w_in", "colT"), ("fox_w_out", "row")],
    ]
    depth = sharded["ffn_w_up"].shape[0]
    plan = []
    for i in range(depth):
        m, j = i % len(mixers), i // len(mixers)
        entries = [(n, j, kind) for n, kind in mixers[m]]
        entries += [("ffn_w_up", i, "col"), ("ffn_conv_w", i, "exact"), ("ffn_w_down", i, "row")]
        plan.append(entries)
    return plan


def _local_loss(sharded, repl, x, positions, loss_target):
    x = x[0]
    loss_target = loss_target[0]
    d = x.shape[-1]
    plan = _layer_plan(sharded)
    lb_table = jnp.cumsum(jax.nn.softmax(repl["hgrn_lb_param"], axis=0), axis=0)
    for i, entries in enumerate(plan):
        shards = tuple(sharded[n][j] for n, j, _ in entries)
        kinds = tuple(kind for _, _, kind in entries)
        full = _make_layer_gather([s.shape for s in shards], kinds, "layer%d" % i)(shards)
        wts = {n: f for (n, _, _), f in zip(entries, full)}
        m, j = i % 4, i // 4
        tag = "l%d" % i
        hn = _norm(x, repl["mix_pre_g"][i], tag + "_mix_pre")
        if m == 0:
            y = _hgrn_block(hn, wts["hgrn_w_in"], wts["hgrn_w_out"], repl["hgrn_norm_g"][j], lb_table[i], tag + "_hgrn")
        elif m == 1:
            y = _swa_block(hn, positions, wts["swa_w_in"], wts["swa_w_out"], repl["swa_sinks"][j], tag + "_swa")
        elif m == 2:
            y = _sconv_block(hn, wts["sc_w_in"], wts["sc_conv_w"], wts["sc_w_out"], tag + "_sc")
        else:
            w_in = wts["fox_w_in"]
            heads = d // FOX_HEAD_DIM
            w_qkvg = jnp.concatenate([w_in[:3 * d], w_in[3 * d + heads:]], axis=0)
            w_f = jnp.pad(w_in[3 * d:3 * d + heads], ((0, 128 - heads), (0, 0)))
            y = _fox_block(hn, w_qkvg, w_f, repl["fox_b_f"][j], wts["fox_w_out"], tag + "_fox")
        x = _norm_add(y, repl["mix_post_g"][i], x, tag + "_mix_post")
        hn = _norm(x, repl["ffn_pre_g"][i], tag + "_ffn_pre")
        y = _ffn_block(hn, wts["ffn_w_up"], wts["ffn_conv_w"], repl["ffn_conv_b"][i], wts["ffn_w_down"], tag + "_ffn")
        x = _norm_add(y, repl["ffn_post_g"][i], x, tag + "_ffn_post")
    err = jnp.square(x - loss_target)
    return 0.5 * jnp.sum(jnp.mean(err, axis=-1))


SHARDED = ["hgrn_w_in", "hgrn_w_out", "swa_w_in", "swa_w_out", "sc_w_in", "sc_conv_w", "sc_w_out",
           "fox_w_in", "fox_w_out", "ffn_w_up", "ffn_conv_w", "ffn_w_down"]
REPLICATED = ["mix_pre_g", "mix_post_g", "ffn_pre_g", "ffn_post_g", "hgrn_norm_g", "hgrn_lb_param",
              "swa_sinks", "fox_b_f", "ffn_conv_b"]
WEIGHTS = ["mix_pre_g", "mix_post_g", "ffn_pre_g", "ffn_post_g", "hgrn_w_in", "hgrn_w_out", "hgrn_norm_g",
           "hgrn_lb_param", "swa_w_in", "swa_w_out", "swa_sinks", "sc_w_in", "sc_conv_w", "sc_w_out",
           "fox_w_in", "fox_b_f", "fox_w_out", "ffn_w_up", "ffn_conv_w", "ffn_conv_b", "ffn_w_down"]


def _sum_replicated(loss, grads):
    parts = [loss.reshape(1)] + [grads[n].reshape(-1) for n in REPLICATED]
    flat = jnp.concatenate(parts)
    n = flat.shape[0]
    cols = 1024
    rows = -(-n // cols)
    rows += (-rows) % 8
    flat = jnp.pad(flat, (0, rows * cols - n)).reshape(rows, cols)
    total = _sum_blocks(_all_gather(flat, "small_gather"), "small_sum").reshape(-1)
    out, off = {}, 1
    for name in REPLICATED:
        size = grads[name].size
        out[name] = total[off:off + size].reshape(grads[name].shape)
        off += size
    return total[0], out


def kernel(x, positions, mix_pre_g, mix_post_g, ffn_pre_g, ffn_post_g, hgrn_w_in, hgrn_w_out, hgrn_norm_g, hgrn_lb_param, swa_w_in, swa_w_out, swa_sinks, sc_w_in, sc_conv_w, sc_w_out, fox_w_in, fox_b_f, fox_w_out, ffn_w_up, ffn_conv_w, ffn_conv_b, ffn_w_down, loss_target, m_mix_pre_g, m_mix_post_g, m_ffn_pre_g, m_ffn_post_g, m_hgrn_w_in, m_hgrn_w_out, m_hgrn_norm_g, m_hgrn_lb_param, m_swa_w_in, m_swa_w_out, m_swa_sinks, m_sc_w_in, m_sc_conv_w, m_sc_w_out, m_fox_w_in, m_fox_b_f, m_fox_w_out, m_ffn_w_up, m_ffn_conv_w, m_ffn_conv_b, m_ffn_w_down, v_mix_pre_g, v_mix_post_g, v_ffn_pre_g, v_ffn_post_g, v_hgrn_w_in, v_hgrn_w_out, v_hgrn_norm_g, v_hgrn_lb_param, v_swa_w_in, v_swa_w_out, v_swa_sinks, v_sc_w_in, v_sc_conv_w, v_sc_w_out, v_fox_w_in, v_fox_b_f, v_fox_w_out, v_ffn_w_up, v_ffn_conv_w, v_ffn_conv_b, v_ffn_w_down):
    given = dict(locals())
    sharded = {n: given[n] for n in SHARDED}
    repl = {n: given[n] for n in REPLICATED}
    loss, (g_sharded, g_repl, grad_x) = jax.value_and_grad(_local_loss, argnums=(0, 1, 2))(
        sharded, repl, x, positions, loss_target)
    loss, g_repl = _sum_replicated(loss, g_repl)
    grads = {**g_sharded, **g_repl}
    delta, new_m, new_v = {}, {}, {}
    for n in WEIGHTS:
        delta[n], new_m[n], new_v[n] = _adamw(given[n], grads[n], given["m_" + n], given["v_" + n], "adamw_" + n)
    return (loss, grad_x, *[grads[n] for n in WEIGHTS], *[delta[n] for n in WEIGHTS],
            *[new_m[n] for n in WEIGHTS], *[new_v[n] for n in WEIGHTS])
```

```python
import functools
import math

import jax
import jax.numpy as jnp
from jax import lax
from jax.experimental import pallas as pl
from jax.experimental.pallas import tpu as pltpu

N_DEV = 8
MESH = pl.DeviceIdType.MESH
HBM_SPEC = pl.BlockSpec(memory_space=pltpu.HBM)
PACK_COLS = 512
PACK_ROW_BLOCK = 512
PACK_TILE_ROWS = 16

MM_TILE = 1024
MM_K_SINGLE = 2048
MM_K_TILE = 1536
MM_VMEM_BYTES = 48 * 1024 * 1024
ROW_BLOCK = 256
CONV_ROW_BLOCK = 512
COL_BLOCK = 512
HALO = 8
ATT_BLOCK = 512
FOX_HEADS_PER_STEP = 2
HGRN_SUB = 16
HGRN_ROWS = 256
NEG = -1e30

RMS_EPS = 1e-6
HGRN_EXPAND = 128
SWA_HEAD_DIM = 64
SWA_GROUP = 8
SWA_WINDOW = 128
FOX_HEAD_DIM = 64
ROPE_THETA = 500000.0
ROT_DIM = SWA_HEAD_DIM // 4

ADAM_LR = 0.001
ADAM_B1 = 0.9
ADAM_B2 = 0.999
ADAM_EPS = 1e-08
ADAM_WD = 0.01
ADAM_STEP = 10

_NT = (((1,), (1,)), ((), ()))
_NN = (((1,), (0,)), ((), ()))
_TN = (((0,), (0,)), ((), ()))


def _tile(dim, pref):
    if dim <= pref:
        return dim
    t = pref - pref % 128
    while t >= 128:
        if dim % t == 0:
            return t
        t -= 128
    return dim


def _dot(a, b, dims):
    return lax.dot_general(a.astype(jnp.bfloat16), b.astype(jnp.bfloat16), dims, preferred_element_type=jnp.float32)


def _sigmoid(x):
    return 1.0 / (1.0 + jnp.exp(-x))


def _col_to_row(c):
    n = c.shape[0]
    eye = lax.broadcasted_iota(jnp.int32, (n, n), 0) == lax.broadcasted_iota(jnp.int32, (n, n), 1)
    return jnp.sum(jnp.where(eye, c, 0.0), axis=0, keepdims=True)


def _row_to_col(r):
    n = r.shape[1]
    eye = lax.broadcasted_iota(jnp.int32, (n, n), 0) == lax.broadcasted_iota(jnp.int32, (n, n), 1)
    return jnp.sum(jnp.where(eye, r, 0.0), axis=1, keepdims=True)


def _all_gather(x, name):
    r, c = x.shape

    def body(x_ref, out_ref, send_sems, recv_sems, local_sem):
        mx, my, mc = lax.axis_index("x"), lax.axis_index("y"), lax.axis_index("c")
        me, sibling = (mx, my, mc), (mx, my, 1 - mc)
        chips = [(1 - mx, my), (mx, 1 - my), (1 - mx, 1 - my)]

        def rows(px, py, pc):
            return out_ref.at[4 * px + 2 * py + pc]

        def copy(k, block, to, src=None):
            return pltpu.make_async_remote_copy(
                src_ref=rows(*block) if src is None else src, dst_ref=rows(*block),
                send_sem=send_sems.at[k], recv_sem=recv_sems.at[k], device_id=to, device_id_type=MESH)

        mine = pltpu.make_async_copy(x_ref, rows(*me), local_sem)
        mine.start()
        first = [copy(0, me, sibling, src=x_ref)]
        first += [copy(1 + j, me, (*chip, mc), src=x_ref) for j, chip in enumerate(chips)]
        for cp in first:
            cp.start()
        passed = [copy(4 + j, (*chip, mc), sibling) for j, chip in enumerate(chips)]
        for j, chip in enumerate(chips):
            copy(1 + j, (*chip, mc), me).wait_recv()
            passed[j].start()
        copy(0, sibling, me).wait_recv()
        for j, chip in enumerate(chips):
            copy(4 + j, (*chip, 1 - mc), me).wait_recv()
        for cp in first + passed:
            cp.wait_send()
        mine.wait()

    return pl.pallas_call(
        body, name=name, out_shape=jax.ShapeDtypeStruct((N_DEV, r, c), x.dtype),
        in_specs=[HBM_SPEC], out_specs=HBM_SPEC,
        scratch_shapes=[pltpu.SemaphoreType.DMA((7,)), pltpu.SemaphoreType.DMA((7,)), pltpu.SemaphoreType.DMA],
    )(x)


def _pair_exchange(x, name):
    n = x.shape[0] // 2

    def body(x_ref, out_ref, send_sems, recv_sems):
        mc = lax.axis_index("c")
        sibling = (lax.axis_index("x"), lax.axis_index("y"), 1 - mc)
        copies = [pltpu.make_async_remote_copy(
            src_ref=x_ref.at[2 * j + 1 - mc], dst_ref=out_ref.at[j], send_sem=send_sems.at[j],
            recv_sem=recv_sems.at[j], device_id=sibling, device_id_type=MESH) for j in range(n)]
        for cp in copies:
            cp.start()
        for cp in copies:
            cp.wait()

    return pl.pallas_call(
        body, name=name, out_shape=jax.ShapeDtypeStruct((n,) + x.shape[1:], x.dtype),
        in_specs=[HBM_SPEC], out_specs=HBM_SPEC,
        scratch_shapes=[pltpu.SemaphoreType.DMA((n,)), pltpu.SemaphoreType.DMA((n,))],
    )(x)


def _chip_exchange(x, name):
    flips = [(1, 0), (0, 1), (1, 1)]

    def body(x_ref, out_ref, send_sems, recv_sems, local_sem):
        mx, my, mc = lax.axis_index("x"), lax.axis_index("y"), lax.axis_index("c")
        me = 2 * mx + my

        def chip(flip):
            return ((1 - mx) if flip[0] else mx, (1 - my) if flip[1] else my)

        def copy(k, dst_block):
            px, py = chip(flips[k])
            return pltpu.make_async_remote_copy(
                src_ref=x_ref.at[2 * px + py], dst_ref=out_ref.at[dst_block],
                send_sem=send_sems.at[k], recv_sem=recv_sems.at[k], device_id=(px, py, mc), device_id_type=MESH)

        mine = pltpu.make_async_copy(x_ref.at[me], out_ref.at[me], local_sem)
        mine.start()
        sends = [copy(k, me) for k in range(len(flips))]
        for cp in sends:
            cp.start()
        for k in range(len(flips)):
            px, py = chip(flips[k])
            copy(k, 2 * px + py).wait_recv()
        for cp in sends:
            cp.wait_send()
        mine.wait()

    return pl.pallas_call(
        body, name=name, out_shape=jax.ShapeDtypeStruct(x.shape, x.dtype),
        in_specs=[HBM_SPEC], out_specs=HBM_SPEC,
        scratch_shapes=[pltpu.SemaphoreType.DMA((3,)), pltpu.SemaphoreType.DMA((3,)), pltpu.SemaphoreType.DMA],
    )(x)


def _sum_blocks(x, name, out_dtype=jnp.float32):
    n, r, c = x.shape
    tr = PACK_ROW_BLOCK if r % PACK_ROW_BLOCK == 0 else r

    def body(x_ref, o_ref):
        acc = x_ref[0].astype(jnp.float32)
        for j in range(1, n):
            acc = acc + x_ref[j].astype(jnp.float32)
        o_ref[...] = acc.astype(o_ref.dtype)

    return pl.pallas_call(
        body, name=name, grid=(r // tr,), out_shape=jax.ShapeDtypeStruct((r, c), out_dtype),
        in_specs=[pl.BlockSpec((n, tr, c), lambda i: (0, i, 0))],
        out_specs=pl.BlockSpec((tr, c), lambda i: (i, 0)),
    )(x)


def _pair_sum(x, got, name):
    n, _, r, c = x.shape
    tr = PACK_ROW_BLOCK if r % PACK_ROW_BLOCK == 0 else r

    def body(x_ref, g_ref, o_ref):
        mine = jnp.where(lax.axis_index("c") == 0, x_ref[0, 0], x_ref[0, 1])
        o_ref[0] = (mine.astype(jnp.float32) + g_ref[0].astype(jnp.float32)).astype(o_ref.dtype)

    spec = pl.BlockSpec((1, tr, c), lambda j, i: (j, i, 0))
    return pl.pallas_call(
        body, name=name, grid=(n, r // tr), out_shape=jax.ShapeDtypeStruct(got.shape, got.dtype),
        in_specs=[pl.BlockSpec((1, 2, tr, c), lambda j, i: (j, 0, i, 0)), spec], out_specs=spec,
    )(x, got)


def _reduce_scatter(x, name):
    _, r, c = x.shape
    paired = _pair_sum(x.reshape(4, 2, r, c), _pair_exchange(x, name + "_pair"), name + "_pair_sum")
    return _sum_blocks(_chip_exchange(paired, name + "_chips"), name + "_sum")


def _matmul(a, b, mode, name, out_dtype=jnp.float32):
    if mode == "nn":
        (m, k), (_, n) = a.shape, b.shape
    elif mode == "nt":
        (m, k), (n, _) = a.shape, b.shape
    else:
        (k, m), (_, n) = a.shape, b.shape
    tm, tn = _tile(m, MM_TILE), _tile(n, MM_TILE)
    tk = k if k <= MM_K_SINGLE else _tile(k, MM_K_TILE)
    nk = k // tk
    dims = {"nn": _NN, "nt": _NT, "tn": _TN}[mode]
    rows_inner = nk == 1 and (n // tn) * a.size < (m // tm) * b.size

    def body(a_ref, b_ref, o_ref, *scratch):
        part = lax.dot_general(a_ref[...], b_ref[...], dims, preferred_element_type=jnp.float32)
        if nk == 1:
            o_ref[...] = part.astype(o_ref.dtype)
            return
        acc_ref, = scratch
        kk = pl.program_id(2)

        @pl.when(kk == 0)
        def _():
            acc_ref[...] = part

        @pl.when(kk > 0)
        def _():
            acc_ref[...] += part

        @pl.when(kk == nk - 1)
        def _():
            o_ref[...] = acc_ref[...].astype(o_ref.dtype)

    def ij(g0, g1):
        return (g1, g0) if rows_inner else (g0, g1)

    def a_map(g0, g1, kk):
        i, _ = ij(g0, g1)
        return (kk, i) if mode == "tn" else (i, kk)

    def b_map(g0, g1, kk):
        _, j = ij(g0, g1)
        return (j, kk) if mode == "nt" else (kk, j)

    def o_map(g0, g1, kk):
        return ij(g0, g1)

    grid = (n // tn, m // tm, nk) if rows_inner else (m // tm, n // tn, nk)
    return pl.pallas_call(
        body, name=name, grid=grid, out_shape=jax.ShapeDtypeStruct((m, n), out_dtype),
        in_specs=[pl.BlockSpec((tk, tm) if mode == "tn" else (tm, tk), a_map),
                  pl.BlockSpec((tn, tk) if mode == "nt" else (tk, tn), b_map)],
        out_specs=pl.BlockSpec((tm, tn), o_map),
        scratch_shapes=[pltpu.VMEM((tm, tn), jnp.float32)] if nk > 1 else [],
        compiler_params=pltpu.CompilerParams(dimension_semantics=("parallel", "parallel", "arbitrary"),
                                             vmem_limit_bytes=MM_VMEM_BYTES),
    )(a, b)


def _piece_rows(shape, kind, gathering):
    size = math.prod(shape) * (2 if kind == "exact" and gathering else 1)
    rows = -(-size // PACK_COLS)
    return rows + (-rows) % PACK_TILE_ROWS


def _as_rows(x, rows, lead):
    n = x.shape[-1]
    if n != rows * PACK_COLS:
        x = jnp.pad(x, [(0, 0)] * len(lead) + [(0, rows * PACK_COLS - n)])
    return x.reshape(tuple(lead) + (rows, PACK_COLS))


def _stack_rows(parts, axis):
    total = sum(p.shape[axis] for p in parts)
    fill = (-total) % PACK_ROW_BLOCK
    if fill:
        shape = list(parts[0].shape)
        shape[axis] = fill
        parts = parts + [jnp.zeros(shape, parts[0].dtype)]
    return jnp.concatenate(parts, axis=axis)


def _gather_pack(shards, kinds):
    parts = []
    for s, kind in zip(shards, kinds):
        if kind == "exact":
            flat = lax.bitcast_convert_type(s, jnp.bfloat16).reshape(-1)
        elif kind == "colT":
            flat = s.T.astype(jnp.bfloat16).reshape(-1)
        else:
            flat = s.astype(jnp.bfloat16).reshape(-1)
        parts.append(_as_rows(flat, _piece_rows(s.shape, kind, True), ()))
    return _stack_rows(parts, 0)


def _gather_unpack(gathered, shapes, kinds):
    out, off = [], 0
    for shape, kind in zip(shapes, kinds):
        size = math.prod(shape) * (2 if kind == "exact" else 1)
        rows = _piece_rows(shape, kind, True)
        piece = gathered[:, off:off + rows]
        off += rows
        if size != rows * PACK_COLS:
            piece = piece.reshape(N_DEV, rows * PACK_COLS)[:, :size]
        if kind == "colT":
            out.append(piece.reshape(N_DEV * shape[1], shape[0]))
            continue
        if kind == "exact":
            piece = lax.bitcast_convert_type(piece.reshape((N_DEV,) + tuple(shape) + (2,)), jnp.float32)
            kind = "col"
        else:
            piece = piece.reshape((N_DEV,) + tuple(shape))
        if kind == "col":
            piece = jnp.moveaxis(piece, 0, -2)
            piece = piece.reshape(piece.shape[:-2] + (N_DEV * shape[-1],))
        else:
            piece = piece.reshape((N_DEV * shape[0],) + tuple(shape[1:]))
        out.append(piece)
    return tuple(out)


def _scatter_pack(cts, shapes, kinds):
    parts = []
    for ct, shape, kind in zip(cts, shapes, kinds):
        ct = ct.astype(jnp.bfloat16)
        if kind not in ("row", "colT"):
            ct = jnp.moveaxis(ct.reshape(tuple(shape[:-1]) + (N_DEV, shape[-1])), -2, 0)
        rows = _piece_rows(shape, kind, False)
        if math.prod(shape) == rows * PACK_COLS:
            parts.append(ct.reshape(N_DEV, rows, PACK_COLS))
        else:
            parts.append(_as_rows(ct.reshape(N_DEV, -1), rows, (N_DEV,)))
    return _stack_rows(parts, 1)


def _scatter_unpack(summed, shapes, kinds):
    out, off = [], 0
    for shape, kind in zip(shapes, kinds):
        size = math.prod(shape)
        rows = _piece_rows(shape, kind, False)
        piece = summed[off:off + rows]
        off += rows
        if size != rows * PACK_COLS:
            piece = piece.reshape(-1)[:size]
        out.append(piece.reshape(shape[1], shape[0]).T if kind == "colT" else piece.reshape(shape))
    return tuple(out)


def _make_layer_gather(shapes, kinds, tag):
    shapes = tuple(tuple(s) for s in shapes)

    def impl(shards):
        return _gather_unpack(_all_gather(_gather_pack(shards, kinds), tag + "_gather"), shapes, kinds)

    gather = jax.custom_vjp(impl)

    def fwd(shards):
        return impl(shards), None

    def bwd(_, cts):
        packed = _scatter_pack(cts, shapes, kinds)
        return (_scatter_unpack(_reduce_scatter(packed, tag + "_scatter"), shapes, kinds),)

    gather.defvjp(fwd, bwd)
    return gather


def _adamw(w, g, m, v, name):
    shape = w.shape
    cols = shape[-1]
    rows = math.prod(shape[:-1])
    tr = rows
    if rows > 512:
        for cand in (256, 128, 64, 32, 16, 8):
            if rows % cand == 0:
                tr = cand
                break
    c1 = 1.0 - ADAM_B1 ** ADAM_STEP
    c2 = 1.0 - ADAM_B2 ** ADAM_STEP

    def body(w_ref, g_ref, m_ref, v_ref, d_ref, nm_ref, nv_ref):
        gg = g_ref[...]
        nm = ADAM_B1 * m_ref[...] + (1.0 - ADAM_B1) * gg
        nv = ADAM_B2 * v_ref[...] + (1.0 - ADAM_B2) * (gg * gg)
        m_hat = nm / c1
        v_hat = nv / c2
        d_ref[...] = -ADAM_LR * (m_hat / (jnp.sqrt(v_hat) + ADAM_EPS) + ADAM_WD * w_ref[...])
        nm_ref[...] = nm
        nv_ref[...] = nv

    spec = pl.BlockSpec((tr, cols), lambda i: (i, 0))
    outs = pl.pallas_call(
        body, name=name, grid=(rows // tr,), out_shape=[jax.ShapeDtypeStruct((rows, cols), jnp.float32)] * 3,
        in_specs=[spec] * 4, out_specs=[spec] * 3,
    )(*[t.reshape(rows, cols) for t in (w, g, m, v)])
    return tuple(o.reshape(shape) for o in outs)


def _rms_fwd_call(x, g, res, name):
    t, d = x.shape
    tr = _tile(t, ROW_BLOCK)
    with_res = res is not None

    def body(*refs):
        if with_res:
            x_ref, g_ref, r_ref, y_ref = refs
        else:
            x_ref, g_ref, y_ref = refs
        xv = x_ref[...]
        rstd = lax.rsqrt(jnp.mean(xv * xv, axis=-1, keepdims=True) + RMS_EPS)
        y = xv * rstd * g_ref[...]
        if with_res:
            y = y + r_ref[...]
        y_ref[...] = y

    row = pl.BlockSpec((tr, d), lambda i: (i, 0))
    gspec = pl.BlockSpec((1, d), lambda i: (0, 0))
    ins = [x, g] + ([res] if with_res else [])
    return pl.pallas_call(
        body, name=name, grid=(t // tr,), out_shape=jax.ShapeDtypeStruct((t, d), jnp.float32),
        in_specs=[row, gspec] + ([row] if with_res else []), out_specs=row,
    )(*ins)


def _rms_bwd_call(x, g, dy, name):
    t, d = x.shape
    tr = _tile(t, ROW_BLOCK)

    def body(x_ref, g_ref, dy_ref, dx_ref, dg_ref):
        i = pl.program_id(0)
        xv = x_ref[...]
        rstd = lax.rsqrt(jnp.mean(xv * xv, axis=-1, keepdims=True) + RMS_EPS)
        xh = xv * rstd
        dyv = dy_ref[...]
        dyg = dyv * g_ref[...]
        dx_ref[...] = rstd * (dyg - xh * jnp.mean(dyg * xh, axis=-1, keepdims=True))

        @pl.when(i == 0)
        def _():
            dg_ref[...] = jnp.zeros_like(dg_ref)

        dg_ref[...] += jnp.sum(dyv * xh, axis=0, keepdims=True)

    row = pl.BlockSpec((tr, d), lambda i: (i, 0))
    gspec = pl.BlockSpec((1, d), lambda i: (0, 0))
    return pl.pallas_call(
        body, name=name, grid=(t // tr,),
        out_shape=[jax.ShapeDtypeStruct((t, d), jnp.float32), jax.ShapeDtypeStruct((1, d), jnp.float32)],
        in_specs=[row, gspec, row], out_specs=[row, gspec],
        compiler_params=pltpu.CompilerParams(dimension_semantics=("arbitrary",)),
    )(x, g, dy)


@functools.partial(jax.custom_vjp, nondiff_argnums=(3,))
def _norm_add(y, g, res, name):
    return _rms_fwd_call(y, g.reshape(1, -1), res, name + "_fwd")


def _norm_add_fwd(y, g, res, name):
    return _rms_fwd_call(y, g.reshape(1, -1), res, name + "_fwd"), (y, g)


def _norm_add_bwd(name, saved, dout):
    y, g = saved
    dy, dg = _rms_bwd_call(y, g.reshape(1, -1), dout, name + "_bwd")
    return dy, dg.reshape(g.shape), dout


_norm_add.defvjp(_norm_add_fwd, _norm_add_bwd)


@functools.partial(jax.custom_vjp, nondiff_argnums=(2,))
def _norm(x, g, name):
    return _rms_fwd_call(x, g.reshape(1, -1), None, name + "_fwd")


def _norm_fwd(x, g, name):
    return _rms_fwd_call(x, g.reshape(1, -1), None, name + "_fwd"), (x, g)


def _norm_bwd(name, saved, dout):
    x, g = saved
    dx, dg = _rms_bwd_call(x, g.reshape(1, -1), dout, name + "_bwd")
    return dx, dg.reshape(g.shape)


_norm.defvjp(_norm_fwd, _norm_bwd)


def _shift_down(x, before1, before2, row):
    xm1 = jnp.where(row == 0, before1, pltpu.roll(x, 1, 0))
    xm2 = jnp.where(row == 0, before2, jnp.where(row == 1, before1, pltpu.roll(x, 2, 0)))
    return xm1, xm2


def _shift_up(x, after1, after2, row):
    n = x.shape[0]
    xp1 = jnp.where(row == n - 1, after1, pltpu.roll(x, n - 1, 0))
    xp2 = jnp.where(row == n - 1, after2, jnp.where(row == n - 2, after1, pltpu.roll(x, n - 2, 0)))
    return xp1, xp2


def _ffn_act_fwd_call(u, conv_w, conv_b, name):
    t, f2 = u.shape
    f = f2 // 2
    tr, tc = _tile(t, CONV_ROW_BLOCK), _tile(f, COL_BLOCK)
    nj, hb = f // tc, tr // HALO

    def body(ug_ref, uu_ref, hg_ref, hu_ref, wg_ref, wu_ref, bg_ref, bu_ref, cg_ref, cu_ref, a_ref):
        i = pl.program_id(0)
        row = lax.broadcasted_iota(jnp.int32, (tr, tc), 0)

        def conv(x_ref, h_ref, w_ref, b_ref):
            xv = x_ref[...]
            h1 = jnp.where(i > 0, h_ref[7:8, :], 0.0)
            h2 = jnp.where(i > 0, h_ref[6:7, :], 0.0)
            xm1, xm2 = _shift_down(xv, h1, h2, row)
            return w_ref[0:1, :] * xm2 + w_ref[1:2, :] * xm1 + w_ref[2:3, :] * xv + b_ref[...]

        cg = conv(ug_ref, hg_ref, wg_ref, bg_ref)
        cu = conv(uu_ref, hu_ref, wu_ref, bu_ref)
        cg_ref[...] = cg
        cu_ref[...] = cu
        a_ref[...] = (cg * _sigmoid(cg) * cu).astype(jnp.bfloat16)

    def blk(off):
        return pl.BlockSpec((tr, tc), lambda i, j: (i, j + off))

    def halo(off):
        return pl.BlockSpec((HALO, tc), lambda i, j: (jnp.maximum(i * hb - 1, 0), j + off))

    def wspec(rows, off):
        return pl.BlockSpec((rows, tc), lambda i, j: (0, j + off))

    out_blk = pl.BlockSpec((tr, tc), lambda i, j: (i, j))
    return pl.pallas_call(
        body, name=name, grid=(t // tr, nj),
        out_shape=[jax.ShapeDtypeStruct((t, f), jnp.float32), jax.ShapeDtypeStruct((t, f), jnp.float32),
                   jax.ShapeDtypeStruct((t, f), jnp.bfloat16)],
        in_specs=[blk(0), blk(nj), halo(0), halo(nj), wspec(3, 0), wspec(3, nj), wspec(1, 0), wspec(1, nj)],
        out_specs=[out_blk, out_blk, out_blk],
    )(u, u, u, u, conv_w, conv_w, conv_b, conv_b)


def _ffn_act_bwd_call(da, cg, cu, u, conv_w, name):
    t, f = da.shape
    tr, tc = _tile(t, CONV_ROW_BLOCK), _tile(f, COL_BLOCK)
    nj, hb, ni = f // tc, tr // HALO, t // tr

    def body(da_ref, cg_ref, cu_ref, nda_ref, ncg_ref, ncu_ref, u_ref, hu_ref, w_ref, du_ref, dw_ref, db_ref):
        j, i = pl.program_id(0), pl.program_id(1)
        is_gate = j < nj
        row = lax.broadcasted_iota(jnp.int32, (tr, tc), 0)

        def duc_of(dav, cgv, cuv):
            sg = _sigmoid(cgv)
            d_gate = dav * cuv * (sg * (1.0 + cgv * (1.0 - sg)))
            d_up = dav * (cgv * sg)
            return jnp.where(is_gate, d_gate, d_up)

        duc = duc_of(da_ref[...], cg_ref[...], cu_ref[...])
        n1 = jnp.where(i < ni - 1, duc_of(nda_ref[0:1, :], ncg_ref[0:1, :], ncu_ref[0:1, :]), 0.0)
        n2 = jnp.where(i < ni - 1, duc_of(nda_ref[1:2, :], ncg_ref[1:2, :], ncu_ref[1:2, :]), 0.0)
        dp1, dp2 = _shift_up(duc, n1, n2, row)
        du_ref[...] = (w_ref[2:3, :] * duc + w_ref[1:2, :] * dp1 + w_ref[0:1, :] * dp2).astype(jnp.bfloat16)

        uv = u_ref[...]
        h1 = jnp.where(i > 0, hu_ref[7:8, :], 0.0)
        h2 = jnp.where(i > 0, hu_ref[6:7, :], 0.0)
        um1, um2 = _shift_down(uv, h1, h2, row)

        @pl.when(i == 0)
        def _():
            dw_ref[...] = jnp.zeros_like(dw_ref)
            db_ref[...] = jnp.zeros_like(db_ref)

        rows8 = lax.broadcasted_iota(jnp.int32, (HALO, tc), 0)
        s0 = jnp.sum(duc * um2, axis=0, keepdims=True)
        s1 = jnp.sum(duc * um1, axis=0, keepdims=True)
        s2 = jnp.sum(duc * uv, axis=0, keepdims=True)
        dw_ref[...] += jnp.where(rows8 == 0, s0, jnp.where(rows8 == 1, s1, jnp.where(rows8 == 2, s2, 0.0)))
        db_ref[...] += jnp.sum(duc, axis=0, keepdims=True)

    half = pl.BlockSpec((tr, tc), lambda j, i: (i, j % nj))
    nhalf = pl.BlockSpec((HALO, tc), lambda j, i: (jnp.minimum((i + 1) * hb, t // HALO - 1), j % nj))
    full = pl.BlockSpec((tr, tc), lambda j, i: (i, j))
    hfull = pl.BlockSpec((HALO, tc), lambda j, i: (jnp.maximum(i * hb - 1, 0), j))
    return pl.pallas_call(
        body, name=name, grid=(2 * nj, ni),
        out_shape=[jax.ShapeDtypeStruct((t, 2 * f), jnp.bfloat16), jax.ShapeDtypeStruct((HALO, 2 * f), jnp.float32),
                   jax.ShapeDtypeStruct((1, 2 * f), jnp.float32)],
        in_specs=[half, half, half, nhalf, nhalf, nhalf, full, hfull, pl.BlockSpec((3, tc), lambda j, i: (0, j))],
        out_specs=[full, pl.BlockSpec((HALO, tc), lambda j, i: (0, j)), pl.BlockSpec((1, tc), lambda j, i: (0, j))],
        compiler_params=pltpu.CompilerParams(dimension_semantics=("parallel", "arbitrary")),
    )(da, cg, cu, da, cg, cu, u, u, conv_w)


@functools.partial(jax.custom_vjp, nondiff_argnums=(5,))
def _ffn_block(h, w_up, conv_w, conv_b, w_down, name):
    return _ffn_block_fwd(h, w_up, conv_w, conv_b, w_down, name)[0]


def _ffn_block_fwd(h, w_up, conv_w, conv_b, w_down, name):
    hb = h.astype(jnp.bfloat16)
    u = _matmul(hb, w_up, "nn", name + "_up")
    cg, cu, a = _ffn_act_fwd_call(u, conv_w, conv_b.reshape(1, -1), name + "_act")
    return _matmul(a, w_down, "nn", name + "_down"), (hb, u, cg, cu, a, w_up, conv_w, conv_b, w_down)


def _ffn_block_bwd(name, saved, dy):
    hb, u, cg, cu, a, w_up, conv_w, conv_b, w_down = saved
    dyb = dy.astype(jnp.bfloat16)
    da = _matmul(dyb, w_down, "nt", name + "_down_dx")
    dw_down = _matmul(a, dyb, "tn", name + "_down_dw", jnp.bfloat16)
    du, dcw, dcb = _ffn_act_bwd_call(da, cg, cu, u, conv_w, name + "_act_bwd")
    dh = _matmul(du, w_up, "nt", name + "_up_dx")
    dw_up = _matmul(hb, du, "tn", name + "_up_dw", jnp.bfloat16)
    return dh, dw_up, dcw[:3], dcb.reshape(conv_b.shape), dw_down


_ffn_block.defvjp(_ffn_block_fwd, _ffn_block_bwd)


def _sconv_fwd_call(z, conv_w, name):
    t, d3 = z.shape
    d = d3 // 3
    tr, tc = _tile(t, CONV_ROW_BLOCK), _tile(d, COL_BLOCK)
    nj, hb = d // tc, tr // HALO

    def body(b_ref, c_ref, x_ref, hc_ref, hx_ref, w_ref, a_ref):
        i = pl.program_id(0)
        row = lax.broadcasted_iota(jnp.int32, (tr, tc), 0)
        p = c_ref[...] * x_ref[...]
        h1 = jnp.where(i > 0, hc_ref[7:8, :] * hx_ref[7:8, :], 0.0)
        h2 = jnp.where(i > 0, hc_ref[6:7, :] * hx_ref[6:7, :], 0.0)
        pm1, pm2 = _shift_down(p, h1, h2, row)
        pc = w_ref[0:1, :] * pm2 + w_ref[1:2, :] * pm1 + w_ref[2:3, :] * p
        a_ref[...] = (b_ref[...] * pc).astype(jnp.bfloat16)

    def blk(off):
        return pl.BlockSpec((tr, tc), lambda i, j: (i, j + off))

    def halo(off):
        return pl.BlockSpec((HALO, tc), lambda i, j: (jnp.maximum(i * hb - 1, 0), j + off))

    return pl.pallas_call(
        body, name=name, grid=(t // tr, nj), out_shape=jax.ShapeDtypeStruct((t, d), jnp.bfloat16),
        in_specs=[blk(0), blk(nj), blk(2 * nj), halo(nj), halo(2 * nj), pl.BlockSpec((3, tc), lambda i, j: (0, j))],
        out_specs=pl.BlockSpec((tr, tc), lambda i, j: (i, j)),
    )(z, z, z, z, z, conv_w)


def _sconv_bwd_call(da, z, conv_w, name):
    t, d = da.shape
    tr, tc = _tile(t, CONV_ROW_BLOCK), _tile(d, COL_BLOCK)
    nj, hb, ni = d // tc, tr // HALO, t // tr

    def body(da_ref, b_ref, c_ref, x_ref, hc_ref, hx_ref, nda_ref, nb_ref, w_ref, db_ref, dc_ref, dx_ref, dw_ref):
        i = pl.program_id(1)
        row = lax.broadcasted_iota(jnp.int32, (tr, tc), 0)
        cv, xv, dav = c_ref[...], x_ref[...], da_ref[...]
        p = cv * xv
        h1 = jnp.where(i > 0, hc_ref[7:8, :] * hx_ref[7:8, :], 0.0)
        h2 = jnp.where(i > 0, hc_ref[6:7, :] * hx_ref[6:7, :], 0.0)
        pm1, pm2 = _shift_down(p, h1, h2, row)
        pc = w_ref[0:1, :] * pm2 + w_ref[1:2, :] * pm1 + w_ref[2:3, :] * p
        db_ref[...] = (dav * pc).astype(jnp.bfloat16)
        dpc = dav * b_ref[...]
        n1 = jnp.where(i < ni - 1, nda_ref[0:1, :] * nb_ref[0:1, :], 0.0)
        n2 = jnp.where(i < ni - 1, nda_ref[1:2, :] * nb_ref[1:2, :], 0.0)
        dp1, dp2 = _shift_up(dpc, n1, n2, row)
        dp = w_ref[2:3, :] * dpc + w_ref[1:2, :] * dp1 + w_ref[0:1, :] * dp2
        dc_ref[...] = (dp * xv).astype(jnp.bfloat16)
        dx_ref[...] = (dp * cv).astype(jnp.bfloat16)

        @pl.when(i == 0)
        def _():
            dw_ref[...] = jnp.zeros_like(dw_ref)

        rows8 = lax.broadcasted_iota(jnp.int32, (HALO, tc), 0)
        s0 = jnp.sum(dpc * pm2, axis=0, keepdims=True)
        s1 = jnp.sum(dpc * pm1, axis=0, keepdims=True)
        s2 = jnp.sum(dpc * p, axis=0, keepdims=True)
        dw_ref[...] += jnp.where(rows8 == 0, s0, jnp.where(rows8 == 1, s1, jnp.where(rows8 == 2, s2, 0.0)))

    def blk(off):
        return pl.BlockSpec((tr, tc), lambda j, i: (i, j + off))

    def halo(off):
        return pl.BlockSpec((HALO, tc), lambda j, i: (jnp.maximum(i * hb - 1, 0), j + off))

    def nhalo(off):
        return pl.BlockSpec((HALO, tc), lambda j, i: (jnp.minimum((i + 1) * hb, t // HALO - 1), j + off))

    out = pl.BlockSpec((tr, tc), lambda j, i: (i, j))
    return pl.pallas_call(
        body, name=name, grid=(nj, ni),
        out_shape=[jax.ShapeDtypeStruct((t, d), jnp.bfloat16)] * 3 + [jax.ShapeDtypeStruct((HALO, d), jnp.float32)],
        in_specs=[blk(0), blk(0), blk(nj), blk(2 * nj), halo(nj), halo(2 * nj), nhalo(0), nhalo(0),
                  pl.BlockSpec((3, tc), lambda j, i: (0, j))],
        out_specs=[out, out, out, pl.BlockSpec((HALO, tc), lambda j, i: (0, j))],
        compiler_params=pltpu.CompilerParams(dimension_semantics=("parallel", "arbitrary")),
    )(da, z, z, z, z, z, da, z, conv_w)


@functools.partial(jax.custom_vjp, nondiff_argnums=(4,))
def _sconv_block(h, w_in, conv_w, w_out, name):
    return _sconv_block_fwd(h, w_in, conv_w, w_out, name)[0]


def _sconv_block_fwd(h, w_in, conv_w, w_out, name):
    hb = h.astype(jnp.bfloat16)
    z = _matmul(hb, w_in, "nn", name + "_in")
    a = _sconv_fwd_call(z, conv_w, name + "_gate")
    return _matmul(a, w_out, "nn", name + "_out"), (hb, z, a, w_in, conv_w, w_out)


def _sconv_block_bwd(name, saved, dy):
    hb, z, a, w_in, conv_w, w_out = saved
    dyb = dy.astype(jnp.bfloat16)
    da = _matmul(dyb, w_out, "nt", name + "_out_dx")
    dw_out = _matmul(a, dyb, "tn", name + "_out_dw", jnp.bfloat16)
    db, dc, dx, dcw = _sconv_bwd_call(da, z, conv_w, name + "_gate_bwd")
    dz = jnp.concatenate([db, dc, dx], axis=1)
    dh = _matmul(dz, w_in, "nt", name + "_in_dx")
    dw_in = _matmul(hb, dz, "tn", name + "_in_dw", jnp.bfloat16)
    return dh, dw_in, dcw[:3], dw_out


_sconv_block.defvjp(_sconv_block_fwd, _sconv_block_bwd)


def _to_heads(x, heads):
    t = x.shape[0]
    return x.reshape(t, heads, -1).transpose(1, 0, 2)


def _from_heads(x):
    h, t, d = x.shape
    return x.transpose(1, 0, 2).reshape(t, h * d)


def _partial_rope(x, positions):
    half = ROT_DIM // 2
    inv_freq = ROPE_THETA ** (-jnp.arange(half, dtype=jnp.float32) / half)
    ang = positions.astype(jnp.float32)[:, None] * inv_freq[None, :]
    cos = jnp.cos(ang)[:, None, :]
    sin = jnp.sin(ang)[:, None, :]
    x1, x2 = x[..., :half], x[..., half:ROT_DIM]
    return jnp.concatenate([x1 * cos - x2 * sin, x2 * cos + x1 * sin, x[..., ROT_DIM:]], axis=-1)


def _swa_probs(q_ref, kp_ref, kc_ref, sink_ref, n, w, scale):
    grp, d = q_ref.shape[0], q_ref.shape[2]
    q2 = q_ref[...].reshape(grp * w, d)
    kcat = jnp.concatenate([kp_ref[0], kc_ref[0]], axis=0)
    s3 = (_dot(q2, kcat, _NT) * scale).reshape(grp, w, 2 * w)
    qi = lax.broadcasted_iota(jnp.int32, s3.shape, 1)
    kj = lax.broadcasted_iota(jnp.int32, s3.shape, 2)
    diff = qi + w - kj
    allowed = (diff >= 0) & (diff < w) & ((kj >= w) | (n > 0))
    s3 = jnp.where(allowed, s3, NEG)
    sink = sink_ref[...]
    m = jnp.maximum(jnp.max(s3, axis=2, keepdims=True), sink)
    e = jnp.exp(s3 - m)
    es = jnp.exp(sink - m)
    z = jnp.sum(e, axis=2, keepdims=True) + es
    return q2, kcat, e / z, es / z


def _swa_fwd_call(q, k, v, sinks, name):
    qh, t, d = q.shape
    kv = k.shape[0]
    grp, w = qh // kv, SWA_WINDOW
    nblk = t // w
    scale = d ** -0.5

    def body(q_ref, kp_ref, kc_ref, vp_ref, vc_ref, sink_ref, o_ref):
        n = pl.program_id(1)
        _, _, p3, _ = _swa_probs(q_ref, kp_ref, kc_ref, sink_ref, n, w, scale)
        vcat = jnp.concatenate([vp_ref[0], vc_ref[0]], axis=0)
        o = _dot(p3.reshape(grp * w, 2 * w), vcat, _NN)
        o_ref[...] = o.reshape(grp, w, d).astype(o_ref.dtype)

    qspec = pl.BlockSpec((grp, w, d), lambda h, n: (h, n, 0))
    prev = pl.BlockSpec((1, w, d), lambda h, n: (h, jnp.maximum(n - 1, 0), 0))
    cur = pl.BlockSpec((1, w, d), lambda h, n: (h, n, 0))
    return pl.pallas_call(
        body, name=name, grid=(kv, nblk), out_shape=jax.ShapeDtypeStruct((qh, t, d), jnp.bfloat16),
        in_specs=[qspec, prev, cur, prev, cur, pl.BlockSpec((grp, 1, 1), lambda h, n: (h, 0, 0))],
        out_specs=qspec,
    )(q, k, k, v, v, sinks)


def _swa_bwd_call(q, k, v, sinks, do, name):
    qh, t, d = q.shape
    kv = k.shape[0]
    grp, w = qh // kv, SWA_WINDOW
    nblk = t // w
    scale = d ** -0.5

    def body(q_ref, kp_ref, kc_ref, vp_ref, vc_ref, sink_ref, do_ref, dq_ref, dk_ref, dv_ref, ds_ref,
             carry_k, carry_v, part_k, part_v):
        n = pl.program_id(1)

        @pl.when(n == 0)
        def _():
            carry_k[...] = jnp.zeros_like(carry_k)
            carry_v[...] = jnp.zeros_like(carry_v)
            ds_ref[...] = jnp.zeros_like(ds_ref)

        @pl.when(n < nblk)
        def _():
            q2, kcat, p3, ps = _swa_probs(q_ref, kp_ref, kc_ref, sink_ref, n, w, scale)
            vcat = jnp.concatenate([vp_ref[0], vc_ref[0]], axis=0)
            do2 = do_ref[...].reshape(grp * w, d)
            dp3 = _dot(do2, vcat, _NT).reshape(grp, w, 2 * w)
            rs = jnp.sum(p3 * dp3, axis=2, keepdims=True)
            ds2 = (p3 * (dp3 - rs)).reshape(grp * w, 2 * w)
            dsink = -jnp.sum(ps * rs, axis=1, keepdims=True)
            ds_ref[...] += jnp.broadcast_to(dsink, ds_ref.shape)
            dq_ref[...] = (_dot(ds2, kcat, _NN) * scale).reshape(grp, w, d)
            part_k[...] = _dot(ds2, q2, _TN) * scale
            part_v[...] = _dot(p3.reshape(grp * w, 2 * w), do2, _TN)

        @pl.when(n == nblk)
        def _():
            part_k[...] = jnp.zeros_like(part_k)
            part_v[...] = jnp.zeros_like(part_v)

        dk_ref[0] = carry_k[...] + part_k[0:w, :]
        dv_ref[0] = carry_v[...] + part_v[0:w, :]
        carry_k[...] = part_k[w:2 * w, :]
        carry_v[...] = part_v[w:2 * w, :]

    last = nblk - 1
    qspec = pl.BlockSpec((grp, w, d), lambda h, n: (h, jnp.minimum(n, last), 0))
    prev = pl.BlockSpec((1, w, d), lambda h, n: (h, jnp.maximum(jnp.minimum(n, last) - 1, 0), 0))
    cur = pl.BlockSpec((1, w, d), lambda h, n: (h, jnp.minimum(n, last), 0))
    kout = pl.BlockSpec((1, w, d), lambda h, n: (h, jnp.maximum(n - 1, 0), 0))
    return pl.pallas_call(
        body, name=name, grid=(kv, nblk + 1),
        out_shape=[jax.ShapeDtypeStruct((qh, t, d), jnp.float32), jax.ShapeDtypeStruct((kv, t, d), jnp.float32),
                   jax.ShapeDtypeStruct((kv, t, d), jnp.float32), jax.ShapeDtypeStruct((qh, 1, 128), jnp.float32)],
        in_specs=[qspec, prev, cur, prev, cur, pl.BlockSpec((grp, 1, 1), lambda h, n: (h, 0, 0)), qspec],
        out_specs=[qspec, kout, kout, pl.BlockSpec((grp, 1, 128), lambda h, n: (h, 0, 0))],
        scratch_shapes=[pltpu.VMEM((w, d), jnp.float32), pltpu.VMEM((w, d), jnp.float32),
                        pltpu.VMEM((2 * w, d), jnp.float32), pltpu.VMEM((2 * w, d), jnp.float32)],
        compiler_params=pltpu.CompilerParams(dimension_semantics=("parallel", "arbitrary")),
    )(q, k, k, v, v, sinks, do)


def _swa_split(z, positions, qh, kv, d):
    q, k, v = jnp.split(z, [qh * d, qh * d + kv * d], axis=-1)
    t = z.shape[0]
    q = _partial_rope(q.reshape(t, qh, d), positions).transpose(1, 0, 2)
    k = _partial_rope(k.reshape(t, kv, d), positions).transpose(1, 0, 2)
    v = v.reshape(t, kv, d).transpose(1, 0, 2)
    return q, k, v


@functools.partial(jax.custom_vjp, nondiff_argnums=(5,))
def _swa_block(h, positions, w_in, w_out, sinks, name):
    return _swa_block_fwd(h, positions, w_in, w_out, sinks, name)[0]


def _swa_block_fwd(h, positions, w_in, w_out, sinks, name):
    d = SWA_HEAD_DIM
    qh = h.shape[1] // d
    kv = qh // SWA_GROUP
    hb = h.astype(jnp.bfloat16)
    z = _matmul(hb, w_in, "nt", name + "_in")
    q, k, v = [a.astype(jnp.bfloat16) for a in _swa_split(z, positions, qh, kv, d)]
    o = _swa_fwd_call(q, k, v, sinks.reshape(qh, 1, 1), name + "_attn")
    a = _from_heads(o)
    return _matmul(a, w_out, "nn", name + "_out"), (hb, positions, z, q, k, v, a, w_in, w_out, sinks)


def _swa_block_bwd(name, saved, dy):
    hb, positions, z, q, k, v, a, w_in, w_out, sinks = saved
    d = SWA_HEAD_DIM
    qh = q.shape[0]
    kv = k.shape[0]
    dyb = dy.astype(jnp.bfloat16)
    da = _matmul(dyb, w_out, "nt", name + "_out_dx")
    dw_out = _matmul(a, dyb, "tn", name + "_out_dw", jnp.bfloat16)
    do = _to_heads(da, qh)
    dq, dk, dv, dsinks = _swa_bwd_call(q, k, v, sinks.reshape(qh, 1, 1), do, name + "_attn_bwd")
    _, split_vjp = jax.vjp(lambda zz: _swa_split(zz, positions, qh, kv, d), z)
    dz = split_vjp((dq, dk, dv))[0].astype(jnp.bfloat16)
    dh = _matmul(dz, w_in, "nn", name + "_in_dx")
    dw_in = _matmul(dz, hb, "tn", name + "_in_dw", jnp.bfloat16)
    return dh, None, dw_in, dw_out, dsinks[:, 0, 0].reshape(sinks.shape)


_swa_block.defvjp(_swa_block_fwd, _swa_block_bwd)


def _fox_specs(hp, nb, tb, d):
    blk = pl.BlockSpec((hp, 1, tb, d), lambda h, i: (h, i, 0, 0))
    whole = pl.BlockSpec((hp, nb, tb, d), lambda h, i: (h, 0, 0, 0))
    rowblk = pl.BlockSpec((hp, 1, 1, tb), lambda h, i: (h, i, 0, 0))
    rowwhole = pl.BlockSpec((hp, nb, 1, tb), lambda h, i: (h, 0, 0, 0))
    return blk, whole, rowblk, rowwhole


def _fox_fwd_call(q, k, v, g, bias, name):
    hh, nb, tb, d = q.shape
    scale = d ** -0.5

    hp = FOX_HEADS_PER_STEP

    def body(q_ref, k_ref, v_ref, g_ref, b_ref, og_ref, o_ref, lse_ref):
        i = pl.program_id(1)
        qs = [q_ref[a, 0] for a in range(hp)]
        rows = lax.broadcasted_iota(jnp.int32, (tb, tb), 0)
        cols = lax.broadcasted_iota(jnp.int32, (tb, tb), 1)

        def step(kb, carry, masked):
            out = []
            for a in range(hp):
                m, l, acc = carry[a]
                s = _dot(qs[a], k_ref[a, kb], _NT) + b_ref[a, kb]
                if masked:
                    s = jnp.where(cols <= rows, s, NEG)
                m_new = jnp.maximum(m, jnp.max(s, axis=1, keepdims=True))
                alpha = jnp.exp(m - m_new)
                p = jnp.exp(s - m_new)
                out.append((m_new, alpha * l + jnp.sum(p, axis=1, keepdims=True),
                            alpha * acc + _dot(p, v_ref[a, kb], _NN)))
            return tuple(out)

        init = (jnp.full((tb, 1), NEG, jnp.float32), jnp.zeros((tb, 1), jnp.float32), jnp.zeros((tb, d), jnp.float32))
        carry = lax.fori_loop(0, i, lambda kb, c: step(kb, c, False), (init,) * hp)
        for a, (m, l, acc) in enumerate(step(i, carry, True)):
            o = acc / l
            o_ref[a, 0] = o
            og_ref[a, 0] = (o * _sigmoid(g_ref[a, 0])).astype(og_ref.dtype)
            lse_ref[a, 0] = _col_to_row(m + jnp.log(l))

    blk, whole, rowblk, rowwhole = _fox_specs(hp, nb, tb, d)
    return pl.pallas_call(
        body, name=name, grid=(hh // hp, nb),
        out_shape=[jax.ShapeDtypeStruct(q.shape, jnp.bfloat16), jax.ShapeDtypeStruct(q.shape, jnp.float32),
                   jax.ShapeDtypeStruct((hh, nb, 1, tb), jnp.float32)],
        in_specs=[blk, whole, whole, blk, rowwhole],
        out_specs=[blk, blk, rowblk],
    )(q, k, v, g, bias)


def _fox_dq_call(q, k, v, g, o, dog, bias, lse, name):
    hh, nb, tb, d = q.shape
    scale = d ** -0.5

    hp = FOX_HEADS_PER_STEP

    def body(q_ref, k_ref, v_ref, g_ref, o_ref, dog_ref, b_ref, lse_ref, dq_ref, dg_ref, do_ref, delta_ref):
        i = pl.program_id(1)
        rows = lax.broadcasted_iota(jnp.int32, (tb, tb), 0)
        cols = lax.broadcasted_iota(jnp.int32, (tb, tb), 1)
        qs, dos, deltas, lses = [], [], [], []
        for a in range(hp):
            sig = _sigmoid(g_ref[a, 0])
            ov, dogv = o_ref[a, 0], dog_ref[a, 0]
            do = dogv * sig
            dg_ref[a, 0] = dogv * ov * sig * (1.0 - sig)
            do_ref[a, 0] = do.astype(do_ref.dtype)
            qs.append(q_ref[a, 0])
            dos.append(do.astype(jnp.bfloat16))
            deltas.append(jnp.sum(do * ov, axis=1, keepdims=True))
            lses.append(_row_to_col(lse_ref[a, 0]))

        def step(kb, carry, masked):
            out = []
            for a in range(hp):
                dq, spdp, sp = carry[a]
                s = _dot(qs[a], k_ref[a, kb], _NT) + b_ref[a, kb]
                p = jnp.exp(s - lses[a])
                if masked:
                    p = jnp.where(cols <= rows, p, 0.0)
                dp = _dot(dos[a], v_ref[a, kb], _NT)
                out.append((dq + _dot(p * (dp - deltas[a]), k_ref[a, kb], _NN),
                            spdp + jnp.sum(p * dp, axis=1, keepdims=True), sp + jnp.sum(p, axis=1, keepdims=True)))
            return tuple(out)

        zcol = jnp.zeros((tb, 1), jnp.float32)
        init = (jnp.zeros((tb, d), jnp.float32), zcol, zcol)
        carry = lax.fori_loop(0, i, lambda kb, c: step(kb, c, False), (init,) * hp)
        for a, (dq, spdp, sp) in enumerate(step(i, carry, True)):
            dq_ref[a, 0] = dq * scale
            delta_ref[a, 0] = _col_to_row(spdp / sp)

    blk, whole, rowblk, rowwhole = _fox_specs(hp, nb, tb, d)
    return pl.pallas_call(
        body, name=name, grid=(hh // hp, nb),
        out_shape=[jax.ShapeDtypeStruct(q.shape, jnp.float32), jax.ShapeDtypeStruct(q.shape, jnp.float32),
                   jax.ShapeDtypeStruct(q.shape, jnp.bfloat16), jax.ShapeDtypeStruct((hh, nb, 1, tb), jnp.float32)],
        in_specs=[blk, whole, whole, blk, blk, blk, rowwhole, rowblk],
        out_specs=[blk, blk, blk, rowblk],
    )(q, k, v, g, o, dog, bias, lse)


def _fox_dkv_call(q, k, v, do, bias, lse, delta, name):
    hh, nb, tb, d = q.shape
    scale = d ** -0.5

    hp = FOX_HEADS_PER_STEP

    def body(q_ref, k_ref, v_ref, do_ref, b_ref, lse_ref, delta_ref, dk_ref, dv_ref, db_ref):
        j = pl.program_id(1)
        ks = [k_ref[a, 0] for a in range(hp)]
        vs = [v_ref[a, 0] for a in range(hp)]
        bcols = [_row_to_col(b_ref[a, 0]) for a in range(hp)]
        rows = lax.broadcasted_iota(jnp.int32, (tb, tb), 0)
        cols = lax.broadcasted_iota(jnp.int32, (tb, tb), 1)

        def step(qb, carry, masked):
            out = []
            for a in range(hp):
                dk, dv, db = carry[a]
                st = _dot(ks[a], q_ref[a, qb], _NT) + bcols[a] - lse_ref[a, qb]
                pt = jnp.exp(st)
                if masked:
                    pt = jnp.where(rows <= cols, pt, 0.0)
                dpt = _dot(vs[a], do_ref[a, qb], _NT)
                dst = pt * (dpt - delta_ref[a, qb])
                out.append((dk + _dot(dst, q_ref[a, qb], _NN), dv + _dot(pt, do_ref[a, qb], _NN),
                            db + jnp.sum(dst, axis=1, keepdims=True)))
            return tuple(out)

        init = (jnp.zeros((tb, d), jnp.float32), jnp.zeros((tb, d), jnp.float32), jnp.zeros((tb, 1), jnp.float32))
        carry = step(j, (init,) * hp, True)
        for a, (dk, dv, db) in enumerate(lax.fori_loop(j + 1, nb, lambda qb, c: step(qb, c, False), carry)):
            dk_ref[a, 0] = dk
            dv_ref[a, 0] = dv
            db_ref[a, 0] = _col_to_row(db)

    blk, whole, rowblk, rowwhole = _fox_specs(hp, nb, tb, d)
    return pl.pallas_call(
        body, name=name, grid=(hh // hp, nb),
        out_shape=[jax.ShapeDtypeStruct(q.shape, jnp.float32), jax.ShapeDtypeStruct(q.shape, jnp.float32),
                   jax.ShapeDtypeStruct((hh, nb, 1, tb), jnp.float32)],
        in_specs=[whole, blk, blk, whole, rowblk, rowwhole, rowwhole],
        out_specs=[blk, blk, rowblk],
    )(q, k, v, do, bias, lse, delta)


def _fox_bias_fwd_call(fl, b_f, name):
    t, lanes = fl.shape
    tr = _tile(t, ROW_BLOCK)

    def body(fl_ref, b_ref, o_ref, carry):
        i = pl.program_id(0)

        @pl.when(i == 0)
        def _():
            carry[...] = jnp.zeros_like(carry)

        xv = fl_ref[...] + b_ref[...]
        ls = jnp.minimum(xv, 0.0) - jnp.log(1.0 + jnp.exp(-jnp.abs(xv)))
        pos = lax.broadcasted_iota(jnp.int32, (tr, lanes), 0)
        o_ref[...] = -(_group_cumsum(ls, tr, pos) + carry[...])
        carry[...] += jnp.sum(ls, axis=0, keepdims=True)

    blk = pl.BlockSpec((tr, lanes), lambda i: (i, 0))
    return pl.pallas_call(
        body, name=name, grid=(t // tr,), out_shape=jax.ShapeDtypeStruct((t, lanes), jnp.float32),
        in_specs=[blk, pl.BlockSpec((1, lanes), lambda i: (0, 0))], out_specs=blk,
        scratch_shapes=[pltpu.VMEM((1, lanes), jnp.float32)],
        compiler_params=pltpu.CompilerParams(dimension_semantics=("arbitrary",)),
    )(fl, b_f)


def _fox_bias_bwd_call(dbias, fl, b_f, name):
    t, lanes = fl.shape
    tr = _tile(t, ROW_BLOCK)
    last = t // tr - 1

    def body(db_ref, fl_ref, b_ref, dfl_ref, dbf_ref, carry):
        i = pl.program_id(0)

        @pl.when(i == 0)
        def _():
            carry[...] = jnp.zeros_like(carry)
            dbf_ref[...] = jnp.zeros_like(dbf_ref)

        dbv = db_ref[...]
        pos = lax.broadcasted_iota(jnp.int32, (tr, lanes), 0)
        dls = -(_group_rcumsum(dbv, tr, pos) + carry[...])
        carry[...] += jnp.sum(dbv, axis=0, keepdims=True)
        dfl = dls * _sigmoid(-(fl_ref[...] + b_ref[...]))
        dfl_ref[...] = dfl
        dbf_ref[...] += jnp.sum(dfl, axis=0, keepdims=True)

    blk = pl.BlockSpec((tr, lanes), lambda i: (last - i, 0))
    one = pl.BlockSpec((1, lanes), lambda i: (0, 0))
    return pl.pallas_call(
        body, name=name, grid=(t // tr,),
        out_shape=[jax.ShapeDtypeStruct((t, lanes), jnp.float32), jax.ShapeDtypeStruct((1, lanes), jnp.float32)],
        in_specs=[blk, blk, one], out_specs=[blk, one],
        scratch_shapes=[pltpu.VMEM((1, lanes), jnp.float32)],
        compiler_params=pltpu.CompilerParams(dimension_semantics=("arbitrary",)),
    )(dbias, fl, b_f)


@functools.partial(jax.custom_vjp, nondiff_argnums=(5,))
def _fox_block(h, w_qkvg, w_f, b_f, w_out, name):
    return _fox_block_fwd(h, w_qkvg, w_f, b_f, w_out, name)[0]


def _fox_block_fwd(h, w_qkvg, w_f, b_f, w_out, name):
    t, dm = h.shape
    d = FOX_HEAD_DIM
    heads = dm // d
    lanes = w_f.shape[0]
    tb = _tile(t, ATT_BLOCK)
    nb = t // tb
    hb = h.astype(jnp.bfloat16)
    z = _matmul(hb, w_qkvg, "nt", name + "_in")
    fl = _matmul(hb, w_f, "nt", name + "_inf")
    z4 = z.reshape(t, 4, heads, d).transpose(1, 2, 0, 3).reshape(4, heads, nb, tb, d)
    q = (z4[0] * d ** -0.5).astype(jnp.bfloat16)
    k, v = z4[1].astype(jnp.bfloat16), z4[2].astype(jnp.bfloat16)
    g = z4[3]
    b_fp = jnp.pad(b_f, (0, lanes - heads)).reshape(1, lanes)
    bias = _fox_bias_fwd_call(fl, b_fp, name + "_bias")[:, :heads].T.reshape(heads, nb, 1, tb)
    og, o, lse = _fox_fwd_call(q, k, v, g, bias, name + "_attn")
    a = _from_heads(og.reshape(heads, t, d))
    return _matmul(a, w_out, "nn", name + "_out"), (hb, q, k, v, g, bias, o, lse, a, fl, b_fp, w_qkvg, w_f, w_out)


def _fox_block_bwd(name, saved, dy):
    hb, q, k, v, g, bias, o, lse, a, fl, b_fp, w_qkvg, w_f, w_out = saved
    heads, nb, tb, d = q.shape
    t = nb * tb
    lanes = w_f.shape[0]
    dyb = dy.astype(jnp.bfloat16)
    da = _matmul(dyb, w_out, "nt", name + "_out_dx")
    dw_out = _matmul(a, dyb, "tn", name + "_out_dw", jnp.bfloat16)
    dog = _to_heads(da, heads).reshape(heads, nb, tb, d)
    dq, dg, do, delta = _fox_dq_call(q, k, v, g, o, dog, bias, lse, name + "_attn_dq")
    dk, dv, dbias = _fox_dkv_call(q, k, v, do, bias, lse, delta, name + "_attn_dkv")
    dz = jnp.stack([dq, dk, dv, dg]).reshape(4, heads, t, d).transpose(2, 0, 1, 3).reshape(t, 4 * heads * d)
    dz = dz.astype(jnp.bfloat16)
    dbias_rows = jnp.pad(dbias.reshape(heads, t).T, ((0, 0), (0, lanes - heads)))
    dfl, db_fp = _fox_bias_bwd_call(dbias_rows, fl, b_fp, name + "_bias_bwd")
    dflb = dfl.astype(jnp.bfloat16)
    dh = _matmul(dz, w_qkvg, "nn", name + "_in_dx") + _matmul(dflb, w_f, "nn", name + "_inf_dx")
    dw_qkvg = _matmul(dz, hb, "tn", name + "_in_dw", jnp.bfloat16)
    dw_f = _matmul(dflb, hb, "tn", name + "_inf_dw", jnp.bfloat16)
    return dh, dw_qkvg, dw_f, db_fp[0, :heads], dw_out


_fox_block.defvjp(_fox_block_fwd, _fox_block_bwd)


def _group_cumsum(x, c, pos):
    sh = 1
    while sh < c:
        x = x + jnp.where(pos >= sh, pltpu.roll(x, sh, 0), 0.0)
        sh *= 2
    return x


def _group_rcumsum(x, c, pos):
    n = x.shape[0]
    sh = 1
    while sh < c:
        x = x + jnp.where(pos < c - sh, pltpu.roll(x, n - sh, 0), 0.0)
        sh *= 2
    return x


def _hgrn_prep(zq_ref, zf_ref, zi_ref, lb_ref, b_scr, rows, c):
    nb = rows // c
    zq, zf = zq_ref[...], zf_ref[...]
    lbv = lb_ref[...]
    sq = _sigmoid(zq)
    sf = _sigmoid(zf)
    f = lbv + (1.0 - lbv) * sf
    pos = lax.broadcasted_iota(jnp.int32, (rows, 128), 0) % c
    b = _group_cumsum(jnp.log(f), c, pos)
    sh = (nb, c, 128)
    q3, k3, v3, b3 = (zq * sq).reshape(sh), (1.0 - f).reshape(sh), zi_ref[...].reshape(sh), b.reshape(sh)
    b_scr[...] = b3
    glast = b_scr[:, c - 1:c, :]
    eb = jnp.exp(b3)
    ek = jnp.exp(glast - b3)
    return dict(zq=zq, sq=sq, sf=sf, f=f, pos=pos, q3=q3, k3=k3, v3=v3, b3=b3, eb=eb, ek=ek,
                qt=q3 * eb, kh=k3 * ek, dec=jnp.exp(glast))


def _hgrn_fwd_call(z, lb, norm_g, name):
    t, w4 = z.shape
    w = w4 // 4
    heads = w // HGRN_EXPAND
    rows, c = _tile(t, HGRN_ROWS), HGRN_SUB
    nb, ntb = rows // c, t // rows

    def body(zq_ref, zf_ref, zi_ref, zg_ref, lb_ref, ng_ref, og_ref, o_ref, ck_ref, st_ref, b_scr, k_scr, v_scr):
        tb = pl.program_id(1)

        @pl.when(tb == 0)
        def _():
            st_ref[...] = jnp.zeros_like(st_ref)

        ck_ref[0, 0] = st_ref[...]
        p = _hgrn_prep(zq_ref, zf_ref, zi_ref, lb_ref, b_scr, rows, c)
        k_scr[...] = p["k3"]
        v_scr[...] = p["v3"]
        tpos = lax.broadcasted_iota(jnp.int32, (nb, c, 128), 1)
        od = jnp.zeros((nb, c, 128), jnp.float32)
        for s in range(c):
            bs, ks, vs = b_scr[:, s:s + 1, :], k_scr[:, s:s + 1, :], v_scr[:, s:s + 1, :]
            e = jnp.exp(jnp.minimum(p["b3"] - bs, 0.0))
            a = jnp.sum(jnp.where(tpos >= s, p["q3"] * ks * e, 0.0), axis=2, keepdims=True)
            od = od + a * vs
        st = st_ref[...]
        for i in range(nb):
            o_ref[pl.ds(i * c, c), :] = _dot(p["qt"][i], st, _NT) + od[i]
            st = st * p["dec"][i] + _dot(p["v3"][i], p["kh"][i], _TN)
        st_ref[...] = st
        o = o_ref[...]
        rstd = lax.rsqrt(jnp.mean(o * o, axis=1, keepdims=True) + RMS_EPS)
        zg = zg_ref[...]
        og_ref[...] = (o * rstd * ng_ref[...] * (zg * _sigmoid(zg))).astype(og_ref.dtype)

    def col(off):
        return pl.BlockSpec((rows, 128), lambda h, tb: (tb, h + off))

    out = pl.BlockSpec((rows, 128), lambda h, tb: (tb, h))
    return pl.pallas_call(
        body, name=name, grid=(heads, ntb),
        out_shape=[jax.ShapeDtypeStruct((t, w), jnp.bfloat16), jax.ShapeDtypeStruct((t, w), jnp.float32),
                   jax.ShapeDtypeStruct((heads, ntb, 128, 128), jnp.float32)],
        in_specs=[col(0), col(heads), col(2 * heads), col(3 * heads), pl.BlockSpec((1, 128), lambda h, tb: (0, h)),
                  pl.BlockSpec((1, 128), lambda h, tb: (0, 0))],
        out_specs=[out, out, pl.BlockSpec((1, 1, 128, 128), lambda h, tb: (h, tb, 0, 0))],
        scratch_shapes=[pltpu.VMEM((128, 128), jnp.float32)] + [pltpu.VMEM((nb, c, 128), jnp.float32)] * 3,
        compiler_params=pltpu.CompilerParams(dimension_semantics=("parallel", "arbitrary")),
    )(z, z, z, z, lb, norm_g)


def _hgrn_bwd_call(z, lb, norm_g, o_raw, dog, ck, name):
    t, w4 = z.shape
    w = w4 // 4
    heads = w // HGRN_EXPAND
    rows, c = _tile(t, HGRN_ROWS), HGRN_SUB
    nb, ntb = rows // c, t // rows

    def body(zq_ref, zf_ref, zi_ref, zg_ref, lb_ref, ng_ref, o_ref, dog_ref, ck_ref,
             dzq_ref, dzf_ref, dzi_ref, dzg_ref, dlb_ref, dng_ref,
             dst_ref, s_store, b_scr, k_scr, v_scr, dqt_scr, dkh_scr, dv_scr, ddec_scr):
        tb = pl.program_id(1)

        @pl.when(tb == 0)
        def _():
            dst_ref[...] = jnp.zeros_like(dst_ref)
            dlb_ref[...] = jnp.zeros_like(dlb_ref)
            dng_ref[...] = jnp.zeros_like(dng_ref)

        p = _hgrn_prep(zq_ref, zf_ref, zi_ref, lb_ref, b_scr, rows, c)
        q3, k3, v3, b3 = p["q3"], p["k3"], p["v3"], p["b3"]

        o = o_ref[...]
        rstd = lax.rsqrt(jnp.mean(o * o, axis=1, keepdims=True) + RMS_EPS)
        xh = o * rstd
        zg = zg_ref[...]
        sg = _sigmoid(zg)
        ng = ng_ref[...]
        dogv = dog_ref[...]
        don = dogv * (zg * sg)
        dzg_ref[...] = (dogv * (xh * ng) * (sg * (1.0 + zg * (1.0 - sg)))).astype(dzg_ref.dtype)
        dng_ref[0] += jnp.sum(don * xh, axis=0, keepdims=True)
        dyg = don * ng
        do = rstd * (dyg - xh * jnp.mean(dyg * xh, axis=1, keepdims=True))
        do3 = do.reshape(nb, c, 128)

        st = ck_ref[0, 0]
        for i in range(nb):
            s_store[i] = st
            st = st * p["dec"][i] + _dot(v3[i], p["kh"][i], _TN)

        dst = dst_ref[...]
        for i in reversed(range(nb)):
            sl = pl.ds(i * c, c)
            st_i = s_store[i]
            dqt_scr[sl, :] = _dot(do3[i], st_i, _NN)
            dv_scr[sl, :] = _dot(p["kh"][i], dst, _NT)
            dkh_scr[sl, :] = _dot(v3[i], dst, _NN)
            ddec_scr[i] = jnp.broadcast_to(jnp.sum(st_i * dst, axis=0, keepdims=True), (8, 128))
            dst = dst * p["dec"][i] + _dot(do3[i], p["qt"][i], _TN)
        dst_ref[...] = dst

        k_scr[...] = k3
        v_scr[...] = v3
        tpos = lax.broadcasted_iota(jnp.int32, (nb, c, 128), 1)
        zero = jnp.zeros((nb, c, 128), jnp.float32)
        dqd, dkd, dvd, dbd = zero, zero, zero, zero
        for s in range(c):
            bs, ks, vs = b_scr[:, s:s + 1, :], k_scr[:, s:s + 1, :], v_scr[:, s:s + 1, :]
            e = jnp.where(tpos >= s, jnp.exp(jnp.minimum(b3 - bs, 0.0)), 0.0)
            qe = q3 * e
            a = jnp.sum(qe * ks, axis=2, keepdims=True)
            da = jnp.sum(do3 * vs, axis=2, keepdims=True)
            y = da * (ks * e)
            dqd = dqd + y
            dbd = dbd + y * q3
            dks = jnp.sum(da * qe, axis=1, keepdims=True)
            dvs = jnp.sum(a * do3, axis=1, keepdims=True)
            at_s = tpos == s
            dkd = dkd + jnp.where(at_s, dks, 0.0)
            dvd = dvd + jnp.where(at_s, dvs, 0.0)
            dbd = dbd - jnp.where(at_s, dks * ks, 0.0)

        dqt3 = dqt_scr[...].reshape(nb, c, 128)
        dkh3 = dkh_scr[...].reshape(nb, c, 128)
        ddec = ddec_scr[:, 0:1, :]
        dq_act = dqt3 * p["eb"] + dqd
        dk = dkh3 * p["ek"] + dkd
        khd = p["kh"] * dkh3
        db = p["qt"] * dqt3 - khd + dbd
        dglast = jnp.sum(khd, axis=1, keepdims=True) + ddec * p["dec"]
        db = db + jnp.where(tpos == c - 1, dglast, 0.0)
        dlogf = _group_rcumsum(db.reshape(rows, 128), c, p["pos"])

        zq, sq, sf, f = p["zq"], p["sq"], p["sf"], p["f"]
        lbv = lb_ref[...]
        dzq_ref[...] = (dq_act.reshape(rows, 128) * (sq * (1.0 + zq * (1.0 - sq)))).astype(dzq_ref.dtype)
        df = dlogf / f - dk.reshape(rows, 128)
        dzf_ref[...] = (df * (1.0 - lbv) * (sf * (1.0 - sf))).astype(dzf_ref.dtype)
        dlb_ref[...] += jnp.sum(df * (1.0 - sf), axis=0, keepdims=True)
        dzi_ref[...] = (dv_scr[...] + dvd.reshape(rows, 128)).astype(dzi_ref.dtype)

    last = ntb - 1

    def col(off):
        return pl.BlockSpec((rows, 128), lambda h, tb: (last - tb, h + off))

    out = pl.BlockSpec((rows, 128), lambda h, tb: (last - tb, h))
    lbs = pl.BlockSpec((1, 128), lambda h, tb: (0, h))
    big = pltpu.VMEM((rows, 128), jnp.float32)
    return pl.pallas_call(
        body, name=name, grid=(heads, ntb),
        out_shape=[jax.ShapeDtypeStruct((t, w), jnp.bfloat16)] * 4
        + [jax.ShapeDtypeStruct((1, w), jnp.float32), jax.ShapeDtypeStruct((heads, 1, 128), jnp.float32)],
        in_specs=[col(0), col(heads), col(2 * heads), col(3 * heads), lbs, pl.BlockSpec((1, 128), lambda h, tb: (0, 0)),
                  out, out, pl.BlockSpec((1, 1, 128, 128), lambda h, tb: (h, last - tb, 0, 0))],
        out_specs=[out, out, out, out, lbs, pl.BlockSpec((1, 1, 128), lambda h, tb: (h, 0, 0))],
        scratch_shapes=[pltpu.VMEM((128, 128), jnp.float32), pltpu.VMEM((nb, 128, 128), jnp.float32)]
        + [pltpu.VMEM((nb, c, 128), jnp.float32)] * 3 + [big] * 3 + [pltpu.VMEM((nb, 8, 128), jnp.float32)],
        compiler_params=pltpu.CompilerParams(dimension_semantics=("parallel", "arbitrary")),
    )(z, z, z, z, lb, norm_g, o_raw, dog, ck)


@functools.partial(jax.custom_vjp, nondiff_argnums=(5,))
def _hgrn_block(h, w_in, w_out, norm_g, lb, name):
    return _hgrn_block_fwd(h, w_in, w_out, norm_g, lb, name)[0]


def _hgrn_block_fwd(h, w_in, w_out, norm_g, lb, name):
    hb = h.astype(jnp.bfloat16)
    z = _matmul(hb, w_in, "nn", name + "_in")
    og, o_raw, ck = _hgrn_fwd_call(z, lb.reshape(1, -1), norm_g.reshape(1, -1), name + "_scan")
    return _matmul(og, w_out, "nn", name + "_out"), (hb, z, og, o_raw, ck, w_in, w_out, norm_g, lb)


def _hgrn_block_bwd(name, saved, dy):
    hb, z, og, o_raw, ck, w_in, w_out, norm_g, lb = saved
    dyb = dy.astype(jnp.bfloat16)
    dog = _matmul(dyb, w_out, "nt", name + "_out_dx")
    dw_out = _matmul(og, dyb, "tn", name + "_out_dw", jnp.bfloat16)
    dzq, dzf, dzi, dzg, dlb, dng = _hgrn_bwd_call(z, lb.reshape(1, -1), norm_g.reshape(1, -1), o_raw, dog, ck,
                                                  name + "_scan_bwd")
    dz = jnp.concatenate([dzq, dzf, dzi, dzg], axis=1)
    dh = _matmul(dz, w_in, "nt", name + "_in_dx")
    dw_in = _matmul(hb, dz, "tn", name + "_in_dw", jnp.bfloat16)
    return dh, dw_in, dw_out, jnp.sum(dng, axis=0).reshape(norm_g.shape), dlb.reshape(lb.shape)


_hgrn_block.defvjp(_hgrn_block_fwd, _hgrn_block_bwd)


def _layer_plan(sharded):
    mixers = [
        [("hgrn_w_in", "col"), ("hgrn_w_out", "row")],
        [("swa_w_in", "colT"), ("swa_w_out", "row")],
        [("sc_w_in", "col"), ("sc_conv_w", "exact"), ("sc_w_out", "row")],
        [("fox_w_in", "colT"), ("fox_w_out", "row")],
    ]
    depth = sharded["ffn_w_up"].shape[0]
    plan = []
    for i in range(depth):
        m, j = i % len(mixers), i // len(mixers)
        entries = [(n, j, kind) for n, kind in mixers[m]]
        entries += [("ffn_w_up", i, "col"), ("ffn_conv_w", i, "exact"), ("ffn_w_down", i, "row")]
        plan.append(entries)
    return plan


def _local_loss(sharded, repl, x, positions, loss_target):
    x = x[0]
    loss_target = loss_target[0]
    d = x.shape[-1]
    plan = _layer_plan(sharded)
    lb_table = jnp.cumsum(jax.nn.softmax(repl["hgrn_lb_param"], axis=0), axis=0)
    for i, entries in enumerate(plan):
        shards = tuple(sharded[n][j] for n, j, _ in entries)
        kinds = tuple(kind for _, _, kind in entries)
        full = _make_layer_gather([s.shape for s in shards], kinds, "layer%d" % i)(shards)
        wts = {n: f for (n, _, _), f in zip(entries, full)}
        m, j = i % 4, i // 4
        tag = "l%d" % i
        hn = _norm(x, repl["mix_pre_g"][i], tag + "_mix_pre")
        if m == 0:
            y = _hgrn_block(hn, wts["hgrn_w_in"], wts["hgrn_w_out"], repl["hgrn_norm_g"][j], lb_table[i], tag + "_hgrn")
        elif m == 1:
            y = _swa_block(hn, positions, wts["swa_w_in"], wts["swa_w_out"], repl["swa_sinks"][j], tag + "_swa")
        elif m == 2:
            y = _sconv_block(hn, wts["sc_w_in"], wts["sc_conv_w"], wts["sc_w_out"], tag + "_sc")
        else:
            w_in = wts["fox_w_in"]
            heads = d // FOX_HEAD_DIM
            w_qkvg = jnp.concatenate([w_in[:3 * d], w_in[3 * d + heads:]], axis=0)
            w_f = jnp.pad(w_in[3 * d:3 * d + heads], ((0, 128 - heads), (0, 0)))
            y = _fox_block(hn, w_qkvg, w_f, repl["fox_b_f"][j], wts["fox_w_out"], tag + "_fox")
        x = _norm_add(y, repl["mix_post_g"][i], x, tag + "_mix_post")
        hn = _norm(x, repl["ffn_pre_g"][i], tag + "_ffn_pre")
        y = _ffn_block(hn, wts["ffn_w_up"], wts["ffn_conv_w"], repl["ffn_conv_b"][i], wts["ffn_w_down"], tag + "_ffn")
        x = _norm_add(y, repl["ffn_post_g"][i], x, tag + "_ffn_post")
    err = jnp.square(x - loss_target)
    return 0.5 * jnp.sum(jnp.mean(err, axis=-1))


SHARDED = ["hgrn_w_in", "hgrn_w_out", "swa_w_in", "swa_w_out", "sc_w_in", "sc_conv_w", "sc_w_out",
           "fox_w_in", "fox_w_out", "ffn_w_up", "ffn_conv_w", "ffn_w_down"]
REPLICATED = ["mix_pre_g", "mix_post_g", "ffn_pre_g", "ffn_post_g", "hgrn_norm_g", "hgrn_lb_param",
              "swa_sinks", "fox_b_f", "ffn_conv_b"]
WEIGHTS = ["mix_pre_g", "mix_post_g", "ffn_pre_g", "ffn_post_g", "hgrn_w_in", "hgrn_w_out", "hgrn_norm_g",
           "hgrn_lb_param", "swa_w_in", "swa_w_out", "swa_sinks", "sc_w_in", "sc_conv_w", "sc_w_out",
           "fox_w_in", "fox_b_f", "fox_w_out", "ffn_w_up", "ffn_conv_w", "ffn_conv_b", "ffn_w_down"]


def _sum_replicated(loss, grads):
    parts = [loss.reshape(1)] + [grads[n].reshape(-1) for n in REPLICATED]
    flat = jnp.concatenate(parts)
    n = flat.shape[0]
    cols = 1024
    rows = -(-n // cols)
    rows += (-rows) % 8
    flat = jnp.pad(flat, (0, rows * cols - n)).reshape(rows, cols)
    total = _sum_blocks(_all_gather(flat, "small_gather"), "small_sum").reshape(-1)
    out, off = {}, 1
    for name in REPLICATED:
        size = grads[name].size
        out[name] = total[off:off + size].reshape(grads[name].shape)
        off += size
    return total[0], out


def kernel(x, positions, mix_pre_g, mix_post_g, ffn_pre_g, ffn_post_g, hgrn_w_in, hgrn_w_out, hgrn_norm_g, hgrn_lb_param, swa_w_in, swa_w_out, swa_sinks, sc_w_in, sc_conv_w, sc_w_out, fox_w_in, fox_b_f, fox_w_out, ffn_w_up, ffn_conv_w, ffn_conv_b, ffn_w_down, loss_target, m_mix_pre_g, m_mix_post_g, m_ffn_pre_g, m_ffn_post_g, m_hgrn_w_in, m_hgrn_w_out, m_hgrn_norm_g, m_hgrn_lb_param, m_swa_w_in, m_swa_w_out, m_swa_sinks, m_sc_w_in, m_sc_conv_w, m_sc_w_out, m_fox_w_in, m_fox_b_f, m_fox_w_out, m_ffn_w_up, m_ffn_conv_w, m_ffn_conv_b, m_ffn_w_down, v_mix_pre_g, v_mix_post_g, v_ffn_pre_g, v_ffn_post_g, v_hgrn_w_in, v_hgrn_w_out, v_hgrn_norm_g, v_hgrn_lb_param, v_swa_w_in, v_swa_w_out, v_swa_sinks, v_sc_w_in, v_sc_conv_w, v_sc_w_out, v_fox_w_in, v_fox_b_f, v_fox_w_out, v_ffn_w_up, v_ffn_conv_w, v_ffn_conv_b, v_ffn_w_down):
    given = dict(locals())
    sharded = {n: given[n] for n in SHARDED}
    repl = {n: given[n] for n in REPLICATED}
    loss, (g_sharded, g_repl, grad_x) = jax.value_and_grad(_local_loss, argnums=(0, 1, 2))(
        sharded, repl, x, positions, loss_target)
    loss, g_repl = _sum_replicated(loss, g_repl)
    grads = {**g_sharded, **g_repl}
    delta, new_m, new_v = {}, {}, {}
    for n in WEIGHTS:
        delta[n], new_m[n], new_v[n] = _adamw(given[n], grads[n], given["m_" + n], given["v_" + n], "adamw_" + n)
    return (loss, grad_x, *[grads[n] for n in WEIGHTS], *[delta[n] for n in WEIGHTS],
            *[new_m[n] for n in WEIGHTS], *[new_v[n] for n in WEIGHTS])
```
